```python
import math
import jax, jax.numpy as jnp
from jax import lax
import numpy as np

D_MODEL = 1024
BATCH = 4
SEQ = 8192
DEPTH = 4

N_MIXERS = 3
HEAD_DIM = 64
ROT_DIM = HEAD_DIM // 4
ROPE_THETA = 500000.0
NORM_EPS = 1e-6
D_FF = 4 * D_MODEL

DIFF_HEADS = D_MODEL // (2 * HEAD_DIM)
DIFF_QK_WIDTH = DIFF_HEADS * 2 * HEAD_DIM
DIFF_V_DIM = 2 * HEAD_DIM
DIFF_IN = 2 * DIFF_QK_WIDTH + DIFF_HEADS * DIFF_V_DIM
DENSE_Q_BLOCK = 128

MOBA_HEADS = D_MODEL // HEAD_DIM
MOBA_BLOCK = 256
MOBA_TOPK = 3
MOBA_Q_CHUNK = 32
MOBA_IN = 3 * MOBA_HEADS * HEAD_DIM

SWA_HEADS = D_MODEL // HEAD_DIM
SWA_KV_HEADS = SWA_HEADS // 8
SWA_WINDOW = 128
SWA_Q_BLOCK = SWA_WINDOW
SWA_IN = (SWA_HEADS + 2 * SWA_KV_HEADS) * HEAD_DIM

N_DIFF = (DEPTH + 2) // 3
N_MOBA = (DEPTH + 1) // 3
N_SWA = DEPTH // 3

kernel_name = "hybrid_diff_moba_swa_sink_trunk"


def rms_norm(x, g):
    xf = x.astype(jnp.float32)
    y = xf * lax.rsqrt(jnp.mean(xf * xf, axis=-1, keepdims=True) + NORM_EPS)
    return (y * g.astype(jnp.float32)).astype(x.dtype)


def rope_tables(positions):
    inv = ROPE_THETA ** (-jnp.arange(0, ROT_DIM, 2, dtype=jnp.float32) / ROT_DIM)
    ang = positions.astype(jnp.float32)[..., None] * inv
    return jnp.cos(ang), jnp.sin(ang)


def apply_partial_rope(x, cos, sin):
    half = ROT_DIM // 2
    c = cos[:, :, None, :].astype(x.dtype)
    s = sin[:, :, None, :].astype(x.dtype)
    x1 = x[..., :half]
    x2 = x[..., half:ROT_DIM]
    return jnp.concatenate([x1 * c - x2 * s, x2 * c + x1 * s, x[..., ROT_DIM:]], axis=-1)


def diff_attention(h, w_in, w_out, lam_q1, lam_k1, lam_q2, lam_k2, sub_g, cos, sin, lambda_init):
    B, S, _ = h.shape
    H = DIFF_HEADS
    qkv = h @ w_in
    q, k, v = jnp.split(qkv, [DIFF_QK_WIDTH, 2 * DIFF_QK_WIDTH], axis=-1)
    q = apply_partial_rope(q.reshape(B, S, 2 * H, HEAD_DIM), cos, sin) * (HEAD_DIM ** -0.5)
    k = apply_partial_rope(k.reshape(B, S, 2 * H, HEAD_DIM), cos, sin)
    q = q.reshape(B, S, H, 2, HEAD_DIM)
    k = k.reshape(B, S, H, 2, HEAD_DIM)
    v = v.reshape(B, S, H, DIFF_V_DIM)
    f32 = jnp.float32
    lam = (jnp.exp(jnp.sum(lam_q1.astype(f32) * lam_k1.astype(f32)))
           - jnp.exp(jnp.sum(lam_q2.astype(f32) * lam_k2.astype(f32))) + lambda_init)
    nqb = S // DENSE_Q_BLOCK
    qb = q.reshape(B, nqb, DENSE_Q_BLOCK, H, 2, HEAD_DIM).transpose(1, 0, 2, 3, 4, 5)
    key_pos = jnp.arange(S)

    def block(args):
        qi, blk = args
        q_pos = blk * DENSE_Q_BLOCK + jnp.arange(DENSE_Q_BLOCK)
        s = jnp.einsum('bqhcd,bkhcd->bhcqk', qi, k).astype(f32)
        causal = key_pos[None, :] <= q_pos[:, None]
        s = jnp.where(causal, s, -jnp.inf)
        p = jax.nn.softmax(s, axis=-1)
        p = p[:, :, 0] - lam * p[:, :, 1]
        return jnp.einsum('bhqk,bkhe->bqhe', p.astype(v.dtype), v)

    o = lax.map(block, (qb, jnp.arange(nqb)))
    o = o.transpose(1, 0, 2, 3, 4).reshape(B, S, H, DIFF_V_DIM)
    o = rms_norm(o, sub_g) * (1.0 - lambda_init)
    return o.reshape(B, S, H * DIFF_V_DIM) @ w_out


def moba_attention(h, w_in, w_out, cos, sin):
    B, S, _ = h.shape
    H, D, Bk = MOBA_HEADS, HEAD_DIM, MOBA_BLOCK
    f32 = jnp.float32
    q, k, v = jnp.split(h @ w_in, 3, axis=-1)
    q = apply_partial_rope(q.reshape(B, S, H, D), cos, sin) * (D ** -0.5)
    k = apply_partial_rope(k.reshape(B, S, H, D), cos, sin)
    v = v.reshape(B, S, H, D)
    nb = -(-S // Bk)
    pad = nb * Bk - S
    k = jnp.pad(k, ((0, 0), (0, pad), (0, 0), (0, 0)))
    v = jnp.pad(v, ((0, 0), (0, pad), (0, 0), (0, 0)))
    K = min(MOBA_TOPK, nb)
    kb = k.reshape(B, nb, Bk, H, D).transpose(0, 3, 1, 2, 4)
    vb = v.reshape(B, nb, Bk, H, D).transpose(0, 3, 1, 2, 4)
    k_mean = jnp.mean(kb.astype(f32), axis=3).astype(k.dtype)
    nqc = S // MOBA_Q_CHUNK
    qc = q.reshape(B, nqc, MOBA_Q_CHUNK, H, D).transpose(1, 0, 3, 2, 4)
    b_idx = jnp.arange(B)[:, None, None, None]
    h_idx = jnp.arange(H)[None, :, None, None]
    blk_ids = jnp.arange(nb)
    in_blk = jnp.arange(Bk)

    def chunk(args):
        qi, ci = args
        q_pos = ci * MOBA_Q_CHUNK + jnp.arange(MOBA_Q_CHUNK)
        own = (ci * MOBA_Q_CHUNK) // Bk
        gate = jnp.einsum('bhqd,bhnd->bhqn', qi, k_mean).astype(f32)
        gate = jnp.where(blk_ids < own, gate, -jnp.inf)
        _, sel = lax.top_k(gate, K)
        valid = jnp.arange(K) < own
        ks = kb[b_idx, h_idx, sel]
        vs = vb[b_idx, h_idx, sel]
        s_sel = jnp.einsum('bhqd,bhqnkd->bhqnk', qi, ks).astype(f32)
        s_sel = jnp.where(valid[:, None], s_sel, -jnp.inf).reshape(B, H, MOBA_Q_CHUNK, K * Bk)
        k_own = lax.dynamic_index_in_dim(kb, own, axis=2, keepdims=False)
        v_own = lax.dynamic_index_in_dim(vb, own, axis=2, keepdims=False)
        s_own = jnp.einsum('bhqd,bhkd->bhqk', qi, k_own).astype(f32)
        own_pos = own * Bk + in_blk
        s_own = jnp.where(own_pos[None, :] <= q_pos[:, None], s_own, -jnp.inf)
        p = jax.nn.softmax(jnp.concatenate([s_sel, s_own], axis=-1), axis=-1).astype(v.dtype)
        p_sel = p[..., :K * Bk].reshape(B, H, MOBA_Q_CHUNK, K, Bk)
        p_own = p[..., K * Bk:]
        return (jnp.einsum('bhqnk,bhqnkd->bhqd', p_sel, vs)
                + jnp.einsum('bhqk,bhkd->bhqd', p_own, v_own))

    o = lax.map(chunk, (qc, jnp.arange(nqc)))
    o = o.transpose(1, 0, 3, 2, 4).reshape(B, S, H * D)
    return o @ w_out


def swa_sink_attention(h, w_in, b_in, sinks, w_out, cos, sin):
    B, S, _ = h.shape
    KV, D, Qb = SWA_KV_HEADS, HEAD_DIM, SWA_Q_BLOCK
    G = SWA_HEADS // KV
    f32 = jnp.float32
    qkv = h @ w_in + b_in
    q, k, v = jnp.split(qkv, [SWA_HEADS * D, (SWA_HEADS + KV) * D], axis=-1)
    q = apply_partial_rope(q.reshape(B, S, SWA_HEADS, D), cos, sin) * (D ** -0.5)
    k = apply_partial_rope(k.reshape(B, S, KV, D), cos, sin)
    v = v.reshape(B, S, KV, D)
    nb = S // Qb
    qb = q.reshape(B, nb, Qb, KV, G, D)
    kb = k.reshape(B, nb, Qb, KV, D)
    vb = v.reshape(B, nb, Qb, KV, D)
    pad_k = jnp.zeros_like(kb[:, :1])
    pad_v = jnp.zeros_like(vb[:, :1])
    k2 = jnp.concatenate([jnp.concatenate([pad_k, kb[:, :-1]], axis=1), kb], axis=2)
    v2 = jnp.concatenate([jnp.concatenate([pad_v, vb[:, :-1]], axis=1), vb], axis=2)
    s = jnp.einsum('bnqkgd,bnjkd->bnkgqj', qb, k2).astype(f32)
    qi = jnp.arange(Qb)[:, None] + Qb
    kj = jnp.arange(2 * Qb)[None, :]
    dist = qi - kj
    band = (dist >= 0) & (dist < SWA_WINDOW)
    prev_ok = (jnp.arange(nb)[:, None] > 0) | (jnp.arange(2 * Qb)[None, :] >= Qb)
    mask = band[None, :, :] & prev_ok[:, None, :]
    s = jnp.where(mask[None, :, None, None], s, -jnp.inf)
    sink = sinks.astype(f32).reshape(KV, G)[None, None, :, :, None, None]
    m = jnp.maximum(jnp.max(s, axis=-1, keepdims=True), sink)
    e = jnp.exp(s - m)
    p = (e / (jnp.sum(e, axis=-1, keepdims=True) + jnp.exp(sink - m))).astype(v.dtype)
    o = jnp.einsum('bnkgqj,bnjkd->bnqkgd', p, v2).reshape(B, S, SWA_HEADS * D)
    return o @ w_out


def sqrelu_mlp(h, w_up, w_down):
    a = jnp.maximum(h @ w_up, 0)
    return (a * a) @ w_down


def setup_inputs(seed: int = 0) -> dict:
    key = jax.random.key(seed)
    ks = jax.random.split(key, 24)
    f32 = jnp.float32
    nrm = lambda k, shape, scale: jax.random.normal(k, shape, f32) * scale
    x = jax.random.normal(ks[0], (BATCH, SEQ, D_MODEL), f32)
    positions = jnp.broadcast_to(jnp.arange(SEQ, dtype=jnp.int32)[None, :], (BATCH, SEQ))
    return {
        "x": x,
        "positions": positions,
        "attn_norm_g": 1.0 + nrm(ks[1], (DEPTH, D_MODEL), 0.02),
        "mlp_norm_g": 1.0 + nrm(ks[2], (DEPTH, D_MODEL), 0.02),
        "diff_w_in": nrm(ks[3], (N_DIFF, D_MODEL, DIFF_IN), D_MODEL ** -0.5),
        "diff_w_out": nrm(ks[4], (N_DIFF, DIFF_HEADS * DIFF_V_DIM, D_MODEL), (DIFF_HEADS * DIFF_V_DIM) ** -0.5),
        "diff_lam_q1": nrm(ks[5], (N_DIFF, HEAD_DIM), 0.1),
        "diff_lam_k1": nrm(ks[6], (N_DIFF, HEAD_DIM), 0.1),
        "diff_lam_q2": nrm(ks[7], (N_DIFF, HEAD_DIM), 0.1),
        "diff_lam_k2": nrm(ks[8], (N_DIFF, HEAD_DIM), 0.1),
        "diff_subln_g": 1.0 + nrm(ks[9], (N_DIFF, DIFF_V_DIM), 0.02),
        "moba_w_in": nrm(ks[10], (N_MOBA, D_MODEL, MOBA_IN), D_MODEL ** -0.5),
        "moba_w_out": nrm(ks[11], (N_MOBA, MOBA_HEADS * HEAD_DIM, D_MODEL), (MOBA_HEADS * HEAD_DIM) ** -0.5),
        "swa_w_in": nrm(ks[12], (N_SWA, D_MODEL, SWA_IN), D_MODEL ** -0.5),
        "swa_b_in": nrm(ks[13], (N_SWA, SWA_IN), 0.02),
        "swa_sinks": nrm(ks[14], (N_SWA, SWA_HEADS), 0.5),
        "swa_w_out": nrm(ks[15], (N_SWA, SWA_HEADS * HEAD_DIM, D_MODEL), (SWA_HEADS * HEAD_DIM) ** -0.5),
        "mlp_w_up": nrm(ks[16], (DEPTH, D_MODEL, D_FF), D_MODEL ** -0.5),
        "mlp_w_down": nrm(ks[17], (DEPTH, D_FF, D_MODEL), 0.5 * D_FF ** -0.5),
        "final_norm_g": 1.0 + nrm(ks[18], (D_MODEL,), 0.02),
    }


def reference(x, positions, attn_norm_g, mlp_norm_g, diff_w_in, diff_w_out, diff_lam_q1, diff_lam_k1,
              diff_lam_q2, diff_lam_k2, diff_subln_g, moba_w_in, moba_w_out, swa_w_in, swa_b_in,
              swa_sinks, swa_w_out, mlp_w_up, mlp_w_down, final_norm_g):
    cos, sin = rope_tables(positions)
    h = x
    for i in range(DEPTH):
        mixer = i % N_MIXERS
        slot = i // N_MIXERS
        a = rms_norm(h, attn_norm_g[i])
        if mixer == 0:
            lambda_init = 0.8 - 0.6 * math.exp(-0.3 * i)
            y = diff_attention(a, diff_w_in[slot], diff_w_out[slot], diff_lam_q1[slot], diff_lam_k1[slot],
                               diff_lam_q2[slot], diff_lam_k2[slot], diff_subln_g[slot], cos, sin, lambda_init)
        elif mixer == 1:
            y = moba_attention(a, moba_w_in[slot], moba_w_out[slot], cos, sin)
        else:
            y = swa_sink_attention(a, swa_w_in[slot], swa_b_in[slot], swa_sinks[slot], swa_w_out[slot], cos, sin)
        h = h + y
        h = h + sqrelu_mlp(rms_norm(h, mlp_norm_g[i]), mlp_w_up[i], mlp_w_down[i])
    return rms_norm(h, final_norm_g)
```

```python
import functools
import math

import jax
import jax.numpy as jnp
from jax import lax
from jax.experimental import pallas as pl
from jax.experimental.pallas import tpu as pltpu

F32 = jnp.float32
BF16 = jnp.bfloat16

D_MODEL = 1024
HEAD_DIM = 64
ROT_DIM = HEAD_DIM // 4
ROT_HALF = ROT_DIM // 2
ROPE_THETA = 500000.0
NORM_EPS = 1e-6
D_FF = 4 * D_MODEL
Q_SCALE = HEAD_DIM ** -0.5
N_MIXERS = 3

LANES = 128
PAIRS = D_MODEL // LANES
MOBA_BLOCK = 256
MOBA_TOPK = 3
SWA_WINDOW = 128
SWA_KV_HEADS = 2

ATTN_BLOCK = 256
SWA_Q_BLOCK = 256
PROJ_ROWS = 512
FF_CHUNK = 1024
VMEM_LIMIT = 56 * 1024 * 1024

NEG_INF = float("-inf")


def _nt(a, b, precision=None):
    return lax.dot_general(a, b, (((1,), (1,)), ((), ())), preferred_element_type=F32, precision=precision)


def _nn(a, b):
    return jnp.dot(a, b, preferred_element_type=F32)


def _rms(x, g):
    return x * lax.rsqrt(jnp.mean(x * x, axis=-1, keepdims=True) + NORM_EPS) * g


def _lo_hi(x):
    lane = lax.broadcasted_iota(jnp.int32, (1, LANES), 1)
    zero = jnp.zeros_like(x)
    return jnp.where(lane < HEAD_DIM, x, zero), jnp.where(lane < HEAD_DIM, zero, x)


def _rope_table_kernel(pos_ref, inv_ref, c_ref, sa_ref, sb_ref):
    ang = pos_ref[...].astype(F32) * inv_ref[...]
    lane = lax.broadcasted_iota(jnp.int32, ang.shape, 1) % HEAD_DIM
    cos = jnp.cos(ang)
    sin = jnp.sin(ang)
    c_ref[...] = jnp.where(lane < ROT_DIM, cos, 1.0)
    sa_ref[...] = jnp.where(lane < ROT_HALF, -sin, 0.0)
    sb_ref[...] = jnp.where((lane >= ROT_HALF) & (lane < ROT_DIM), sin, 0.0)


def _rope_tables(positions):
    t = positions.size
    rows = 1024
    inv = ROPE_THETA ** (-jnp.arange(0, ROT_DIM, 2, dtype=F32) / ROT_DIM)
    lane = jnp.arange(LANES) % HEAD_DIM
    inv_lane = jnp.where(lane < ROT_DIM, inv[lane % ROT_HALF], 0.0).reshape(1, LANES)
    tab = jax.ShapeDtypeStruct((t, LANES), F32)
    spec = pl.BlockSpec((rows, LANES), lambda i: (i, 0))
    return pl.pallas_call(
        _rope_table_kernel,
        grid=(t // rows,),
        in_specs=[pl.BlockSpec((rows, 1), lambda i: (i, 0)), pl.BlockSpec((1, LANES), lambda i: (0, 0))],
        out_specs=[spec, spec, spec],
        out_shape=[tab, tab, tab],
    )(positions.reshape(t, 1), inv_lane)


def _proj_kernel(*refs, nq, nk, kv_block, has_bias, with_kmean):
    it = iter(refs)
    h_ref, g_ref, wqk_ref, wvt_ref = next(it), next(it), next(it), next(it)
    bqk_ref = next(it) if has_bias else None
    bv_ref = next(it) if has_bias else None
    c_ref, sa_ref, sb_ref = next(it), next(it), next(it)
    q_ref, k_ref, vt_ref = next(it), next(it), next(it)
    km_ref = next(it) if with_kmean else None

    rows = h_ref.shape[0]
    xn = _rms(h_ref[...], g_ref[...]).astype(BF16)
    cos, sin_a, sin_b = c_ref[...], sa_ref[...], sb_ref[...]

    for c in range((nq + nk) // LANES):
        col = c * LANES
        y = _nn(xn, wqk_ref[:, col:col + LANES])
        if has_bias:
            y = y + bqk_ref[:, col:col + LANES]
        y = y * cos + pltpu.roll(y, LANES - ROT_HALF, 1) * sin_a + pltpu.roll(y, ROT_HALF, 1) * sin_b
        if col < nq:
            q_ref[:, col:col + LANES] = (y * Q_SCALE).astype(q_ref.dtype)
        else:
            kc = col - nq
            k_ref[:, kc:kc + LANES] = y.astype(k_ref.dtype)
            if with_kmean:
                for r in range(rows // MOBA_BLOCK):
                    blk = y[r * MOBA_BLOCK:(r + 1) * MOBA_BLOCK]
                    km_ref[0, r:r + 1, kc:kc + LANES] = jnp.sum(blk, axis=0, keepdims=True) * (1.0 / MOBA_BLOCK)

    vt = _nt(wvt_ref[...], xn)
    if has_bias:
        vt = vt + bv_ref[...]
    for r in range(rows // kv_block):
        vt_ref[0, r] = vt[:, r * kv_block:(r + 1) * kv_block].astype(vt_ref.dtype)


def _project(h, g, wqk, wvt, tables, *, batch, nq, kv_block, q_dtype=BF16, bias=None, with_kmean=False):
    t = h.shape[0]
    rows = PROJ_ROWS
    nk = wqk.shape[1] - nq
    nv = wvt.shape[0]
    seq = t // batch
    tiles_per_seq = seq // rows
    chunks = rows // kv_block

    const = lambda i: (0, 0)
    row = lambda i: (i, 0)
    in_specs = [pl.BlockSpec((rows, D_MODEL), row), pl.BlockSpec((1, D_MODEL), const),
                pl.BlockSpec(wqk.shape, const), pl.BlockSpec(wvt.shape, const)]
    args = [h, g.reshape(1, D_MODEL), wqk, wvt]
    if bias is not None:
        in_specs += [pl.BlockSpec((1, nq + nk), const), pl.BlockSpec((nv, 1), const)]
        args += [bias[0].reshape(1, nq + nk), bias[1].reshape(nv, 1)]
    in_specs += [pl.BlockSpec((rows, LANES), row)] * 3
    args += list(tables)

    out_shape = [jax.ShapeDtypeStruct((t, nq), q_dtype), jax.ShapeDtypeStruct((t, nk), BF16),
                 jax.ShapeDtypeStruct((batch, seq // kv_block, nv, kv_block), BF16)]
    out_specs = [pl.BlockSpec((rows, nq), row), pl.BlockSpec((rows, nk), row),
                 pl.BlockSpec((1, chunks, nv, kv_block),
                              lambda i: (i // tiles_per_seq, i % tiles_per_seq, 0, 0))]
    if with_kmean:
        out_shape.append(jax.ShapeDtypeStruct((t // rows, rows // MOBA_BLOCK, nk), F32))
        out_specs.append(pl.BlockSpec((1, rows // MOBA_BLOCK, nk), lambda i: (i, 0, 0)))

    return pl.pallas_call(
        functools.partial(_proj_kernel, nq=nq, nk=nk, kv_block=kv_block, has_bias=bias is not None,
                          with_kmean=with_kmean),
        grid=(t // rows,),
        in_specs=in_specs, out_specs=out_specs, out_shape=out_shape,
        compiler_params=pltpu.CompilerParams(dimension_semantics=("arbitrary",), vmem_limit_bytes=VMEM_LIMIT),
    )(*args)


def _first_tile(s, vt, m_ref, l_ref, acc_ref):
    m = jnp.max(s, axis=0, keepdims=True)
    p = jnp.exp(s - m)
    m_ref[...] = m
    l_ref[...] = jnp.sum(p, axis=0, keepdims=True)
    acc_ref[...] = _nn(vt, p.astype(BF16))


def _next_tile(s, vt, m_ref, l_ref, acc_ref):
    m_old = m_ref[...]
    m = jnp.maximum(m_old, jnp.max(s, axis=0, keepdims=True))
    alpha = jnp.exp(m_old - m)
    p = jnp.exp(s - m)
    m_ref[...] = m
    l_ref[...] = alpha * l_ref[...] + jnp.sum(p, axis=0, keepdims=True)
    acc_ref[...] = alpha * acc_ref[...] + _nn(vt, p.astype(BF16))


def _causal_tile_mask(n):
    key = lax.broadcasted_iota(jnp.int32, (n, n), 0)
    qry = lax.broadcasted_iota(jnp.int32, (n, n), 1)
    return key <= qry


def _diff_attn_kernel(lq1_ref, lk1_ref, lq2_ref, lk2_ref, subg_ref, q_ref, k_ref, vt_ref, o_ref,
                      m1_ref, l1_ref, acc1_ref, m2_ref, l2_ref, acc2_ref, *, lambda_init):
    i = pl.program_id(2)
    blk = ATTN_BLOCK
    q = q_ref[...]

    def scores(j):
        k_lo, k_hi = _lo_hi(k_ref[pl.ds(pl.multiple_of(j * blk, blk), blk), :])
        return _nt(k_lo, q), _nt(k_hi, q)

    s1, s2 = scores(i)
    keep = _causal_tile_mask(blk)
    vt = vt_ref[0, i]
    _first_tile(jnp.where(keep, s1, NEG_INF), vt, m1_ref, l1_ref, acc1_ref)
    _first_tile(jnp.where(keep, s2, NEG_INF), vt, m2_ref, l2_ref, acc2_ref)

    def body(j, carry):
        s1, s2 = scores(j)
        vt = vt_ref[0, j]
        _next_tile(s1, vt, m1_ref, l1_ref, acc1_ref)
        _next_tile(s2, vt, m2_ref, l2_ref, acc2_ref)
        return carry

    lax.fori_loop(0, i, body, 0)

    lam = (jnp.exp(jnp.sum(lq1_ref[...] * lk1_ref[...], axis=1, keepdims=True))
           - jnp.exp(jnp.sum(lq2_ref[...] * lk2_ref[...], axis=1, keepdims=True)) + lambda_init)
    o = acc1_ref[...] / l1_ref[...] - lam * (acc2_ref[...] / l2_ref[...])
    o = o * lax.rsqrt(jnp.mean(o * o, axis=0, keepdims=True) + NORM_EPS) * subg_ref[...]
    o_ref[...] = (o * (1.0 - lambda_init)).T.astype(o_ref.dtype)


def _diff_attention(q, k, vt, lam_params, sub_g, *, batch, lambda_init):
    t = q.shape[0]
    seq = t // batch
    blk = ATTN_BLOCK
    nqb = seq // blk
    small = pl.BlockSpec((1, HEAD_DIM), lambda b, h, i: (0, 0))
    stat = pltpu.VMEM((1, blk), F32)
    acc = pltpu.VMEM((LANES, blk), F32)
    return pl.pallas_call(
        functools.partial(_diff_attn_kernel, lambda_init=lambda_init),
        grid=(batch, PAIRS, nqb),
        in_specs=[small, small, small, small,
                  pl.BlockSpec((LANES, 1), lambda b, h, i: (0, 0)),
                  pl.BlockSpec((blk, LANES), lambda b, h, i: (b * nqb + i, h)),
                  pl.BlockSpec((seq, LANES), lambda b, h, i: (b, h)),
                  pl.BlockSpec((1, nqb, LANES, blk), lambda b, h, i: (b, 0, h, 0))],
        out_specs=pl.BlockSpec((blk, LANES), lambda b, h, i: (b * nqb + i, h)),
        out_shape=jax.ShapeDtypeStruct((t, D_MODEL), BF16),
        scratch_shapes=[stat, stat, acc, stat, stat, acc],
        compiler_params=pltpu.CompilerParams(dimension_semantics=("arbitrary",) * 3, vmem_limit_bytes=VMEM_LIMIT),
    )(*[p.reshape(1, HEAD_DIM) for p in lam_params], sub_g.reshape(LANES, 1), q, k, vt)


def _moba_select(gate, own):
    nb = gate.shape[0]
    blk_id = lax.broadcasted_iota(jnp.int32, gate.shape, 0)
    g = jnp.where(blk_id < own, gate, NEG_INF)
    sel = jnp.zeros(gate.shape, F32)
    for t in range(MOBA_TOPK):
        best = jnp.max(g, axis=0, keepdims=True)
        first = jnp.min(jnp.where(g == best, blk_id, nb), axis=0, keepdims=True)
        hit = blk_id == first
        sel = jnp.maximum(sel, jnp.where(hit, jnp.where(t < own, 1.0, 0.0), 0.0))
        g = jnp.where(hit, NEG_INF, g)
    return sel


def _moba_attn_kernel(q_ref, k_ref, vt_ref, km_ref, o_ref,
                      sel1_ref, sel2_ref, m1_ref, l1_ref, acc1_ref, m2_ref, l2_ref, acc2_ref):
    i = pl.program_id(2)
    blk = ATTN_BLOCK
    qf = q_ref[...]
    q = qf.astype(BF16)

    km_lo, km_hi = _lo_hi(km_ref[0])
    sel1_ref[...] = _moba_select(_nt(km_lo, qf, lax.Precision.HIGHEST), i)
    sel2_ref[...] = _moba_select(_nt(km_hi, qf, lax.Precision.HIGHEST), i)

    def scores(j):
        k_lo, k_hi = _lo_hi(k_ref[pl.ds(pl.multiple_of(j * blk, blk), blk), :])
        return _nt(k_lo, q), _nt(k_hi, q)

    s1, s2 = scores(i)
    keep = _causal_tile_mask(blk)
    vt = vt_ref[0, i]
    _first_tile(jnp.where(keep, s1, NEG_INF), vt[:HEAD_DIM], m1_ref, l1_ref, acc1_ref)
    _first_tile(jnp.where(keep, s2, NEG_INF), vt[HEAD_DIM:], m2_ref, l2_ref, acc2_ref)

    def body(j, carry):
        s1, s2 = scores(j)
        vt = vt_ref[0, j]
        _next_tile(jnp.where(sel1_ref[pl.ds(j, 1), :] > 0.0, s1, NEG_INF), vt[:HEAD_DIM], m1_ref, l1_ref, acc1_ref)
        _next_tile(jnp.where(sel2_ref[pl.ds(j, 1), :] > 0.0, s2, NEG_INF), vt[HEAD_DIM:], m2_ref, l2_ref, acc2_ref)
        return carry

    lax.fori_loop(0, i, body, 0)

    o = jnp.concatenate([acc1_ref[...] / l1_ref[...], acc2_ref[...] / l2_ref[...]], axis=0)
    o_ref[...] = o.T.astype(o_ref.dtype)


def _moba_attention(q, k, vt, kmean, *, batch):
    t = q.shape[0]
    seq = t // batch
    blk = ATTN_BLOCK
    nqb = seq // blk
    stat = pltpu.VMEM((1, blk), F32)
    acc = pltpu.VMEM((HEAD_DIM, blk), F32)
    sel = pltpu.VMEM((nqb, blk), F32)
    return pl.pallas_call(
        _moba_attn_kernel,
        grid=(batch, PAIRS, nqb),
        in_specs=[pl.BlockSpec((blk, LANES), lambda b, h, i: (b * nqb + i, h)),
                  pl.BlockSpec((seq, LANES), lambda b, h, i: (b, h)),
                  pl.BlockSpec((1, nqb, LANES, blk), lambda b, h, i: (b, 0, h, 0)),
                  pl.BlockSpec((1, nqb, LANES), lambda b, h, i: (b, 0, h))],
        out_specs=pl.BlockSpec((blk, LANES), lambda b, h, i: (b * nqb + i, h)),
        out_shape=jax.ShapeDtypeStruct((t, D_MODEL), BF16),
        scratch_shapes=[sel, sel, stat, stat, acc, stat, stat, acc],
        compiler_params=pltpu.CompilerParams(dimension_semantics=("arbitrary",) * 3, vmem_limit_bytes=VMEM_LIMIT),
    )(q, k, vt, kmean)


def _swa_attn_kernel(sink_ref, q_ref, k_ref, vt_ref, o_ref):
    h = pl.program_id(1)
    i = pl.program_id(2)
    qb = SWA_Q_BLOCK
    w = SWA_WINDOW
    q = q_ref[...]
    q0 = pl.multiple_of(i * qb, qb)
    prev0 = pl.multiple_of(jnp.maximum(q0 - w, 0), w)
    k_own = _lo_hi(k_ref[pl.ds(q0, qb), :])
    k_prev = _lo_hi(k_ref[pl.ds(prev0, w), :])

    key = lax.broadcasted_iota(jnp.int32, (qb, qb), 0)
    qry = lax.broadcasted_iota(jnp.int32, (qb, qb), 1)
    own_keep = (key <= qry) & (qry - key < w)
    pkey = lax.broadcasted_iota(jnp.int32, (w, qb), 0)
    pqry = lax.broadcasted_iota(jnp.int32, (w, qb), 1)
    prev_keep = pkey > pqry + jnp.where(i > 0, 0, w)

    rows = (h // (PAIRS // SWA_KV_HEADS)) * HEAD_DIM
    vt_prev = vt_ref[0, jnp.maximum(2 * i - 1, 0), pl.ds(pl.multiple_of(rows, HEAD_DIM), HEAD_DIM), :]
    vt_own = jnp.concatenate(
        [vt_ref[0, 2 * i + c, pl.ds(pl.multiple_of(rows, HEAD_DIM), HEAD_DIM), :] for c in range(qb // w)], axis=1)

    outs = []
    for half in range(2):
        sink = sink_ref[2 * h + half]
        s_own = jnp.where(own_keep, _nt(k_own[half], q), NEG_INF)
        s_prev = jnp.where(prev_keep, _nt(k_prev[half], q), NEG_INF)
        m = jnp.maximum(jnp.maximum(jnp.max(s_own, axis=0, keepdims=True), jnp.max(s_prev, axis=0, keepdims=True)),
                        sink)
        e_own = jnp.exp(s_own - m)
        e_prev = jnp.exp(s_prev - m)
        denom = jnp.sum(e_own, axis=0, keepdims=True) + jnp.sum(e_prev, axis=0, keepdims=True) + jnp.exp(sink - m)
        inv = 1.0 / denom
        acc = _nn(vt_own, (e_own * inv).astype(BF16)) + _nn(vt_prev, (e_prev * inv).astype(BF16))
        outs.append(acc)
    o_ref[...] = jnp.concatenate(outs, axis=0).T.astype(o_ref.dtype)


def _swa_attention(q, kdup, vt, sinks, *, batch):
    t = q.shape[0]
    seq = t // batch
    qb = SWA_Q_BLOCK
    nqb = seq // qb
    per_kv = PAIRS // SWA_KV_HEADS
    return pl.pallas_call(
        _swa_attn_kernel,
        grid=(batch, PAIRS, nqb),
        in_specs=[pl.BlockSpec(memory_space=pltpu.SMEM),
                  pl.BlockSpec((qb, LANES), lambda b, h, i: (b * nqb + i, h)),
                  pl.BlockSpec((seq, LANES), lambda b, h, i: (b, h // per_kv)),
                  pl.BlockSpec((1, seq // SWA_WINDOW, LANES, SWA_WINDOW), lambda b, h, i: (b, 0, 0, 0))],
        out_specs=pl.BlockSpec((qb, LANES), lambda b, h, i: (b * nqb + i, h)),
        out_shape=jax.ShapeDtypeStruct((t, D_MODEL), BF16),
        compiler_params=pltpu.CompilerParams(dimension_semantics=("arbitrary",) * 3, vmem_limit_bytes=VMEM_LIMIT),
    )(sinks, q, kdup, vt)


def _post_kernel(*refs, final):
    h_ref, o_ref, wout_ref, g_ref, wup_ref, wdn_ref = refs[:6]
    fg_ref = refs[6] if final else None
    out_ref = refs[-1]
    h1 = h_ref[...] + _nn(o_ref[...], wout_ref[...])
    xn = _rms(h1, g_ref[...]).astype(BF16)
    acc = h1
    for c in range(D_FF // FF_CHUNK):
        lo = c * FF_CHUNK
        a = jnp.maximum(_nn(xn, wup_ref[:, lo:lo + FF_CHUNK]), 0.0)
        acc = acc + _nn((a * a).astype(BF16), wdn_ref[lo:lo + FF_CHUNK, :])
    if final:
        acc = _rms(acc, fg_ref[...])
    out_ref[...] = acc


def _post(h, o, w_out, g, w_up, w_down, final_g=None):
    t = h.shape[0]
    rows = PROJ_ROWS
    const = lambda i: (0, 0)
    row = lambda i: (i, 0)
    in_specs = [pl.BlockSpec((rows, D_MODEL), row), pl.BlockSpec((rows, D_MODEL), row),
                pl.BlockSpec(w_out.shape, const), pl.BlockSpec((1, D_MODEL), const),
                pl.BlockSpec(w_up.shape, const), pl.BlockSpec(w_down.shape, const)]
    args = [h, o, w_out, g.reshape(1, D_MODEL), w_up, w_down]
    if final_g is not None:
        in_specs.append(pl.BlockSpec((1, D_MODEL), const))
        args.append(final_g.reshape(1, D_MODEL))
    return pl.pallas_call(
        functools.partial(_post_kernel, final=final_g is not None),
        grid=(t // rows,),
        in_specs=in_specs,
        out_specs=pl.BlockSpec((rows, D_MODEL), row),
        out_shape=jax.ShapeDtypeStruct((t, D_MODEL), F32),
        compiler_params=pltpu.CompilerParams(dimension_semantics=("arbitrary",), vmem_limit_bytes=VMEM_LIMIT),
    )(*args)


def kernel(x, positions, attn_norm_g, mlp_norm_g, diff_w_in, diff_w_out, diff_lam_q1, diff_lam_k1, diff_lam_q2,
           diff_lam_k2, diff_subln_g, moba_w_in, moba_w_out, swa_w_in, swa_b_in, swa_sinks, swa_w_out, mlp_w_up,
           mlp_w_down, final_norm_g):
    batch, seq, _ = x.shape
    depth = attn_norm_g.shape[0]
    tables = _rope_tables(positions)
    h = x.reshape(batch * seq, D_MODEL)
    qk = 2 * D_MODEL

    for i in range(depth):
        mixer, slot = i % N_MIXERS, i // N_MIXERS
        if mixer == 0:
            w_in = diff_w_in[slot]
            q, k, vt = _project(h, attn_norm_g[i], w_in[:, :qk].astype(BF16), w_in[:, qk:].T.astype(BF16), tables,
                                batch=batch, nq=D_MODEL, kv_block=ATTN_BLOCK)
            o = _diff_attention(q, k, vt,
                                (diff_lam_q1[slot], diff_lam_k1[slot], diff_lam_q2[slot], diff_lam_k2[slot]),
                                diff_subln_g[slot], batch=batch, lambda_init=0.8 - 0.6 * math.exp(-0.3 * i))
            w_out = diff_w_out[slot]
        elif mixer == 1:
            w_in = moba_w_in[slot]
            q, k, vt, kmean = _project(h, attn_norm_g[i], w_in[:, :qk].astype(BF16), w_in[:, qk:].T.astype(BF16),
                                       tables, batch=batch, nq=D_MODEL, kv_block=ATTN_BLOCK, q_dtype=F32,
                                       with_kmean=True)
            o = _moba_attention(q, k, vt, kmean.reshape(batch, seq // MOBA_BLOCK, D_MODEL), batch=batch)
            w_out = moba_w_out[slot]
        else:
            w_in, b_in = swa_w_in[slot], swa_b_in[slot]
            dup = lambda a: jnp.concatenate(
                [a[..., D_MODEL + kv * HEAD_DIM:D_MODEL + (kv + 1) * HEAD_DIM]
                 for kv in range(SWA_KV_HEADS) for _ in range(2)], axis=-1)
            v0 = D_MODEL + SWA_KV_HEADS * HEAD_DIM
            wqk = jnp.concatenate([w_in[:, :D_MODEL], dup(w_in)], axis=1).astype(BF16)
            bqk = jnp.concatenate([b_in[:D_MODEL], dup(b_in)])
            q, k, vt = _project(h, attn_norm_g[i], wqk, w_in[:, v0:].T.astype(BF16), tables, batch=batch,
                                nq=D_MODEL, kv_block=SWA_WINDOW, bias=(bqk, b_in[v0:]))
            o = _swa_attention(q, k, vt, swa_sinks[slot], batch=batch)
            w_out = swa_w_out[slot]
        h = _post(h, o, w_out.astype(BF16), mlp_norm_g[i], mlp_w_up[i].astype(BF16), mlp_w_down[i].astype(BF16),
                  final_norm_g if i == depth - 1 else None)
    return h.reshape(batch, seq, D_MODEL)
```

```python
import functools
import math

import jax
import jax.numpy as jnp
from jax import lax
from jax.experimental import pallas as pl
from jax.experimental.pallas import tpu as pltpu

F32 = jnp.float32
BF16 = jnp.bfloat16

D_MODEL = 1024
HEAD_DIM = 64
ROT_DIM = HEAD_DIM // 4
ROT_HALF = ROT_DIM // 2
ROPE_THETA = 500000.0
NORM_EPS = 1e-6
D_FF = 4 * D_MODEL
Q_SCALE = HEAD_DIM ** -0.5
LOG2_E = math.log2(math.e)
N_MIXERS = 3

LANES = 128
BF16_ROWS = 16
PAIRS = D_MODEL // LANES
MOBA_BLOCK = 256
MOBA_TOPK = 3
SWA_WINDOW = 128
SWA_KV_HEADS = 2

ATTN_BLOCK = 512
KEY_CHUNK = 256
KEY_CHUNKS = ATTN_BLOCK // KEY_CHUNK
HEADS_PER_STEP = 1
SWA_Q_BLOCK = 256
PROJ_ROWS = 512
FF_CHUNK = 1024
VMEM_LIMIT = 56 * 1024 * 1024

NEG_INF = float("-inf")
M_INIT = -1e30


def _nt(a, b, precision=None):
    return lax.dot_general(a, b, (((1,), (1,)), ((), ())), preferred_element_type=F32, precision=precision)


def _nn(a, b):
    return jnp.dot(a, b, preferred_element_type=F32)


def _rms(x, g):
    return x * lax.rsqrt(jnp.mean(x * x, axis=-1, keepdims=True) + NORM_EPS) * g


def _lo_hi(x):
    lane = lax.broadcasted_iota(jnp.int32, (1, LANES), 1)
    zero = jnp.zeros_like(x)
    return jnp.where(lane < HEAD_DIM, x, zero), jnp.where(lane < HEAD_DIM, zero, x)


def _rope_table_kernel(pos_ref, inv_ref, c_ref, sa_ref, sb_ref):
    ang = pos_ref[...].astype(F32) * inv_ref[...]
    lane = lax.broadcasted_iota(jnp.int32, ang.shape, 1) % HEAD_DIM
    cos = jnp.cos(ang)
    sin = jnp.sin(ang)
    c_ref[...] = jnp.where(lane < ROT_DIM, cos, 1.0)
    sa_ref[...] = jnp.where(lane < ROT_HALF, -sin, 0.0)
    sb_ref[...] = jnp.where((lane >= ROT_HALF) & (lane < ROT_DIM), sin, 0.0)


def _rope_tables(positions):
    t = positions.size
    rows = 1024
    inv = ROPE_THETA ** (-jnp.arange(0, ROT_DIM, 2, dtype=F32) / ROT_DIM)
    lane = jnp.arange(LANES) % HEAD_DIM
    inv_lane = jnp.where(lane < ROT_DIM, inv[lane % ROT_HALF], 0.0).reshape(1, LANES)
    tab = jax.ShapeDtypeStruct((t, LANES), F32)
    spec = pl.BlockSpec((rows, LANES), lambda i: (i, 0))
    return pl.pallas_call(
        _rope_table_kernel,
        grid=(t // rows,),
        in_specs=[pl.BlockSpec((rows, 1), lambda i: (i, 0)), pl.BlockSpec((1, LANES), lambda i: (0, 0))],
        out_specs=[spec, spec, spec],
        out_shape=[tab, tab, tab],
    )(positions.reshape(t, 1), inv_lane)


def _proj_kernel(*refs, nq, nk, kv_block, q_scale, v_group, has_bias, with_kmean):
    it = iter(refs)
    h_ref, g_ref, wqk_ref, wvt_ref = next(it), next(it), next(it), next(it)
    bqk_ref = next(it) if has_bias else None
    bv_ref = next(it) if has_bias else None
    c_ref, sa_ref, sb_ref = next(it), next(it), next(it)
    q_ref, k_ref, vt_ref = next(it), next(it), next(it)
    km_ref = next(it) if with_kmean else None

    rows = h_ref.shape[0]
    xn = _rms(h_ref[...], g_ref[...]).astype(BF16)
    cos, sin_a, sin_b = c_ref[...], sa_ref[...], sb_ref[...]

    for c in range((nq + nk) // LANES):
        col = c * LANES
        y = _nn(xn, wqk_ref[:, col:col + LANES])
        if has_bias:
            y = y + bqk_ref[:, col:col + LANES]
        y = y * cos + pltpu.roll(y, LANES - ROT_HALF, 1) * sin_a + pltpu.roll(y, ROT_HALF, 1) * sin_b
        if col < nq:
            q_ref[:, col:col + LANES] = (y * q_scale).astype(q_ref.dtype)
        else:
            kc = col - nq
            k_ref[:, kc:kc + LANES] = y.astype(k_ref.dtype)
            if with_kmean:
                for r in range(rows // MOBA_BLOCK):
                    blk = y[r * MOBA_BLOCK:(r + 1) * MOBA_BLOCK]
                    km_ref[0, r:r + 1, kc:kc + LANES] = jnp.sum(blk, axis=0, keepdims=True) * (1.0 / MOBA_BLOCK)

    vt = _nt(wvt_ref[...], xn)
    if has_bias:
        vt = vt + bv_ref[...]
    nv = vt.shape[0]
    for r in range(rows // kv_block):
        blk = vt[:, r * kv_block:(r + 1) * kv_block].astype(vt_ref.dtype)
        if v_group is None:
            vt_ref[0, r] = blk
        else:
            stride = v_group + BF16_ROWS
            for g in range(nv // v_group):
                vt_ref[0, r, g * stride:g * stride + v_group] = blk[g * v_group:(g + 1) * v_group]
                vt_ref[0, r, g * stride + v_group:(g + 1) * stride] = jnp.ones((BF16_ROWS, kv_block), vt_ref.dtype)


def _project(h, g, wqk, wvt, tables, *, batch, nq, kv_block, q_scale, v_group=None, q_dtype=BF16, bias=None,
             with_kmean=False):
    t = h.shape[0]
    rows = PROJ_ROWS
    nk = wqk.shape[1] - nq
    nv = wvt.shape[0]
    nv_out = nv if v_group is None else nv // v_group * (v_group + BF16_ROWS)
    seq = t // batch
    tiles_per_seq = seq // rows
    chunks = rows // kv_block

    const = lambda i: (0, 0)
    row = lambda i: (i, 0)
    in_specs = [pl.BlockSpec((rows, D_MODEL), row), pl.BlockSpec((1, D_MODEL), const),
                pl.BlockSpec(wqk.shape, const), pl.BlockSpec(wvt.shape, const)]
    args = [h, g.reshape(1, D_MODEL), wqk, wvt]
    if bias is not None:
        in_specs += [pl.BlockSpec((1, nq + nk), const), pl.BlockSpec((nv, 1), const)]
        args += [bias[0].reshape(1, nq + nk), bias[1].reshape(nv, 1)]
    in_specs += [pl.BlockSpec((rows, LANES), row)] * 3
    args += list(tables)

    out_shape = [jax.ShapeDtypeStruct((t, nq), q_dtype), jax.ShapeDtypeStruct((t, nk), BF16),
                 jax.ShapeDtypeStruct((batch, seq // kv_block, nv_out, kv_block), BF16)]
    out_specs = [pl.BlockSpec((rows, nq), row), pl.BlockSpec((rows, nk), row),
                 pl.BlockSpec((1, chunks, nv_out, kv_block),
                              lambda i: (i // tiles_per_seq, i % tiles_per_seq, 0, 0))]
    if with_kmean:
        out_shape.append(jax.ShapeDtypeStruct((t // rows, rows // MOBA_BLOCK, nk), F32))
        out_specs.append(pl.BlockSpec((1, rows // MOBA_BLOCK, nk), lambda i: (i, 0, 0)))

    return pl.pallas_call(
        functools.partial(_proj_kernel, nq=nq, nk=nk, kv_block=kv_block, q_scale=q_scale, v_group=v_group,
                          has_bias=bias is not None, with_kmean=with_kmean),
        grid=(t // rows,),
        in_specs=in_specs, out_specs=out_specs, out_shape=out_shape,
        compiler_params=pltpu.CompilerParams(dimension_semantics=("arbitrary",), vmem_limit_bytes=VMEM_LIMIT),
    )(*args)


def _scores_stage(q_ref, k_ref, s_ref, tmax_ref, tile, keep_fn):
    for hh in range(HEADS_PER_STEP):
        lanes = slice(hh * LANES, (hh + 1) * LANES)
        q = q_ref[:, lanes].astype(BF16)
        tmax = [None, None]
        for ch in range(KEY_CHUNKS):
            row0 = pl.multiple_of((tile * KEY_CHUNKS + ch) * KEY_CHUNK, KEY_CHUNK)
            halves = _lo_hi(k_ref[pl.ds(row0, KEY_CHUNK), lanes])
            for c in range(2):
                s = _nt(halves[c], q)
                keep = keep_fn(2 * hh + c, ch)
                if keep is not None:
                    s = jnp.where(keep, s, NEG_INF)
                s_ref[2 * hh + c, ch * KEY_CHUNK:(ch + 1) * KEY_CHUNK] = s
                top = jnp.max(s, axis=0, keepdims=True)
                tmax[c] = top if tmax[c] is None else jnp.maximum(tmax[c], top)
        for c in range(2):
            tmax_ref[2 * hh + c] = tmax[c]


def _softmax_stage(s_ref, tmax_ref, m_ref, alpha_ref, p_ref):
    for mi in range(2 * HEADS_PER_STEP):
        m_old = m_ref[mi]
        m_new = jnp.maximum(m_old, tmax_ref[mi])
        alpha_ref[mi] = jnp.exp2(m_old - m_new)
        m_ref[mi] = m_new
        p_ref[mi] = jnp.exp2(s_ref[mi] - m_new).astype(BF16)


def _values_stage(vt_of, tile, alpha_ref, p_ref, acc_ref):
    for mi in range(2 * HEADS_PER_STEP):
        pv = None
        for ch in range(KEY_CHUNKS):
            term = _nn(vt_of(mi, tile * KEY_CHUNKS + ch), p_ref[mi, ch * KEY_CHUNK:(ch + 1) * KEY_CHUNK])
            pv = term if pv is None else pv + term
        acc_ref[mi] = alpha_ref[mi] * acc_ref[mi] + pv


def _flash_pipeline(i, q_ref, k_ref, vt_of, s_ref, tmax_ref, m_ref, alpha_ref, p_ref, acc_ref, diag_keep, past_keep):
    m_ref[...] = jnp.full(m_ref.shape, M_INIT, F32)
    alpha_ref[...] = jnp.ones(alpha_ref.shape, F32)
    p_ref[...] = jnp.zeros(p_ref.shape, BF16)
    acc_ref[...] = jnp.zeros(acc_ref.shape, F32)

    _scores_stage(q_ref, k_ref, s_ref, tmax_ref, i, diag_keep)

    def tile_at(step):
        return jnp.where(step == 0, i, step - 1)

    def body(t, carry):
        _values_stage(vt_of, tile_at(jnp.maximum(t - 1, 0)), alpha_ref, p_ref, acc_ref)
        _softmax_stage(s_ref, tmax_ref, m_ref, alpha_ref, p_ref)
        _scores_stage(q_ref, k_ref, s_ref, tmax_ref, t, functools.partial(past_keep, t))
        return carry

    lax.fori_loop(0, i, body, 0)
    _values_stage(vt_of, tile_at(jnp.maximum(i - 1, 0)), alpha_ref, p_ref, acc_ref)
    _softmax_stage(s_ref, tmax_ref, m_ref, alpha_ref, p_ref)
    _values_stage(vt_of, tile_at(i), alpha_ref, p_ref, acc_ref)


def _chunk_iotas(ch):
    shape = (KEY_CHUNK, ATTN_BLOCK)
    return (lax.broadcasted_iota(jnp.int32, shape, 0) + ch * KEY_CHUNK, lax.broadcasted_iota(jnp.int32, shape, 1))


def _flash_scratch(acc_rows):
    maps = 2 * HEADS_PER_STEP
    blk = ATTN_BLOCK
    stat = pltpu.VMEM((maps, 1, blk), F32)
    return [pltpu.VMEM((maps, blk, blk), F32), stat, stat, stat, pltpu.VMEM((maps, blk, blk), BF16),
            pltpu.VMEM((maps, acc_rows, blk), F32)]


def _flash_specs(seq, nqb, vt_rows):
    width = LANES * HEADS_PER_STEP
    q_spec = pl.BlockSpec((ATTN_BLOCK, width), lambda b, h, i: (b * nqb + i, h))
    k_spec = pl.BlockSpec((seq, width), lambda b, h, i: (b, h))
    vt_spec = pl.BlockSpec((1, seq // KEY_CHUNK, vt_rows * HEADS_PER_STEP, KEY_CHUNK), lambda b, h, i: (b, 0, h, 0))
    return q_spec, k_spec, vt_spec


def _diff_attn_kernel(lq1_ref, lk1_ref, lq2_ref, lk2_ref, subg_ref, q_ref, k_ref, vt_ref, o_ref,
                      s_ref, tmax_ref, m_ref, alpha_ref, p_ref, acc_ref, *, lambda_init):
    i = pl.program_id(2)
    rows = LANES + BF16_ROWS

    def vt_of(mi, chunk):
        return vt_ref[0, chunk, (mi // 2) * rows:(mi // 2 + 1) * rows]

    def diag_keep(mi, ch):
        key, qry = _chunk_iotas(ch)
        return key <= qry

    _flash_pipeline(i, q_ref, k_ref, vt_of, s_ref, tmax_ref, m_ref, alpha_ref, p_ref, acc_ref, diag_keep,
                    lambda t, mi, ch: None)

    lam = (jnp.exp(jnp.sum(lq1_ref[...] * lk1_ref[...], axis=1, keepdims=True))
           - jnp.exp(jnp.sum(lq2_ref[...] * lk2_ref[...], axis=1, keepdims=True)) + lambda_init)
    for hh in range(HEADS_PER_STEP):
        a1, a2 = acc_ref[2 * hh], acc_ref[2 * hh + 1]
        o = a1[:LANES] / a1[LANES:LANES + 1] - lam * (a2[:LANES] / a2[LANES:LANES + 1])
        o = o * lax.rsqrt(jnp.mean(o * o, axis=0, keepdims=True) + NORM_EPS) * subg_ref[...]
        o_ref[:, hh * LANES:(hh + 1) * LANES] = (o * (1.0 - lambda_init)).T.astype(o_ref.dtype)


def _diff_attention(q, k, vt, lam_params, sub_g, *, batch, lambda_init):
    t = q.shape[0]
    seq = t // batch
    nqb = seq // ATTN_BLOCK
    rows = LANES + BF16_ROWS
    small = pl.BlockSpec((1, HEAD_DIM), lambda b, h, i: (0, 0))
    q_spec, k_spec, vt_spec = _flash_specs(seq, nqb, rows)
    return pl.pallas_call(
        functools.partial(_diff_attn_kernel, lambda_init=lambda_init),
        grid=(batch, PAIRS // HEADS_PER_STEP, nqb),
        in_specs=[small, small, small, small, pl.BlockSpec((LANES, 1), lambda b, h, i: (0, 0)),
                  q_spec, k_spec, vt_spec],
        out_specs=q_spec,
        out_shape=jax.ShapeDtypeStruct((t, D_MODEL), BF16),
        scratch_shapes=_flash_scratch(rows),
        compiler_params=pltpu.CompilerParams(dimension_semantics=("arbitrary",) * 3, vmem_limit_bytes=VMEM_LIMIT),
    )(*[p.reshape(1, HEAD_DIM) for p in lam_params], sub_g.reshape(LANES, 1), q, k, vt)


def _moba_select(gate, own):
    nb = gate.shape[0]
    blk_id = lax.broadcasted_iota(jnp.int32, gate.shape, 0)
    g = jnp.where(blk_id < own, gate, NEG_INF)
    sel = jnp.zeros(gate.shape, F32)
    for t in range(MOBA_TOPK):
        best = jnp.max(g, axis=0, keepdims=True)
        first = jnp.min(jnp.where(g == best, blk_id, nb), axis=0, keepdims=True)
        hit = blk_id == first
        sel = jnp.maximum(sel, jnp.where(hit, jnp.where(t < own, 1.0, 0.0), 0.0))
        g = jnp.where(hit, NEG_INF, g)
    return sel


def _moba_attn_kernel(q_ref, k_ref, vt_ref, km_ref, o_ref,
                      sel_ref, s_ref, tmax_ref, m_ref, alpha_ref, p_ref, acc_ref):
    i = pl.program_id(2)
    half = HEAD_DIM + BF16_ROWS
    own = i * KEY_CHUNKS + lax.broadcasted_iota(jnp.int32, (1, ATTN_BLOCK), 1) // MOBA_BLOCK
    for hh in range(HEADS_PER_STEP):
        lanes = slice(hh * LANES, (hh + 1) * LANES)
        qf = q_ref[:, lanes]
        km = _lo_hi(km_ref[0, :, lanes])
        for c in range(2):
            sel_ref[2 * hh + c] = _moba_select(_nt(km[c], qf, lax.Precision.HIGHEST), own)

    def vt_of(mi, chunk):
        return vt_ref[0, chunk, mi * half:(mi + 1) * half]

    def past_keep(t, mi, ch):
        return sel_ref[mi, pl.ds(t * KEY_CHUNKS + ch, 1), :] > 0.0

    def diag_keep(mi, ch):
        key, qry = _chunk_iotas(ch)
        same_block = qry < (ch + 1) * MOBA_BLOCK
        keep = (key <= qry) & same_block
        if ch + 1 < KEY_CHUNKS:
            keep = keep | (jnp.logical_not(same_block) & past_keep(i, mi, ch))
        return keep

    _flash_pipeline(i, q_ref, k_ref, vt_of, s_ref, tmax_ref, m_ref, alpha_ref, p_ref, acc_ref, diag_keep, past_keep)

    for hh in range(HEADS_PER_STEP):
        a1, a2 = acc_ref[2 * hh], acc_ref[2 * hh + 1]
        o = jnp.concatenate([a1[:HEAD_DIM] / a1[HEAD_DIM:HEAD_DIM + 1], a2[:HEAD_DIM] / a2[HEAD_DIM:HEAD_DIM + 1]],
                            axis=0)
        o_ref[:, hh * LANES:(hh + 1) * LANES] = o.T.astype(o_ref.dtype)


def _moba_attention(q, k, vt, kmean, *, batch):
    t = q.shape[0]
    seq = t // batch
    nqb = seq // ATTN_BLOCK
    nb = seq // MOBA_BLOCK
    half = HEAD_DIM + BF16_ROWS
    q_spec, k_spec, vt_spec = _flash_specs(seq, nqb, 2 * half)
    return pl.pallas_call(
        _moba_attn_kernel,
        grid=(batch, PAIRS // HEADS_PER_STEP, nqb),
        in_specs=[q_spec, k_spec, vt_spec,
                  pl.BlockSpec((1, nb, LANES * HEADS_PER_STEP), lambda b, h, i: (b, 0, h))],
        out_specs=q_spec,
        out_shape=jax.ShapeDtypeStruct((t, D_MODEL), BF16),
        scratch_shapes=[pltpu.VMEM((2 * HEADS_PER_STEP, nb, ATTN_BLOCK), F32)] + _flash_scratch(half),
        compiler_params=pltpu.CompilerParams(dimension_semantics=("arbitrary",) * 3, vmem_limit_bytes=VMEM_LIMIT),
    )(q, k, vt, kmean)


def _swa_attn_kernel(sink_ref, q_ref, k_ref, vt_ref, o_ref):
    h = pl.program_id(1)
    i = pl.program_id(2)
    qb = SWA_Q_BLOCK
    w = SWA_WINDOW
    q = q_ref[...]
    q0 = pl.multiple_of(i * qb, qb)
    prev0 = pl.multiple_of(jnp.maximum(q0 - w, 0), w)
    k_own = _lo_hi(k_ref[pl.ds(q0, qb), :])
    k_prev = _lo_hi(k_ref[pl.ds(prev0, w), :])

    key = lax.broadcasted_iota(jnp.int32, (qb, qb), 0)
    qry = lax.broadcasted_iota(jnp.int32, (qb, qb), 1)
    own_keep = (key <= qry) & (qry - key < w)
    pkey = lax.broadcasted_iota(jnp.int32, (w, qb), 0)
    pqry = lax.broadcasted_iota(jnp.int32, (w, qb), 1)
    prev_keep = pkey > pqry + jnp.where(i > 0, 0, w)

    rows = (h // (PAIRS // SWA_KV_HEADS)) * HEAD_DIM
    vt_prev = vt_ref[0, jnp.maximum(2 * i - 1, 0), pl.ds(pl.multiple_of(rows, HEAD_DIM), HEAD_DIM), :]
    vt_own = jnp.concatenate(
        [vt_ref[0, 2 * i + c, pl.ds(pl.multiple_of(rows, HEAD_DIM), HEAD_DIM), :] for c in range(qb // w)], axis=1)

    outs = []
    for half in range(2):
        sink = sink_ref[2 * h + half]
        s_own = jnp.where(own_keep, _nt(k_own[half], q), NEG_INF)
        s_prev = jnp.where(prev_keep, _nt(k_prev[half], q), NEG_INF)
        m = jnp.maximum(jnp.maximum(jnp.max(s_own, axis=0, keepdims=True), jnp.max(s_prev, axis=0, keepdims=True)),
                        sink)
        e_own = jnp.exp(s_own - m)
        e_prev = jnp.exp(s_prev - m)
        denom = jnp.sum(e_own, axis=0, keepdims=True) + jnp.sum(e_prev, axis=0, keepdims=True) + jnp.exp(sink - m)
        inv = 1.0 / denom
        acc = _nn(vt_own, (e_own * inv).astype(BF16)) + _nn(vt_prev, (e_prev * inv).astype(BF16))
        outs.append(acc)
    o_ref[...] = jnp.concatenate(outs, axis=0).T.astype(o_ref.dtype)


def _swa_attention(q, kdup, vt, sinks, *, batch):
    t = q.shape[0]
    seq = t // batch
    qb = SWA_Q_BLOCK
    nqb = seq // qb
    per_kv = PAIRS // SWA_KV_HEADS
    return pl.pallas_call(
        _swa_attn_kernel,
        grid=(batch, PAIRS, nqb),
        in_specs=[pl.BlockSpec(memory_space=pltpu.SMEM),
                  pl.BlockSpec((qb, LANES), lambda b, h, i: (b * nqb + i, h)),
                  pl.BlockSpec((seq, LANES), lambda b, h, i: (b, h // per_kv)),
                  pl.BlockSpec((1, seq // SWA_WINDOW, LANES, SWA_WINDOW), lambda b, h, i: (b, 0, 0, 0))],
        out_specs=pl.BlockSpec((qb, LANES), lambda b, h, i: (b * nqb + i, h)),
        out_shape=jax.ShapeDtypeStruct((t, D_MODEL), BF16),
        compiler_params=pltpu.CompilerParams(dimension_semantics=("arbitrary",) * 3, vmem_limit_bytes=VMEM_LIMIT),
    )(sinks, q, kdup, vt)


def _post_kernel(*refs, final):
    h_ref, o_ref, wout_ref, g_ref, wup_ref, wdn_ref = refs[:6]
    fg_ref = refs[6] if final else None
    out_ref = refs[-1]
    h1 = h_ref[...] + _nn(o_ref[...], wout_ref[...])
    xn = _rms(h1, g_ref[...]).astype(BF16)
    acc = h1
    for c in range(D_FF // FF_CHUNK):
        lo = c * FF_CHUNK
        a = jnp.maximum(_nn(xn, wup_ref[:, lo:lo + FF_CHUNK]), 0.0)
        acc = acc + _nn((a * a).astype(BF16), wdn_ref[lo:lo + FF_CHUNK, :])
    if final:
        acc = _rms(acc, fg_ref[...])
    out_ref[...] = acc


def _post(h, o, w_out, g, w_up, w_down, final_g=None):
    t = h.shape[0]
    rows = PROJ_ROWS
    const = lambda i: (0, 0)
    row = lambda i: (i, 0)
    in_specs = [pl.BlockSpec((rows, D_MODEL), row), pl.BlockSpec((rows, D_MODEL), row),
                pl.BlockSpec(w_out.shape, const), pl.BlockSpec((1, D_MODEL), const),
                pl.BlockSpec(w_up.shape, const), pl.BlockSpec(w_down.shape, const)]
    args = [h, o, w_out, g.reshape(1, D_MODEL), w_up, w_down]
    if final_g is not None:
        in_specs.append(pl.BlockSpec((1, D_MODEL), const))
        args.append(final_g.reshape(1, D_MODEL))
    return pl.pallas_call(
        functools.partial(_post_kernel, final=final_g is not None),
        grid=(t // rows,),
        in_specs=in_specs,
        out_specs=pl.BlockSpec((rows, D_MODEL), row),
        out_shape=jax.ShapeDtypeStruct((t, D_MODEL), F32),
        compiler_params=pltpu.CompilerParams(dimension_semantics=("arbitrary",), vmem_limit_bytes=VMEM_LIMIT),
    )(*args)


def kernel(x, positions, attn_norm_g, mlp_norm_g, diff_w_in, diff_w_out, diff_lam_q1, diff_lam_k1, diff_lam_q2,
           diff_lam_k2, diff_subln_g, moba_w_in, moba_w_out, swa_w_in, swa_b_in, swa_sinks, swa_w_out, mlp_w_up,
           mlp_w_down, final_norm_g):
    batch, seq, _ = x.shape
    depth = attn_norm_g.shape[0]
    tables = _rope_tables(positions)
    h = x.reshape(batch * seq, D_MODEL)
    qk = 2 * D_MODEL
    log2_scale = Q_SCALE * LOG2_E

    for i in range(depth):
        mixer, slot = i % N_MIXERS, i // N_MIXERS
        if mixer == 0:
            w_in = diff_w_in[slot]
            q, k, vt = _project(h, attn_norm_g[i], w_in[:, :qk].astype(BF16), w_in[:, qk:].T.astype(BF16), tables,
                                batch=batch, nq=D_MODEL, kv_block=KEY_CHUNK, q_scale=log2_scale, v_group=LANES)
            o = _diff_attention(q, k, vt,
                                (diff_lam_q1[slot], diff_lam_k1[slot], diff_lam_q2[slot], diff_lam_k2[slot]),
                                diff_subln_g[slot], batch=batch, lambda_init=0.8 - 0.6 * math.exp(-0.3 * i))
            w_out = diff_w_out[slot]
        elif mixer == 1:
            w_in = moba_w_in[slot]
            q, k, vt, kmean = _project(h, attn_norm_g[i], w_in[:, :qk].astype(BF16), w_in[:, qk:].T.astype(BF16),
                                       tables, batch=batch, nq=D_MODEL, kv_block=KEY_CHUNK, q_scale=log2_scale,
                                       v_group=HEAD_DIM, q_dtype=F32, with_kmean=True)
            o = _moba_attention(q, k, vt, kmean.reshape(batch, seq // MOBA_BLOCK, D_MODEL), batch=batch)
            w_out = moba_w_out[slot]
        else:
            w_in, b_in = swa_w_in[slot], swa_b_in[slot]
            dup = lambda a: jnp.concatenate(
                [a[..., D_MODEL + kv * HEAD_DIM:D_MODEL + (kv + 1) * HEAD_DIM]
                 for kv in range(SWA_KV_HEADS) for _ in range(2)], axis=-1)
            v0 = D_MODEL + SWA_KV_HEADS * HEAD_DIM
            wqk = jnp.concatenate([w_in[:, :D_MODEL], dup(w_in)], axis=1).astype(BF16)
            bqk = jnp.concatenate([b_in[:D_MODEL], dup(b_in)])
            q, k, vt = _project(h, attn_norm_g[i], wqk, w_in[:, v0:].T.astype(BF16), tables, batch=batch,
                                nq=D_MODEL, kv_block=SWA_WINDOW, q_scale=Q_SCALE, bias=(bqk, b_in[v0:]))
            o = _swa_attention(q, k, vt, swa_sinks[slot], batch=batch)
            w_out = swa_w_out[slot]
        h = _post(h, o, w_out.astype(BF16), mlp_norm_g[i], mlp_w_up[i].astype(BF16), mlp_w_down[i].astype(BF16),
                  final_norm_g if i == depth - 1 else None)
    return h.reshape(batch, seq, D_MODEL)
```

```python
import functools
import math

import jax
import jax.numpy as jnp
from jax import lax
from jax.experimental import pallas as pl
from jax.experimental.pallas import tpu as pltpu

F32 = jnp.float32
BF16 = jnp.bfloat16

D_MODEL = 1024
HEAD_DIM = 64
ROT_DIM = HEAD_DIM // 4
ROT_HALF = ROT_DIM // 2
ROPE_THETA = 500000.0
NORM_EPS = 1e-6
D_FF = 4 * D_MODEL
Q_SCALE = HEAD_DIM ** -0.5
LOG2_E = math.log2(math.e)
N_MIXERS = 3

LANES = 128
BF16_ROWS = 16
MXU_COLS = 256
PAIRS = D_MODEL // LANES
MOBA_BLOCK = 256
MOBA_TOPK = 3
SWA_WINDOW = 128
SWA_KV_HEADS = 2

ATTN_BLOCK = 512
KEY_CHUNK = 256
KEY_CHUNKS = ATTN_BLOCK // KEY_CHUNK
HEADS_PER_STEP = 2
SWA_Q_BLOCK = 256
PROJ_ROWS = 512
FF_CHUNK = 1024
VMEM_LIMIT = 56 * 1024 * 1024

NEG_INF = float("-inf")
M_INIT = -1e30


def _nt(a, b, precision=None):
    return lax.dot_general(a, b, (((1,), (1,)), ((), ())), preferred_element_type=F32, precision=precision)


def _nn(a, b):
    return jnp.dot(a, b, preferred_element_type=F32)


def _rms(x, g):
    return x * lax.rsqrt(jnp.mean(x * x, axis=-1, keepdims=True) + NORM_EPS) * g


def _lo_hi(x):
    lane = lax.broadcasted_iota(jnp.int32, (1, LANES), 1)
    zero = jnp.zeros_like(x)
    return jnp.where(lane < HEAD_DIM, x, zero), jnp.where(lane < HEAD_DIM, zero, x)


def _rope_table_kernel(pos_ref, inv_ref, c_ref, sa_ref, sb_ref):
    ang = pos_ref[...].astype(F32) * inv_ref[...]
    lane = lax.broadcasted_iota(jnp.int32, ang.shape, 1) % HEAD_DIM
    cos = jnp.cos(ang)
    sin = jnp.sin(ang)
    c_ref[...] = jnp.where(lane < ROT_DIM, cos, 1.0)
    sa_ref[...] = jnp.where(lane < ROT_HALF, -sin, 0.0)
    sb_ref[...] = jnp.where((lane >= ROT_HALF) & (lane < ROT_DIM), sin, 0.0)


def _rope_tables(positions):
    t = positions.size
    rows = 1024
    inv = ROPE_THETA ** (-jnp.arange(0, ROT_DIM, 2, dtype=F32) / ROT_DIM)
    lane = jnp.arange(LANES) % HEAD_DIM
    inv_lane = jnp.where(lane < ROT_DIM, inv[lane % ROT_HALF], 0.0).reshape(1, LANES)
    tab = jax.ShapeDtypeStruct((t, LANES), F32)
    spec = pl.BlockSpec((rows, LANES), lambda i: (i, 0))
    return pl.pallas_call(
        _rope_table_kernel,
        grid=(t // rows,),
        in_specs=[pl.BlockSpec((rows, 1), lambda i: (i, 0)), pl.BlockSpec((1, LANES), lambda i: (0, 0))],
        out_specs=[spec, spec, spec],
        out_shape=[tab, tab, tab],
    )(positions.reshape(t, 1), inv_lane)


def _proj_kernel(*refs, nq, nk, kv_block, q_scale, v_group, has_bias, with_kmean):
    it = iter(refs)
    h_ref, g_ref, wqk_ref, wvt_ref = next(it), next(it), next(it), next(it)
    bqk_ref = next(it) if has_bias else None
    bv_ref = next(it) if has_bias else None
    c_ref, sa_ref, sb_ref = next(it), next(it), next(it)
    q_ref, k_ref, vt_ref = next(it), next(it), next(it)
    km_ref = next(it) if with_kmean else None

    rows = h_ref.shape[0]
    xn = _rms(h_ref[...], g_ref[...]).astype(BF16)
    cos, sin_a, sin_b = c_ref[...], sa_ref[...], sb_ref[...]

    for c in range((nq + nk) // LANES):
        col = c * LANES
        if c % 2 == 0:
            wide = _nn(xn, wqk_ref[:, col:col + MXU_COLS])
            if has_bias:
                wide = wide + bqk_ref[:, col:col + MXU_COLS]
        y = wide[:, (c % 2) * LANES:(c % 2 + 1) * LANES]
        y = y * cos + pltpu.roll(y, LANES - ROT_HALF, 1) * sin_a + pltpu.roll(y, ROT_HALF, 1) * sin_b
        if col < nq:
            q_ref[:, col:col + LANES] = (y * q_scale).astype(q_ref.dtype)
        else:
            kc = col - nq
            k_ref[:, kc:kc + LANES] = y.astype(k_ref.dtype)
            if with_kmean:
                for r in range(rows // MOBA_BLOCK):
                    blk = y[r * MOBA_BLOCK:(r + 1) * MOBA_BLOCK]
                    km_ref[0, r:r + 1, kc:kc + LANES] = jnp.sum(blk, axis=0, keepdims=True) * (1.0 / MOBA_BLOCK)

    vt = _nt(wvt_ref[...], xn)
    if has_bias:
        vt = vt + bv_ref[...]
    nv = vt.shape[0]
    for r in range(rows // kv_block):
        blk = vt[:, r * kv_block:(r + 1) * kv_block].astype(vt_ref.dtype)
        if v_group is None:
            vt_ref[0, r] = blk
        else:
            stride = v_group + BF16_ROWS
            for g in range(nv // v_group):
                vt_ref[0, r, g * stride:g * stride + v_group] = blk[g * v_group:(g + 1) * v_group]
                vt_ref[0, r, g * stride + v_group:(g + 1) * stride] = jnp.ones((BF16_ROWS, kv_block), vt_ref.dtype)


def _project(h, g, wqk, wvt, tables, *, batch, nq, kv_block, q_scale, v_group=None, q_dtype=BF16, bias=None,
             with_kmean=False):
    t = h.shape[0]
    rows = PROJ_ROWS
    nk = wqk.shape[1] - nq
    nv = wvt.shape[0]
    nv_out = nv if v_group is None else nv // v_group * (v_group + BF16_ROWS)
    seq = t // batch
    tiles_per_seq = seq // rows
    chunks = rows // kv_block

    const = lambda i: (0, 0)
    row = lambda i: (i, 0)
    in_specs = [pl.BlockSpec((rows, D_MODEL), row), pl.BlockSpec((1, D_MODEL), const),
                pl.BlockSpec(wqk.shape, const), pl.BlockSpec(wvt.shape, const)]
    args = [h, g.reshape(1, D_MODEL), wqk, wvt]
    if bias is not None:
        in_specs += [pl.BlockSpec((1, nq + nk), const), pl.BlockSpec((nv, 1), const)]
        args += [bias[0].reshape(1, nq + nk), bias[1].reshape(nv, 1)]
    in_specs += [pl.BlockSpec((rows, LANES), row)] * 3
    args += list(tables)

    out_shape = [jax.ShapeDtypeStruct((t, nq), q_dtype), jax.ShapeDtypeStruct((t, nk), BF16),
                 jax.ShapeDtypeStruct((batch, seq // kv_block, nv_out, kv_block), BF16)]
    out_specs = [pl.BlockSpec((rows, nq), row), pl.BlockSpec((rows, nk), row),
                 pl.BlockSpec((1, chunks, nv_out, kv_block),
                              lambda i: (i // tiles_per_seq, i % tiles_per_seq, 0, 0))]
    if with_kmean:
        out_shape.append(jax.ShapeDtypeStruct((t // rows, rows // MOBA_BLOCK, nk), F32))
        out_specs.append(pl.BlockSpec((1, rows // MOBA_BLOCK, nk), lambda i: (i, 0, 0)))

    return pl.pallas_call(
        functools.partial(_proj_kernel, nq=nq, nk=nk, kv_block=kv_block, q_scale=q_scale, v_group=v_group,
                          has_bias=bias is not None, with_kmean=with_kmean),
        grid=(t // rows,),
        in_specs=in_specs, out_specs=out_specs, out_shape=out_shape,
        compiler_params=pltpu.CompilerParams(dimension_semantics=("arbitrary",), vmem_limit_bytes=VMEM_LIMIT),
    )(*args)


def _scores_stage(q_ref, k_ref, s_ref, tmax_ref, tile, keep_fn):
    for hh in range(HEADS_PER_STEP):
        lanes = slice(hh * LANES, (hh + 1) * LANES)
        q = q_ref[:, lanes].astype(BF16)
        tmax = [None, None]
        for ch in range(KEY_CHUNKS):
            row0 = pl.multiple_of((tile * KEY_CHUNKS + ch) * KEY_CHUNK, KEY_CHUNK)
            halves = _lo_hi(k_ref[pl.ds(row0, KEY_CHUNK), lanes])
            for c in range(2):
                s = _nt(halves[c], q)
                keep = keep_fn(2 * hh + c, ch)
                if keep is not None:
                    s = jnp.where(keep, s, NEG_INF)
                s_ref[2 * hh + c, ch * KEY_CHUNK:(ch + 1) * KEY_CHUNK] = s
                top = jnp.max(s, axis=0, keepdims=True)
                tmax[c] = top if tmax[c] is None else jnp.maximum(tmax[c], top)
        for c in range(2):
            tmax_ref[2 * hh + c] = tmax[c]


def _softmax_stage(s_ref, tmax_ref, m_ref, alpha_ref, p_ref):
    for mi in range(2 * HEADS_PER_STEP):
        m_old = m_ref[mi]
        m_new = jnp.maximum(m_old, tmax_ref[mi])
        alpha_ref[mi] = jnp.exp2(m_old - m_new)
        m_ref[mi] = m_new
        p_ref[mi] = jnp.exp2(s_ref[mi] - m_new).astype(BF16)


def _values_stage(vt_of, tile, alpha_ref, p_ref, acc_ref):
    for mi in range(2 * HEADS_PER_STEP):
        pv = None
        for ch in range(KEY_CHUNKS):
            term = _nn(vt_of(mi, tile * KEY_CHUNKS + ch), p_ref[mi, ch * KEY_CHUNK:(ch + 1) * KEY_CHUNK])
            pv = term if pv is None else pv + term
        acc_ref[mi] = alpha_ref[mi] * acc_ref[mi] + pv


def _flash_pipeline(i, q_ref, k_ref, vt_of, s_ref, tmax_ref, m_ref, alpha_ref, p_ref, acc_ref, diag_keep, past_keep):
    m_ref[...] = jnp.full(m_ref.shape, M_INIT, F32)
    alpha_ref[...] = jnp.ones(alpha_ref.shape, F32)
    p_ref[...] = jnp.zeros(p_ref.shape, BF16)
    acc_ref[...] = jnp.zeros(acc_ref.shape, F32)

    _scores_stage(q_ref, k_ref, s_ref, tmax_ref, i, diag_keep)

    def tile_at(step):
        return jnp.where(step == 0, i, step - 1)

    def body(t, carry):
        _values_stage(vt_of, tile_at(jnp.maximum(t - 1, 0)), alpha_ref, p_ref, acc_ref)
        _softmax_stage(s_ref, tmax_ref, m_ref, alpha_ref, p_ref)
        _scores_stage(q_ref, k_ref, s_ref, tmax_ref, t, functools.partial(past_keep, t))
        return carry

    lax.fori_loop(0, i, body, 0)
    _values_stage(vt_of, tile_at(jnp.maximum(i - 1, 0)), alpha_ref, p_ref, acc_ref)
    _softmax_stage(s_ref, tmax_ref, m_ref, alpha_ref, p_ref)
    _values_stage(vt_of, tile_at(i), alpha_ref, p_ref, acc_ref)


def _chunk_iotas(ch):
    shape = (KEY_CHUNK, ATTN_BLOCK)
    return (lax.broadcasted_iota(jnp.int32, shape, 0) + ch * KEY_CHUNK, lax.broadcasted_iota(jnp.int32, shape, 1))


def _flash_scratch(acc_rows):
    maps = 2 * HEADS_PER_STEP
    blk = ATTN_BLOCK
    stat = pltpu.VMEM((maps, 1, blk), F32)
    return [pltpu.VMEM((maps, blk, blk), F32), stat, stat, stat, pltpu.VMEM((maps, blk, blk), BF16),
            pltpu.VMEM((maps, acc_rows, blk), F32)]


def _flash_specs(seq, nqb, vt_rows):
    width = LANES * HEADS_PER_STEP
    q_spec = pl.BlockSpec((ATTN_BLOCK, width), lambda b, h, i: (b * nqb + i, h))
    k_spec = pl.BlockSpec((seq, width), lambda b, h, i: (b, h))
    vt_spec = pl.BlockSpec((1, seq // KEY_CHUNK, vt_rows * HEADS_PER_STEP, KEY_CHUNK), lambda b, h, i: (b, 0, h, 0))
    return q_spec, k_spec, vt_spec


def _diff_attn_kernel(lq1_ref, lk1_ref, lq2_ref, lk2_ref, subg_ref, q_ref, k_ref, vt_ref, o_ref,
                      s_ref, tmax_ref, m_ref, alpha_ref, p_ref, acc_ref, *, lambda_init):
    i = pl.program_id(2)
    rows = LANES + BF16_ROWS

    def vt_of(mi, chunk):
        return vt_ref[0, chunk, (mi // 2) * rows:(mi // 2 + 1) * rows]

    def diag_keep(mi, ch):
        key, qry = _chunk_iotas(ch)
        return key <= qry

    _flash_pipeline(i, q_ref, k_ref, vt_of, s_ref, tmax_ref, m_ref, alpha_ref, p_ref, acc_ref, diag_keep,
                    lambda t, mi, ch: None)

    lam = (jnp.exp(jnp.sum(lq1_ref[...] * lk1_ref[...], axis=1, keepdims=True))
           - jnp.exp(jnp.sum(lq2_ref[...] * lk2_ref[...], axis=1, keepdims=True)) + lambda_init)
    for hh in range(HEADS_PER_STEP):
        a1, a2 = acc_ref[2 * hh], acc_ref[2 * hh + 1]
        o = a1[:LANES] / a1[LANES:LANES + 1] - lam * (a2[:LANES] / a2[LANES:LANES + 1])
        o = o * lax.rsqrt(jnp.mean(o * o, axis=0, keepdims=True) + NORM_EPS) * subg_ref[...]
        o_ref[:, hh * LANES:(hh + 1) * LANES] = (o * (1.0 - lambda_init)).T.astype(o_ref.dtype)


def _diff_attention(q, k, vt, lam_params, sub_g, *, batch, lambda_init):
    t = q.shape[0]
    seq = t // batch
    nqb = seq // ATTN_BLOCK
    rows = LANES + BF16_ROWS
    small = pl.BlockSpec((1, HEAD_DIM), lambda b, h, i: (0, 0))
    q_spec, k_spec, vt_spec = _flash_specs(seq, nqb, rows)
    return pl.pallas_call(
        functools.partial(_diff_attn_kernel, lambda_init=lambda_init),
        grid=(batch, PAIRS // HEADS_PER_STEP, nqb),
        in_specs=[small, small, small, small, pl.BlockSpec((LANES, 1), lambda b, h, i: (0, 0)),
                  q_spec, k_spec, vt_spec],
        out_specs=q_spec,
        out_shape=jax.ShapeDtypeStruct((t, D_MODEL), BF16),
        scratch_shapes=_flash_scratch(rows),
        compiler_params=pltpu.CompilerParams(dimension_semantics=("arbitrary",) * 3, vmem_limit_bytes=VMEM_LIMIT),
    )(*[p.reshape(1, HEAD_DIM) for p in lam_params], sub_g.reshape(LANES, 1), q, k, vt)


def _moba_select(gate, own):
    nb = gate.shape[0]
    blk_id = lax.broadcasted_iota(jnp.int32, gate.shape, 0)
    g = jnp.where(blk_id < own, gate, NEG_INF)
    sel = jnp.zeros(gate.shape, F32)
    for t in range(MOBA_TOPK):
        best = jnp.max(g, axis=0, keepdims=True)
        first = jnp.min(jnp.where(g == best, blk_id, nb), axis=0, keepdims=True)
        hit = blk_id == first
        sel = jnp.maximum(sel, jnp.where(hit, jnp.where(t < own, 1.0, 0.0), 0.0))
        g = jnp.where(hit, NEG_INF, g)
    return sel


def _moba_attn_kernel(q_ref, k_ref, vt_ref, km_ref, o_ref,
                      sel_ref, s_ref, tmax_ref, m_ref, alpha_ref, p_ref, acc_ref):
    i = pl.program_id(2)
    half = HEAD_DIM + BF16_ROWS
    own = i * KEY_CHUNKS + lax.broadcasted_iota(jnp.int32, (1, ATTN_BLOCK), 1) // MOBA_BLOCK
    for hh in range(HEADS_PER_STEP):
        lanes = slice(hh * LANES, (hh + 1) * LANES)
        qf = q_ref[:, lanes]
        km = _lo_hi(km_ref[0, :, lanes])
        for c in range(2):
            sel_ref[2 * hh + c] = _moba_select(_nt(km[c], qf, lax.Precision.HIGHEST), own)

    def vt_of(mi, chunk):
        return vt_ref[0, chunk, mi * half:(mi + 1) * half]

    def past_keep(t, mi, ch):
        return sel_ref[mi, pl.ds(t * KEY_CHUNKS + ch, 1), :] > 0.0

    def diag_keep(mi, ch):
        key, qry = _chunk_iotas(ch)
        same_block = qry < (ch + 1) * MOBA_BLOCK
        keep = (key <= qry) & same_block
        if ch + 1 < KEY_CHUNKS:
            keep = keep | (jnp.logical_not(same_block) & past_keep(i, mi, ch))
        return keep

    _flash_pipeline(i, q_ref, k_ref, vt_of, s_ref, tmax_ref, m_ref, alpha_ref, p_ref, acc_ref, diag_keep, past_keep)

    for hh in range(HEADS_PER_STEP):
        a1, a2 = acc_ref[2 * hh], acc_ref[2 * hh + 1]
        o = jnp.concatenate([a1[:HEAD_DIM] / a1[HEAD_DIM:HEAD_DIM + 1], a2[:HEAD_DIM] / a2[HEAD_DIM:HEAD_DIM + 1]],
                            axis=0)
        o_ref[:, hh * LANES:(hh + 1) * LANES] = o.T.astype(o_ref.dtype)


def _moba_attention(q, k, vt, kmean, *, batch):
    t = q.shape[0]
    seq = t // batch
    nqb = seq // ATTN_BLOCK
    nb = seq // MOBA_BLOCK
    half = HEAD_DIM + BF16_ROWS
    q_spec, k_spec, vt_spec = _flash_specs(seq, nqb, 2 * half)
    return pl.pallas_call(
        _moba_attn_kernel,
        grid=(batch, PAIRS // HEADS_PER_STEP, nqb),
        in_specs=[q_spec, k_spec, vt_spec,
                  pl.BlockSpec((1, nb, LANES * HEADS_PER_STEP), lambda b, h, i: (b, 0, h))],
        out_specs=q_spec,
        out_shape=jax.ShapeDtypeStruct((t, D_MODEL), BF16),
        scratch_shapes=[pltpu.VMEM((2 * HEADS_PER_STEP, nb, ATTN_BLOCK), F32)] + _flash_scratch(half),
        compiler_params=pltpu.CompilerParams(dimension_semantics=("arbitrary",) * 3, vmem_limit_bytes=VMEM_LIMIT),
    )(q, k, vt, kmean)


def _swa_attn_kernel(sink_ref, q_ref, k_ref, vt_ref, o_ref):
    h = pl.program_id(1)
    i = pl.program_id(2)
    qb = SWA_Q_BLOCK
    w = SWA_WINDOW
    q = q_ref[...]
    q0 = pl.multiple_of(i * qb, qb)
    prev0 = pl.multiple_of(jnp.maximum(q0 - w, 0), w)
    k_own = _lo_hi(k_ref[pl.ds(q0, qb), :])
    k_prev = _lo_hi(k_ref[pl.ds(prev0, w), :])

    key = lax.broadcasted_iota(jnp.int32, (qb, qb), 0)
    qry = lax.broadcasted_iota(jnp.int32, (qb, qb), 1)
    own_keep = (key <= qry) & (qry - key < w)
    pkey = lax.broadcasted_iota(jnp.int32, (w, qb), 0)
    pqry = lax.broadcasted_iota(jnp.int32, (w, qb), 1)
    prev_keep = pkey > pqry + jnp.where(i > 0, 0, w)

    rows = (h // (PAIRS // SWA_KV_HEADS)) * HEAD_DIM
    vt_prev = vt_ref[0, jnp.maximum(2 * i - 1, 0), pl.ds(pl.multiple_of(rows, HEAD_DIM), HEAD_DIM), :]
    vt_own = jnp.concatenate(
        [vt_ref[0, 2 * i + c, pl.ds(pl.multiple_of(rows, HEAD_DIM), HEAD_DIM), :] for c in range(qb // w)], axis=1)

    outs = []
    for half in range(2):
        sink = sink_ref[2 * h + half]
        s_own = jnp.where(own_keep, _nt(k_own[half], q), NEG_INF)
        s_prev = jnp.where(prev_keep, _nt(k_prev[half], q), NEG_INF)
        m = jnp.maximum(jnp.maximum(jnp.max(s_own, axis=0, keepdims=True), jnp.max(s_prev, axis=0, keepdims=True)),
                        sink)
        e_own = jnp.exp(s_own - m)
        e_prev = jnp.exp(s_prev - m)
        denom = jnp.sum(e_own, axis=0, keepdims=True) + jnp.sum(e_prev, axis=0, keepdims=True) + jnp.exp(sink - m)
        inv = 1.0 / denom
        acc = _nn(vt_own, (e_own * inv).astype(BF16)) + _nn(vt_prev, (e_prev * inv).astype(BF16))
        outs.append(acc)
    o_ref[...] = jnp.concatenate(outs, axis=0).T.astype(o_ref.dtype)


def _swa_attention(q, kdup, vt, sinks, *, batch):
    t = q.shape[0]
    seq = t // batch
    qb = SWA_Q_BLOCK
    nqb = seq // qb
    per_kv = PAIRS // SWA_KV_HEADS
    return pl.pallas_call(
        _swa_attn_kernel,
        grid=(batch, PAIRS, nqb),
        in_specs=[pl.BlockSpec(memory_space=pltpu.SMEM),
                  pl.BlockSpec((qb, LANES), lambda b, h, i: (b * nqb + i, h)),
                  pl.BlockSpec((seq, LANES), lambda b, h, i: (b, h // per_kv)),
                  pl.BlockSpec((1, seq // SWA_WINDOW, LANES, SWA_WINDOW), lambda b, h, i: (b, 0, 0, 0))],
        out_specs=pl.BlockSpec((qb, LANES), lambda b, h, i: (b * nqb + i, h)),
        out_shape=jax.ShapeDtypeStruct((t, D_MODEL), BF16),
        compiler_params=pltpu.CompilerParams(dimension_semantics=("arbitrary",) * 3, vmem_limit_bytes=VMEM_LIMIT),
    )(sinks, q, kdup, vt)


def _post_kernel(*refs, final):
    h_ref, o_ref, wout_ref, g_ref, wup_ref, wdn_ref = refs[:6]
    fg_ref = refs[6] if final else None
    out_ref = refs[-1]
    h1 = h_ref[...] + _nn(o_ref[...], wout_ref[...])
    xn = _rms(h1, g_ref[...]).astype(BF16)
    acc = h1
    for c in range(D_FF // FF_CHUNK):
        lo = c * FF_CHUNK
        a = jnp.maximum(_nn(xn, wup_ref[:, lo:lo + FF_CHUNK]), 0.0)
        acc = acc + _nn((a * a).astype(BF16), wdn_ref[lo:lo + FF_CHUNK, :])
    if final:
        acc = _rms(acc, fg_ref[...])
    out_ref[...] = acc


def _post(h, o, w_out, g, w_up, w_down, final_g=None):
    t = h.shape[0]
    rows = PROJ_ROWS
    const = lambda i: (0, 0)
    row = lambda i: (i, 0)
    in_specs = [pl.BlockSpec((rows, D_MODEL), row), pl.BlockSpec((rows, D_MODEL), row),
                pl.BlockSpec(w_out.shape, const), pl.BlockSpec((1, D_MODEL), const),
                pl.BlockSpec(w_up.shape, const), pl.BlockSpec(w_down.shape, const)]
    args = [h, o, w_out, g.reshape(1, D_MODEL), w_up, w_down]
    if final_g is not None:
        in_specs.append(pl.BlockSpec((1, D_MODEL), const))
        args.append(final_g.reshape(1, D_MODEL))
    return pl.pallas_call(
        functools.partial(_post_kernel, final=final_g is not None),
        grid=(t // rows,),
        in_specs=in_specs,
        out_specs=pl.BlockSpec((rows, D_MODEL), row),
        out_shape=jax.ShapeDtypeStruct((t, D_MODEL), F32),
        compiler_params=pltpu.CompilerParams(dimension_semantics=("arbitrary",), vmem_limit_bytes=VMEM_LIMIT),
    )(*args)


def kernel(x, positions, attn_norm_g, mlp_norm_g, diff_w_in, diff_w_out, diff_lam_q1, diff_lam_k1, diff_lam_q2,
           diff_lam_k2, diff_subln_g, moba_w_in, moba_w_out, swa_w_in, swa_b_in, swa_sinks, swa_w_out, mlp_w_up,
           mlp_w_down, final_norm_g):
    batch, seq, _ = x.shape
    depth = attn_norm_g.shape[0]
    tables = _rope_tables(positions)
    h = x.reshape(batch * seq, D_MODEL)
    qk = 2 * D_MODEL
    log2_scale = Q_SCALE * LOG2_E

    for i in range(depth):
        mixer, slot = i % N_MIXERS, i // N_MIXERS
        if mixer == 0:
            w_in = diff_w_in[slot]
            q, k, vt = _project(h, attn_norm_g[i], w_in[:, :qk].astype(BF16), w_in[:, qk:].T.astype(BF16), tables,
                                batch=batch, nq=D_MODEL, kv_block=KEY_CHUNK, q_scale=log2_scale, v_group=LANES)
            o = _diff_attention(q, k, vt,
                                (diff_lam_q1[slot], diff_lam_k1[slot], diff_lam_q2[slot], diff_lam_k2[slot]),
                                diff_subln_g[slot], batch=batch, lambda_init=0.8 - 0.6 * math.exp(-0.3 * i))
            w_out = diff_w_out[slot]
        elif mixer == 1:
            w_in = moba_w_in[slot]
            q, k, vt, kmean = _project(h, attn_norm_g[i], w_in[:, :qk].astype(BF16), w_in[:, qk:].T.astype(BF16),
                                       tables, batch=batch, nq=D_MODEL, kv_block=KEY_CHUNK, q_scale=log2_scale,
                                       v_group=HEAD_DIM, q_dtype=F32, with_kmean=True)
            o = _moba_attention(q, k, vt, kmean.reshape(batch, seq // MOBA_BLOCK, D_MODEL), batch=batch)
            w_out = moba_w_out[slot]
        else:
            w_in, b_in = swa_w_in[slot], swa_b_in[slot]
            dup = lambda a: jnp.concatenate(
                [a[..., D_MODEL + kv * HEAD_DIM:D_MODEL + (kv + 1) * HEAD_DIM]
                 for kv in range(SWA_KV_HEADS) for _ in range(2)], axis=-1)
            v0 = D_MODEL + SWA_KV_HEADS * HEAD_DIM
            wqk = jnp.concatenate([w_in[:, :D_MODEL], dup(w_in)], axis=1).astype(BF16)
            bqk = jnp.concatenate([b_in[:D_MODEL], dup(b_in)])
            q, k, vt = _project(h, attn_norm_g[i], wqk, w_in[:, v0:].T.astype(BF16), tables, batch=batch,
                                nq=D_MODEL, kv_block=SWA_WINDOW, q_scale=Q_SCALE, bias=(bqk, b_in[v0:]))
            o = _swa_attention(q, k, vt, swa_sinks[slot], batch=batch)
            w_out = swa_w_out[slot]
        h = _post(h, o, w_out.astype(BF16), mlp_norm_g[i], mlp_w_up[i].astype(BF16), mlp_w_down[i].astype(BF16),
                  final_norm_g if i == depth - 1 else None)
    return h.reshape(batch, seq, D_MODEL)
```

```python
import functools
import math

import jax
import jax.numpy as jnp
from jax import lax
from jax.experimental import pallas as pl
from jax.experimental.pallas import tpu as pltpu

F32 = jnp.float32
BF16 = jnp.bfloat16

D_MODEL = 1024
HEAD_DIM = 64
ROT_DIM = HEAD_DIM // 4
ROT_HALF = ROT_DIM // 2
ROPE_THETA = 500000.0
NORM_EPS = 1e-6
D_FF = 4 * D_MODEL
Q_SCALE = HEAD_DIM ** -0.5
LOG2_E = math.log2(math.e)
N_MIXERS = 3

LANES = 128
SUBLANES = 8
BF16_ROWS = 16
MXU_COLS = 256
PAIRS = D_MODEL // LANES
MOBA_BLOCK = 256
MOBA_TOPK = 3
SWA_WINDOW = 128
SWA_KV_HEADS = 2

ATTN_BLOCK = 512
KEY_CHUNK = 256
KEY_CHUNKS = ATTN_BLOCK // KEY_CHUNK
HEADS_PER_STEP = 2
SWA_Q_BLOCK = 1024
SWA_SUB_BLOCK = 256
PROJ_ROWS = 512
FF_CHUNK = 1024
VMEM_LIMIT = 56 * 1024 * 1024

NEG_INF = float("-inf")
M_INIT = -1e30


def _nt(a, b, precision=None):
    return lax.dot_general(a, b, (((1,), (1,)), ((), ())), preferred_element_type=F32, precision=precision)


def _nn(a, b):
    return jnp.dot(a, b, preferred_element_type=F32)


def _rms(x, g):
    return x * lax.rsqrt(jnp.mean(x * x, axis=-1, keepdims=True) + NORM_EPS) * g


def _lo_hi(x):
    lane = lax.broadcasted_iota(jnp.int32, (1, LANES), 1)
    zero = jnp.zeros_like(x)
    return jnp.where(lane < HEAD_DIM, x, zero), jnp.where(lane < HEAD_DIM, zero, x)


def _rope_table_kernel(pos_ref, inv_ref, c_ref, sa_ref, sb_ref):
    ang = pos_ref[...].astype(F32) * inv_ref[...]
    lane = lax.broadcasted_iota(jnp.int32, ang.shape, 1) % HEAD_DIM
    cos = jnp.cos(ang)
    sin = jnp.sin(ang)
    c_ref[...] = jnp.where(lane < ROT_DIM, cos, 1.0)
    sa_ref[...] = jnp.where(lane < ROT_HALF, -sin, 0.0)
    sb_ref[...] = jnp.where((lane >= ROT_HALF) & (lane < ROT_DIM), sin, 0.0)


def _rope_tables(positions):
    t = positions.size
    rows = 1024
    inv = ROPE_THETA ** (-jnp.arange(0, ROT_DIM, 2, dtype=F32) / ROT_DIM)
    lane = jnp.arange(LANES) % HEAD_DIM
    inv_lane = jnp.where(lane < ROT_DIM, inv[lane % ROT_HALF], 0.0).reshape(1, LANES)
    tab = jax.ShapeDtypeStruct((t, LANES), F32)
    spec = pl.BlockSpec((rows, LANES), lambda i: (i, 0))
    return pl.pallas_call(
        _rope_table_kernel,
        grid=(t // rows,),
        in_specs=[pl.BlockSpec((rows, 1), lambda i: (i, 0)), pl.BlockSpec((1, LANES), lambda i: (0, 0))],
        out_specs=[spec, spec, spec],
        out_shape=[tab, tab, tab],
    )(positions.reshape(t, 1), inv_lane)


def _proj_kernel(*refs, nq, nk, kv_block, q_scale, v_group, has_bias, with_kmean):
    it = iter(refs)
    h_ref, g_ref, wqk_ref, wvt_ref = next(it), next(it), next(it), next(it)
    bqk_ref = next(it) if has_bias else None
    bv_ref = next(it) if has_bias else None
    c_ref, sa_ref, sb_ref = next(it), next(it), next(it)
    q_ref, k_ref, vt_ref = next(it), next(it), next(it)
    km_ref = next(it) if with_kmean else None

    rows = h_ref.shape[0]
    xn = _rms(h_ref[...], g_ref[...]).astype(BF16)
    cos, sin_a, sin_b = c_ref[...], sa_ref[...], sb_ref[...]

    for c in range((nq + nk) // LANES):
        col = c * LANES
        if c % 2 == 0:
            wide = _nn(xn, wqk_ref[:, col:col + MXU_COLS])
            if has_bias:
                wide = wide + bqk_ref[:, col:col + MXU_COLS]
        y = wide[:, (c % 2) * LANES:(c % 2 + 1) * LANES]
        y = y * cos + pltpu.roll(y, LANES - ROT_HALF, 1) * sin_a + pltpu.roll(y, ROT_HALF, 1) * sin_b
        if col < nq:
            q_ref[:, col:col + LANES] = (y * q_scale).astype(q_ref.dtype)
        else:
            kc = col - nq
            k_ref[:, kc:kc + LANES] = y.astype(k_ref.dtype)
            if with_kmean:
                for r in range(rows // MOBA_BLOCK):
                    blk = y[r * MOBA_BLOCK:(r + 1) * MOBA_BLOCK]
                    km_ref[0, r:r + 1, kc:kc + LANES] = jnp.sum(blk, axis=0, keepdims=True) * (1.0 / MOBA_BLOCK)

    vt = _nt(wvt_ref[...], xn)
    if has_bias:
        vt = vt + bv_ref[...]
    nv = vt.shape[0]
    for r in range(rows // kv_block):
        blk = vt[:, r * kv_block:(r + 1) * kv_block].astype(vt_ref.dtype)
        if v_group is None:
            vt_ref[0, r] = blk
        else:
            stride = v_group + BF16_ROWS
            for g in range(nv // v_group):
                vt_ref[0, r, g * stride:g * stride + v_group] = blk[g * v_group:(g + 1) * v_group]
                vt_ref[0, r, g * stride + v_group:(g + 1) * stride] = jnp.ones((BF16_ROWS, kv_block), vt_ref.dtype)


def _project(h, g, wqk, wvt, tables, *, batch, nq, kv_block, q_scale, v_group=None, q_dtype=BF16, bias=None,
             with_kmean=False):
    t = h.shape[0]
    rows = PROJ_ROWS
    nk = wqk.shape[1] - nq
    nv = wvt.shape[0]
    nv_out = nv if v_group is None else nv // v_group * (v_group + BF16_ROWS)
    seq = t // batch
    tiles_per_seq = seq // rows
    chunks = rows // kv_block

    const = lambda i: (0, 0)
    row = lambda i: (i, 0)
    in_specs = [pl.BlockSpec((rows, D_MODEL), row), pl.BlockSpec((1, D_MODEL), const),
                pl.BlockSpec(wqk.shape, const), pl.BlockSpec(wvt.shape, const)]
    args = [h, g.reshape(1, D_MODEL), wqk, wvt]
    if bias is not None:
        in_specs += [pl.BlockSpec((1, nq + nk), const), pl.BlockSpec((nv, 1), const)]
        args += [bias[0].reshape(1, nq + nk), bias[1].reshape(nv, 1)]
    in_specs += [pl.BlockSpec((rows, LANES), row)] * 3
    args += list(tables)

    out_shape = [jax.ShapeDtypeStruct((t, nq), q_dtype), jax.ShapeDtypeStruct((t, nk), BF16),
                 jax.ShapeDtypeStruct((batch, seq // kv_block, nv_out, kv_block), BF16)]
    out_specs = [pl.BlockSpec((rows, nq), row), pl.BlockSpec((rows, nk), row),
                 pl.BlockSpec((1, chunks, nv_out, kv_block),
                              lambda i: (i // tiles_per_seq, i % tiles_per_seq, 0, 0))]
    if with_kmean:
        out_shape.append(jax.ShapeDtypeStruct((t // rows, rows // MOBA_BLOCK, nk), F32))
        out_specs.append(pl.BlockSpec((1, rows // MOBA_BLOCK, nk), lambda i: (i, 0, 0)))

    return pl.pallas_call(
        functools.partial(_proj_kernel, nq=nq, nk=nk, kv_block=kv_block, q_scale=q_scale, v_group=v_group,
                          has_bias=bias is not None, with_kmean=with_kmean),
        grid=(t // rows,),
        in_specs=in_specs, out_specs=out_specs, out_shape=out_shape,
        compiler_params=pltpu.CompilerParams(dimension_semantics=("arbitrary",), vmem_limit_bytes=VMEM_LIMIT),
    )(*args)


def _scores_stage(q_ref, k_ref, s_ref, tmax_ref, tile, keep_fn):
    for hh in range(HEADS_PER_STEP):
        lanes = slice(hh * LANES, (hh + 1) * LANES)
        q = q_ref[:, lanes].astype(BF16)
        tmax = [None, None]
        for ch in range(KEY_CHUNKS):
            row0 = pl.multiple_of((tile * KEY_CHUNKS + ch) * KEY_CHUNK, KEY_CHUNK)
            halves = _lo_hi(k_ref[pl.ds(row0, KEY_CHUNK), lanes])
            for c in range(2):
                s = _nt(halves[c], q)
                keep = keep_fn(2 * hh + c, ch)
                if keep is not None:
                    s = jnp.where(keep, s, NEG_INF)
                s_ref[2 * hh + c, ch * KEY_CHUNK:(ch + 1) * KEY_CHUNK] = s
                top = jnp.max(s, axis=0, keepdims=True)
                tmax[c] = top if tmax[c] is None else jnp.maximum(tmax[c], top)
        for c in range(2):
            tmax_ref[2 * hh + c] = tmax[c]


def _softmax_stage(s_ref, tmax_ref, m_ref, alpha_ref, p_ref):
    for mi in range(2 * HEADS_PER_STEP):
        m_old = m_ref[mi]
        m_new = jnp.maximum(m_old, tmax_ref[mi])
        alpha_ref[mi] = jnp.exp2(m_old - m_new)
        m_ref[mi] = m_new
        p_ref[mi] = jnp.exp2(s_ref[mi] - m_new).astype(BF16)


def _values_stage(vt_of, tile, alpha_ref, p_ref, acc_ref):
    for mi in range(2 * HEADS_PER_STEP):
        pv = None
        for ch in range(KEY_CHUNKS):
            term = _nn(vt_of(mi, tile * KEY_CHUNKS + ch), p_ref[mi, ch * KEY_CHUNK:(ch + 1) * KEY_CHUNK])
            pv = term if pv is None else pv + term
        acc_ref[mi] = alpha_ref[mi] * acc_ref[mi] + pv


def _flash_pipeline(i, q_ref, k_ref, vt_of, s_ref, tmax_ref, m_ref, alpha_ref, p_ref, acc_ref, diag_keep, past_keep):
    m_ref[...] = jnp.full(m_ref.shape, M_INIT, F32)
    acc_ref[...] = jnp.zeros(acc_ref.shape, F32)

    def tile_at(step):
        return jnp.where(step == 0, i, step - 1)

    def values(step):
        _values_stage(vt_of, tile_at(step), alpha_ref, p_ref, acc_ref)

    def softmax():
        _softmax_stage(s_ref, tmax_ref, m_ref, alpha_ref, p_ref)

    def scores(tile, keep_fn):
        _scores_stage(q_ref, k_ref, s_ref, tmax_ref, tile, keep_fn)

    scores(i, diag_keep)

    @pl.when(i > 0)
    def _fill():
        softmax()
        scores(0, functools.partial(past_keep, 0))

    def body(t, carry):
        values(t - 1)
        softmax()
        scores(t, functools.partial(past_keep, t))
        return carry

    lax.fori_loop(1, i, body, 0)

    @pl.when(i > 0)
    def _drain():
        values(i - 1)

    softmax()
    values(i)


def _chunk_iotas(ch):
    shape = (KEY_CHUNK, ATTN_BLOCK)
    return (lax.broadcasted_iota(jnp.int32, shape, 0) + ch * KEY_CHUNK, lax.broadcasted_iota(jnp.int32, shape, 1))


def _flash_scratch(acc_rows):
    maps = 2 * HEADS_PER_STEP
    blk = ATTN_BLOCK
    stat = pltpu.VMEM((maps, 1, blk), F32)
    return [pltpu.VMEM((maps, blk, blk), F32), stat, stat, stat, pltpu.VMEM((maps, blk, blk), BF16),
            pltpu.VMEM((maps, acc_rows, blk), F32)]


def _flash_specs(seq, nqb, vt_rows):
    width = LANES * HEADS_PER_STEP
    q_spec = pl.BlockSpec((ATTN_BLOCK, width), lambda b, h, i: (b * nqb + i, h))
    k_spec = pl.BlockSpec((seq, width), lambda b, h, i: (b, h))
    vt_spec = pl.BlockSpec((1, seq // KEY_CHUNK, vt_rows * HEADS_PER_STEP, KEY_CHUNK), lambda b, h, i: (b, 0, h, 0))
    return q_spec, k_spec, vt_spec


def _diff_attn_kernel(lq1_ref, lk1_ref, lq2_ref, lk2_ref, subg_ref, q_ref, k_ref, vt_ref, o_ref,
                      s_ref, tmax_ref, m_ref, alpha_ref, p_ref, acc_ref, *, lambda_init):
    i = pl.program_id(2)
    rows = LANES + BF16_ROWS

    def vt_of(mi, chunk):
        return vt_ref[0, chunk, (mi // 2) * rows:(mi // 2 + 1) * rows]

    def diag_keep(mi, ch):
        key, qry = _chunk_iotas(ch)
        return key <= qry

    _flash_pipeline(i, q_ref, k_ref, vt_of, s_ref, tmax_ref, m_ref, alpha_ref, p_ref, acc_ref, diag_keep,
                    lambda t, mi, ch: None)

    lam = (jnp.exp(jnp.sum(lq1_ref[...] * lk1_ref[...], axis=1, keepdims=True))
           - jnp.exp(jnp.sum(lq2_ref[...] * lk2_ref[...], axis=1, keepdims=True)) + lambda_init)
    for hh in range(HEADS_PER_STEP):
        a1, a2 = acc_ref[2 * hh], acc_ref[2 * hh + 1]
        o = a1[:LANES] / a1[LANES:LANES + 1] - lam * (a2[:LANES] / a2[LANES:LANES + 1])
        o = o * lax.rsqrt(jnp.mean(o * o, axis=0, keepdims=True) + NORM_EPS) * subg_ref[...]
        o_ref[:, hh * LANES:(hh + 1) * LANES] = (o * (1.0 - lambda_init)).T.astype(o_ref.dtype)


def _diff_attention(q, k, vt, lam_params, sub_g, *, batch, lambda_init):
    t = q.shape[0]
    seq = t // batch
    nqb = seq // ATTN_BLOCK
    rows = LANES + BF16_ROWS
    small = pl.BlockSpec((1, HEAD_DIM), lambda b, h, i: (0, 0))
    q_spec, k_spec, vt_spec = _flash_specs(seq, nqb, rows)
    return pl.pallas_call(
        functools.partial(_diff_attn_kernel, lambda_init=lambda_init),
        grid=(batch, PAIRS // HEADS_PER_STEP, nqb),
        in_specs=[small, small, small, small, pl.BlockSpec((LANES, 1), lambda b, h, i: (0, 0)),
                  q_spec, k_spec, vt_spec],
        out_specs=q_spec,
        out_shape=jax.ShapeDtypeStruct((t, D_MODEL), BF16),
        scratch_shapes=_flash_scratch(rows),
        compiler_params=pltpu.CompilerParams(dimension_semantics=("arbitrary",) * 3, vmem_limit_bytes=VMEM_LIMIT),
    )(*[p.reshape(1, HEAD_DIM) for p in lam_params], sub_g.reshape(LANES, 1), q, k, vt)


def _moba_select(gate, own):
    nb, width = gate.shape
    groups = range(nb // SUBLANES)
    sub = lax.broadcasted_iota(jnp.int32, (SUBLANES, width), 0)
    ids = [sub + SUBLANES * r for r in groups]
    g = [jnp.where(ids[r] < own, gate[SUBLANES * r:SUBLANES * (r + 1)], NEG_INF) for r in groups]
    sel = [jnp.zeros((SUBLANES, width), F32) for _ in groups]

    def over_blocks(parts, op):
        x = functools.reduce(op, parts)
        for shift in (4, 2, 1):
            x = op(x, pltpu.roll(x, shift, 0))
        return x

    for t in range(MOBA_TOPK):
        best = over_blocks(g, jnp.maximum)
        first = over_blocks([jnp.where(g[r] == best, ids[r], nb) for r in groups], jnp.minimum)
        valid = jnp.where(t < own, 1.0, 0.0)
        for r in groups:
            hit = ids[r] == first
            sel[r] = jnp.maximum(sel[r], jnp.where(hit, valid, 0.0))
            g[r] = jnp.where(hit, NEG_INF, g[r])
    return jnp.concatenate(sel, axis=0)


def _moba_attn_kernel(q_ref, k_ref, vt_ref, km_ref, o_ref,
                      sel_ref, s_ref, tmax_ref, m_ref, alpha_ref, p_ref, acc_ref):
    i = pl.program_id(2)
    half = HEAD_DIM + BF16_ROWS
    maps = 2 * HEADS_PER_STEP
    gates = []
    for hh in range(HEADS_PER_STEP):
        lanes = slice(hh * LANES, (hh + 1) * LANES)
        qf = q_ref[:, lanes]
        gates += [_nt(km_half, qf, lax.Precision.HIGHEST) for km_half in _lo_hi(km_ref[0, :, lanes])]
    own = i * KEY_CHUNKS + (lax.broadcasted_iota(jnp.int32, (1, maps * ATTN_BLOCK), 1) % ATTN_BLOCK) // MOBA_BLOCK
    sel = _moba_select(jnp.concatenate(gates, axis=1), own)
    for mi in range(maps):
        sel_ref[mi] = sel[:, mi * ATTN_BLOCK:(mi + 1) * ATTN_BLOCK]

    def vt_of(mi, chunk):
        return vt_ref[0, chunk, mi * half:(mi + 1) * half]

    def past_keep(t, mi, ch):
        return sel_ref[mi, pl.ds(t * KEY_CHUNKS + ch, 1), :] > 0.0

    def diag_keep(mi, ch):
        key, qry = _chunk_iotas(ch)
        same_block = qry < (ch + 1) * MOBA_BLOCK
        keep = (key <= qry) & same_block
        if ch + 1 < KEY_CHUNKS:
            keep = keep | (jnp.logical_not(same_block) & past_keep(i, mi, ch))
        return keep

    _flash_pipeline(i, q_ref, k_ref, vt_of, s_ref, tmax_ref, m_ref, alpha_ref, p_ref, acc_ref, diag_keep, past_keep)

    for hh in range(HEADS_PER_STEP):
        a1, a2 = acc_ref[2 * hh], acc_ref[2 * hh + 1]
        o = jnp.concatenate([a1[:HEAD_DIM] / a1[HEAD_DIM:HEAD_DIM + 1], a2[:HEAD_DIM] / a2[HEAD_DIM:HEAD_DIM + 1]],
                            axis=0)
        o_ref[:, hh * LANES:(hh + 1) * LANES] = o.T.astype(o_ref.dtype)


def _moba_attention(q, k, vt, kmean, *, batch):
    t = q.shape[0]
    seq = t // batch
    nqb = seq // ATTN_BLOCK
    nb = seq // MOBA_BLOCK
    half = HEAD_DIM + BF16_ROWS
    q_spec, k_spec, vt_spec = _flash_specs(seq, nqb, 2 * half)
    return pl.pallas_call(
        _moba_attn_kernel,
        grid=(batch, PAIRS // HEADS_PER_STEP, nqb),
        in_specs=[q_spec, k_spec, vt_spec,
                  pl.BlockSpec((1, nb, LANES * HEADS_PER_STEP), lambda b, h, i: (b, 0, h))],
        out_specs=q_spec,
        out_shape=jax.ShapeDtypeStruct((t, D_MODEL), BF16),
        scratch_shapes=[pltpu.VMEM((2 * HEADS_PER_STEP, nb, ATTN_BLOCK), F32)] + _flash_scratch(half),
        compiler_params=pltpu.CompilerParams(dimension_semantics=("arbitrary",) * 3, vmem_limit_bytes=VMEM_LIMIT),
    )(q, k, vt, kmean)


def _swa_attn_kernel(sink_ref, q_ref, k_ref, vt_ref, o_ref):
    h = pl.program_id(1)
    i = pl.program_id(2)
    sub, w = SWA_SUB_BLOCK, SWA_WINDOW
    subs = SWA_Q_BLOCK // sub
    key = lax.broadcasted_iota(jnp.int32, (sub, sub), 0)
    qry = lax.broadcasted_iota(jnp.int32, (sub, sub), 1)
    own_keep = (key <= qry) & (qry - key < w)
    pkey = lax.broadcasted_iota(jnp.int32, (w, sub), 0)
    pqry = lax.broadcasted_iota(jnp.int32, (w, sub), 1)
    rows = HEAD_DIM + BF16_ROWS
    row0 = pl.multiple_of((h // (PAIRS // SWA_KV_HEADS)) * rows, BF16_ROWS)

    for sb in range(subs):
        blk = i * subs + sb
        q = q_ref[sb * sub:(sb + 1) * sub, :]
        q0 = pl.multiple_of(blk * sub, sub)
        prev0 = pl.multiple_of(jnp.maximum(q0 - w, 0), w)
        k_own = _lo_hi(k_ref[pl.ds(q0, sub), :])
        k_prev = _lo_hi(k_ref[pl.ds(prev0, w), :])
        prev_keep = pkey > pqry
        if sb == 0:
            prev_keep = pkey > pqry + jnp.where(i > 0, 0, w)
        chunk0 = blk * (sub // w)
        vt_prev = vt_ref[0, jnp.maximum(chunk0 - 1, 0), pl.ds(row0, rows), :]
        vt_own = jnp.concatenate([vt_ref[0, chunk0 + c, pl.ds(row0, rows), :] for c in range(sub // w)], axis=1)

        outs = []
        for half in range(2):
            sink = sink_ref[2 * h + half] * LOG2_E
            s_own = jnp.where(own_keep, _nt(k_own[half], q), NEG_INF)
            s_prev = jnp.where(prev_keep, _nt(k_prev[half], q), NEG_INF)
            m = jnp.maximum(jnp.maximum(jnp.max(s_own, axis=0, keepdims=True),
                                        jnp.max(s_prev, axis=0, keepdims=True)), sink)
            acc = (_nn(vt_own, jnp.exp2(s_own - m).astype(BF16))
                   + _nn(vt_prev, jnp.exp2(s_prev - m).astype(BF16)))
            denom = acc[HEAD_DIM:HEAD_DIM + 1] + jnp.exp2(sink - m)
            outs.append(acc[:HEAD_DIM] / denom)
        o_ref[sb * sub:(sb + 1) * sub, :] = jnp.concatenate(outs, axis=0).T.astype(o_ref.dtype)


def _swa_attention(q, kdup, vt, sinks, *, batch):
    t = q.shape[0]
    seq = t // batch
    qb = SWA_Q_BLOCK
    nqb = seq // qb
    per_kv = PAIRS // SWA_KV_HEADS
    vt_rows = SWA_KV_HEADS * (HEAD_DIM + BF16_ROWS)
    return pl.pallas_call(
        _swa_attn_kernel,
        grid=(batch, PAIRS, nqb),
        in_specs=[pl.BlockSpec(memory_space=pltpu.SMEM),
                  pl.BlockSpec((qb, LANES), lambda b, h, i: (b * nqb + i, h)),
                  pl.BlockSpec((seq, LANES), lambda b, h, i: (b, h // per_kv)),
                  pl.BlockSpec((1, seq // SWA_WINDOW, vt_rows, SWA_WINDOW), lambda b, h, i: (b, 0, 0, 0))],
        out_specs=pl.BlockSpec((qb, LANES), lambda b, h, i: (b * nqb + i, h)),
        out_shape=jax.ShapeDtypeStruct((t, D_MODEL), BF16),
        compiler_params=pltpu.CompilerParams(dimension_semantics=("arbitrary",) * 3, vmem_limit_bytes=VMEM_LIMIT),
    )(sinks, q, kdup, vt)


def _post_kernel(*refs, final):
    h_ref, o_ref, wout_ref, g_ref, wup_ref, wdn_ref = refs[:6]
    fg_ref = refs[6] if final else None
    out_ref = refs[-1]
    h1 = h_ref[...] + _nn(o_ref[...], wout_ref[...])
    xn = _rms(h1, g_ref[...]).astype(BF16)
    acc = h1
    for c in range(D_FF // FF_CHUNK):
        lo = c * FF_CHUNK
        a = jnp.maximum(_nn(xn, wup_ref[:, lo:lo + FF_CHUNK]), 0.0)
        acc = acc + _nn((a * a).astype(BF16), wdn_ref[lo:lo + FF_CHUNK, :])
    if final:
        acc = _rms(acc, fg_ref[...])
    out_ref[...] = acc


def _post(h, o, w_out, g, w_up, w_down, final_g=None):
    t = h.shape[0]
    rows = PROJ_ROWS
    const = lambda i: (0, 0)
    row = lambda i: (i, 0)
    in_specs = [pl.BlockSpec((rows, D_MODEL), row), pl.BlockSpec((rows, D_MODEL), row),
                pl.BlockSpec(w_out.shape, const), pl.BlockSpec((1, D_MODEL), const),
                pl.BlockSpec(w_up.shape, const), pl.BlockSpec(w_down.shape, const)]
    args = [h, o, w_out, g.reshape(1, D_MODEL), w_up, w_down]
    if final_g is not None:
        in_specs.append(pl.BlockSpec((1, D_MODEL), const))
        args.append(final_g.reshape(1, D_MODEL))
    return pl.pallas_call(
        functools.partial(_post_kernel, final=final_g is not None),
        grid=(t // rows,),
        in_specs=in_specs,
        out_specs=pl.BlockSpec((rows, D_MODEL), row),
        out_shape=jax.ShapeDtypeStruct((t, D_MODEL), F32),
        compiler_params=pltpu.CompilerParams(dimension_semantics=("arbitrary",), vmem_limit_bytes=VMEM_LIMIT),
    )(*args)


def kernel(x, positions, attn_norm_g, mlp_norm_g, diff_w_in, diff_w_out, diff_lam_q1, diff_lam_k1, diff_lam_q2,
           diff_lam_k2, diff_subln_g, moba_w_in, moba_w_out, swa_w_in, swa_b_in, swa_sinks, swa_w_out, mlp_w_up,
           mlp_w_down, final_norm_g):
    batch, seq, _ = x.shape
    depth = attn_norm_g.shape[0]
    tables = _rope_tables(positions)
    h = x.reshape(batch * seq, D_MODEL)
    qk = 2 * D_MODEL
    log2_scale = Q_SCALE * LOG2_E

    for i in range(depth):
        mixer, slot = i % N_MIXERS, i // N_MIXERS
        if mixer == 0:
            w_in = diff_w_in[slot]
            q, k, vt = _project(h, attn_norm_g[i], w_in[:, :qk].astype(BF16), w_in[:, qk:].T.astype(BF16), tables,
                                batch=batch, nq=D_MODEL, kv_block=KEY_CHUNK, q_scale=log2_scale, v_group=LANES)
            o = _diff_attention(q, k, vt,
                                (diff_lam_q1[slot], diff_lam_k1[slot], diff_lam_q2[slot], diff_lam_k2[slot]),
                                diff_subln_g[slot], batch=batch, lambda_init=0.8 - 0.6 * math.exp(-0.3 * i))
            w_out = diff_w_out[slot]
        elif mixer == 1:
            w_in = moba_w_in[slot]
            q, k, vt, kmean = _project(h, attn_norm_g[i], w_in[:, :qk].astype(BF16), w_in[:, qk:].T.astype(BF16),
                                       tables, batch=batch, nq=D_MODEL, kv_block=KEY_CHUNK, q_scale=log2_scale,
                                       v_group=HEAD_DIM, q_dtype=F32, with_kmean=True)
            o = _moba_attention(q, k, vt, kmean.reshape(batch, seq // MOBA_BLOCK, D_MODEL), batch=batch)
            w_out = moba_w_out[slot]
        else:
            w_in, b_in = swa_w_in[slot], swa_b_in[slot]
            dup = lambda a: jnp.concatenate(
                [a[..., D_MODEL + kv * HEAD_DIM:D_MODEL + (kv + 1) * HEAD_DIM]
                 for kv in range(SWA_KV_HEADS) for _ in range(2)], axis=-1)
            v0 = D_MODEL + SWA_KV_HEADS * HEAD_DIM
            wqk = jnp.concatenate([w_in[:, :D_MODEL], dup(w_in)], axis=1).astype(BF16)
            bqk = jnp.concatenate([b_in[:D_MODEL], dup(b_in)])
            q, k, vt = _project(h, attn_norm_g[i], wqk, w_in[:, v0:].T.astype(BF16), tables, batch=batch,
                                nq=D_MODEL, kv_block=SWA_WINDOW, q_scale=log2_scale, v_group=HEAD_DIM,
                                bias=(bqk, b_in[v0:]))
            o = _swa_attention(q, k, vt, swa_sinks[slot], batch=batch)
            w_out = swa_w_out[slot]
        h = _post(h, o, w_out.astype(BF16), mlp_norm_g[i], mlp_w_up[i].astype(BF16), mlp_w_down[i].astype(BF16),
                  final_norm_g if i == depth - 1 else None)
    return h.reshape(batch, seq, D_MODEL)
```

```python
import functools
import math

import jax
import jax.numpy as jnp
from jax import lax
from jax.experimental import pallas as pl
from jax.experimental.pallas import tpu as pltpu

F32 = jnp.float32
BF16 = jnp.bfloat16

D_MODEL = 1024
HEAD_DIM = 64
ROT_DIM = HEAD_DIM // 4
ROT_HALF = ROT_DIM // 2
ROPE_THETA = 500000.0
NORM_EPS = 1e-6
D_FF = 4 * D_MODEL
Q_SCALE = HEAD_DIM ** -0.5
LOG2_E = math.log2(math.e)
N_MIXERS = 3

LANES = 128
SUBLANES = 8
BF16_ROWS = 16
MXU_COLS = 256
PAIRS = D_MODEL // LANES
MOBA_BLOCK = 256
MOBA_TOPK = 3
SWA_WINDOW = 128
SWA_KV_HEADS = 2

ATTN_BLOCK = 512
KEY_CHUNK = 256
KEY_CHUNKS = ATTN_BLOCK // KEY_CHUNK
HEADS_PER_STEP = 4
SWA_Q_BLOCK = 1024
SWA_SUB_BLOCK = 256
PROJ_ROWS = 512
FF_CHUNK = 1024
VMEM_LIMIT = 56 * 1024 * 1024

NEG_INF = float("-inf")
M_INIT = -1e30


def _nt(a, b, precision=None):
    return lax.dot_general(a, b, (((1,), (1,)), ((), ())), preferred_element_type=F32, precision=precision)


def _nn(a, b):
    return jnp.dot(a, b, preferred_element_type=F32)


def _rms(x, g):
    return x * lax.rsqrt(jnp.mean(x * x, axis=-1, keepdims=True) + NORM_EPS) * g


def _lo_hi(x):
    lane = lax.broadcasted_iota(jnp.int32, (1, LANES), 1)
    zero = jnp.zeros_like(x)
    return jnp.where(lane < HEAD_DIM, x, zero), jnp.where(lane < HEAD_DIM, zero, x)


def _rope_table_kernel(pos_ref, inv_ref, c_ref, sa_ref, sb_ref):
    ang = pos_ref[...].astype(F32) * inv_ref[...]
    lane = lax.broadcasted_iota(jnp.int32, ang.shape, 1) % HEAD_DIM
    cos = jnp.cos(ang)
    sin = jnp.sin(ang)
    c_ref[...] = jnp.where(lane < ROT_DIM, cos, 1.0)
    sa_ref[...] = jnp.where(lane < ROT_HALF, -sin, 0.0)
    sb_ref[...] = jnp.where((lane >= ROT_HALF) & (lane < ROT_DIM), sin, 0.0)


def _rope_tables(positions):
    t = positions.size
    rows = 1024
    inv = ROPE_THETA ** (-jnp.arange(0, ROT_DIM, 2, dtype=F32) / ROT_DIM)
    lane = jnp.arange(LANES) % HEAD_DIM
    inv_lane = jnp.where(lane < ROT_DIM, inv[lane % ROT_HALF], 0.0).reshape(1, LANES)
    tab = jax.ShapeDtypeStruct((t, LANES), F32)
    spec = pl.BlockSpec((rows, LANES), lambda i: (i, 0))
    return pl.pallas_call(
        _rope_table_kernel,
        grid=(t // rows,),
        in_specs=[pl.BlockSpec((rows, 1), lambda i: (i, 0)), pl.BlockSpec((1, LANES), lambda i: (0, 0))],
        out_specs=[spec, spec, spec],
        out_shape=[tab, tab, tab],
    )(positions.reshape(t, 1), inv_lane)


def _proj_kernel(*refs, nq, nk, kv_block, q_scale, v_group, has_bias, with_kmean):
    it = iter(refs)
    h_ref, g_ref, wqk_ref, wvt_ref = next(it), next(it), next(it), next(it)
    bqk_ref = next(it) if has_bias else None
    bv_ref = next(it) if has_bias else None
    c_ref, sa_ref, sb_ref = next(it), next(it), next(it)
    q_ref, k_ref, vt_ref = next(it), next(it), next(it)
    km_ref = next(it) if with_kmean else None

    rows = h_ref.shape[0]
    xn = _rms(h_ref[...], g_ref[...]).astype(BF16)
    cos, sin_a, sin_b = c_ref[...], sa_ref[...], sb_ref[...]

    for c in range((nq + nk) // LANES):
        col = c * LANES
        if c % 2 == 0:
            wide = _nn(xn, wqk_ref[:, col:col + MXU_COLS])
            if has_bias:
                wide = wide + bqk_ref[:, col:col + MXU_COLS]
        y = wide[:, (c % 2) * LANES:(c % 2 + 1) * LANES]
        y = y * cos + pltpu.roll(y, LANES - ROT_HALF, 1) * sin_a + pltpu.roll(y, ROT_HALF, 1) * sin_b
        if col < nq:
            q_ref[:, col:col + LANES] = (y * q_scale).astype(q_ref.dtype)
        else:
            kc = col - nq
            k_ref[:, kc:kc + LANES] = y.astype(k_ref.dtype)
            if with_kmean:
                for r in range(rows // MOBA_BLOCK):
                    blk = y[r * MOBA_BLOCK:(r + 1) * MOBA_BLOCK]
                    km_ref[0, r:r + 1, kc:kc + LANES] = jnp.sum(blk, axis=0, keepdims=True) * (1.0 / MOBA_BLOCK)

    vt = _nt(wvt_ref[...], xn)
    if has_bias:
        vt = vt + bv_ref[...]
    nv = vt.shape[0]
    for r in range(rows // kv_block):
        blk = vt[:, r * kv_block:(r + 1) * kv_block].astype(vt_ref.dtype)
        if v_group is None:
            vt_ref[0, r] = blk
        else:
            stride = v_group + BF16_ROWS
            for g in range(nv // v_group):
                vt_ref[0, r, g * stride:g * stride + v_group] = blk[g * v_group:(g + 1) * v_group]
                vt_ref[0, r, g * stride + v_group:(g + 1) * stride] = jnp.ones((BF16_ROWS, kv_block), vt_ref.dtype)


def _project(h, g, wqk, wvt, tables, *, batch, nq, kv_block, q_scale, v_group=None, q_dtype=BF16, bias=None,
             with_kmean=False):
    t = h.shape[0]
    rows = PROJ_ROWS
    nk = wqk.shape[1] - nq
    nv = wvt.shape[0]
    nv_out = nv if v_group is None else nv // v_group * (v_group + BF16_ROWS)
    seq = t // batch
    tiles_per_seq = seq // rows
    chunks = rows // kv_block

    const = lambda i: (0, 0)
    row = lambda i: (i, 0)
    in_specs = [pl.BlockSpec((rows, D_MODEL), row), pl.BlockSpec((1, D_MODEL), const),
                pl.BlockSpec(wqk.shape, const), pl.BlockSpec(wvt.shape, const)]
    args = [h, g.reshape(1, D_MODEL), wqk, wvt]
    if bias is not None:
        in_specs += [pl.BlockSpec((1, nq + nk), const), pl.BlockSpec((nv, 1), const)]
        args += [bias[0].reshape(1, nq + nk), bias[1].reshape(nv, 1)]
    in_specs += [pl.BlockSpec((rows, LANES), row)] * 3
    args += list(tables)

    out_shape = [jax.ShapeDtypeStruct((t, nq), q_dtype), jax.ShapeDtypeStruct((t, nk), BF16),
                 jax.ShapeDtypeStruct((batch, seq // kv_block, nv_out, kv_block), BF16)]
    out_specs = [pl.BlockSpec((rows, nq), row), pl.BlockSpec((rows, nk), row),
                 pl.BlockSpec((1, chunks, nv_out, kv_block),
                              lambda i: (i // tiles_per_seq, i % tiles_per_seq, 0, 0))]
    if with_kmean:
        out_shape.append(jax.ShapeDtypeStruct((t // rows, rows // MOBA_BLOCK, nk), F32))
        out_specs.append(pl.BlockSpec((1, rows // MOBA_BLOCK, nk), lambda i: (i, 0, 0)))

    return pl.pallas_call(
        functools.partial(_proj_kernel, nq=nq, nk=nk, kv_block=kv_block, q_scale=q_scale, v_group=v_group,
                          has_bias=bias is not None, with_kmean=with_kmean),
        grid=(t // rows,),
        in_specs=in_specs, out_specs=out_specs, out_shape=out_shape,
        compiler_params=pltpu.CompilerParams(dimension_semantics=("arbitrary",), vmem_limit_bytes=VMEM_LIMIT),
    )(*args)


def _scores_stage(q_ref, k_ref, s_ref, tmax_ref, tile, keep_fn):
    for hh in range(HEADS_PER_STEP):
        lanes = slice(hh * LANES, (hh + 1) * LANES)
        q = q_ref[:, lanes].astype(BF16)
        tmax = [None, None]
        for ch in range(KEY_CHUNKS):
            row0 = pl.multiple_of((tile * KEY_CHUNKS + ch) * KEY_CHUNK, KEY_CHUNK)
            halves = _lo_hi(k_ref[pl.ds(row0, KEY_CHUNK), lanes])
            for c in range(2):
                s = _nt(halves[c], q)
                keep = keep_fn(2 * hh + c, ch)
                if keep is not None:
                    s = jnp.where(keep, s, NEG_INF)
                s_ref[2 * hh + c, ch * KEY_CHUNK:(ch + 1) * KEY_CHUNK] = s
                top = jnp.max(s, axis=0, keepdims=True)
                tmax[c] = top if tmax[c] is None else jnp.maximum(tmax[c], top)
        for c in range(2):
            tmax_ref[2 * hh + c] = tmax[c]


def _softmax_stage(s_ref, tmax_ref, m_ref, alpha_ref, p_ref):
    for mi in range(2 * HEADS_PER_STEP):
        m_old = m_ref[mi]
        m_new = jnp.maximum(m_old, tmax_ref[mi])
        alpha_ref[mi] = jnp.exp2(m_old - m_new)
        m_ref[mi] = m_new
        p_ref[mi] = jnp.exp2(s_ref[mi] - m_new).astype(BF16)


def _values_stage(vt_of, tile, alpha_ref, p_ref, acc_ref):
    for mi in range(2 * HEADS_PER_STEP):
        pv = None
        for ch in range(KEY_CHUNKS):
            term = _nn(vt_of(mi, tile * KEY_CHUNKS + ch), p_ref[mi, ch * KEY_CHUNK:(ch + 1) * KEY_CHUNK])
            pv = term if pv is None else pv + term
        acc_ref[mi] = alpha_ref[mi] * acc_ref[mi] + pv


def _flash_pipeline(i, q_ref, k_ref, vt_of, s_ref, tmax_ref, m_ref, alpha_ref, p_ref, acc_ref, diag_keep, past_keep):
    m_ref[...] = jnp.full(m_ref.shape, M_INIT, F32)
    alpha_ref[...] = jnp.ones(alpha_ref.shape, F32)
    p_ref[...] = jnp.zeros(p_ref.shape, BF16)
    acc_ref[...] = jnp.zeros(acc_ref.shape, F32)

    def tile_at(step):
        return jnp.where(step == 0, i, step - 1)

    def values(step):
        _values_stage(vt_of, tile_at(step), alpha_ref, p_ref, acc_ref)

    def softmax():
        _softmax_stage(s_ref, tmax_ref, m_ref, alpha_ref, p_ref)

    def scores(tile, keep_fn):
        _scores_stage(q_ref, k_ref, s_ref, tmax_ref, tile, keep_fn)

    scores(i, diag_keep)

    def body(t, carry):
        values(jnp.maximum(t - 1, 0))
        softmax()
        scores(t, functools.partial(past_keep, t))
        return carry

    lax.fori_loop(0, i, body, 0)
    values(jnp.maximum(i - 1, 0))
    softmax()
    values(i)


def _chunk_iotas(ch):
    shape = (KEY_CHUNK, ATTN_BLOCK)
    return (lax.broadcasted_iota(jnp.int32, shape, 0) + ch * KEY_CHUNK, lax.broadcasted_iota(jnp.int32, shape, 1))


def _flash_scratch(acc_rows):
    maps = 2 * HEADS_PER_STEP
    blk = ATTN_BLOCK
    stat = pltpu.VMEM((maps, 1, blk), F32)
    return [pltpu.VMEM((maps, blk, blk), F32), stat, stat, stat, pltpu.VMEM((maps, blk, blk), BF16),
            pltpu.VMEM((maps, acc_rows, blk), F32)]


def _flash_specs(seq, nqb, vt_rows):
    width = LANES * HEADS_PER_STEP
    q_spec = pl.BlockSpec((ATTN_BLOCK, width), lambda b, h, i: (b * nqb + i, h))
    once = pl.Buffered(1)
    k_spec = pl.BlockSpec((seq, width), lambda b, h, i: (b, h), pipeline_mode=once)
    vt_spec = pl.BlockSpec((1, seq // KEY_CHUNK, vt_rows * HEADS_PER_STEP, KEY_CHUNK), lambda b, h, i: (b, 0, h, 0),
                           pipeline_mode=once)
    return q_spec, k_spec, vt_spec


def _diff_attn_kernel(lq1_ref, lk1_ref, lq2_ref, lk2_ref, subg_ref, q_ref, k_ref, vt_ref, o_ref,
                      s_ref, tmax_ref, m_ref, alpha_ref, p_ref, acc_ref, *, lambda_init):
    i = pl.program_id(2)
    rows = LANES + BF16_ROWS

    def vt_of(mi, chunk):
        return vt_ref[0, chunk, (mi // 2) * rows:(mi // 2 + 1) * rows]

    def diag_keep(mi, ch):
        key, qry = _chunk_iotas(ch)
        return key <= qry

    _flash_pipeline(i, q_ref, k_ref, vt_of, s_ref, tmax_ref, m_ref, alpha_ref, p_ref, acc_ref, diag_keep,
                    lambda t, mi, ch: None)

    lam = (jnp.exp(jnp.sum(lq1_ref[...] * lk1_ref[...], axis=1, keepdims=True))
           - jnp.exp(jnp.sum(lq2_ref[...] * lk2_ref[...], axis=1, keepdims=True)) + lambda_init)
    for hh in range(HEADS_PER_STEP):
        a1, a2 = acc_ref[2 * hh], acc_ref[2 * hh + 1]
        o = a1[:LANES] / a1[LANES:LANES + 1] - lam * (a2[:LANES] / a2[LANES:LANES + 1])
        o = o * lax.rsqrt(jnp.mean(o * o, axis=0, keepdims=True) + NORM_EPS) * subg_ref[...]
        o_ref[:, hh * LANES:(hh + 1) * LANES] = (o * (1.0 - lambda_init)).T.astype(o_ref.dtype)


def _diff_attention(q, k, vt, lam_params, sub_g, *, batch, lambda_init):
    t = q.shape[0]
    seq = t // batch
    nqb = seq // ATTN_BLOCK
    rows = LANES + BF16_ROWS
    small = pl.BlockSpec((1, HEAD_DIM), lambda b, h, i: (0, 0))
    q_spec, k_spec, vt_spec = _flash_specs(seq, nqb, rows)
    return pl.pallas_call(
        functools.partial(_diff_attn_kernel, lambda_init=lambda_init),
        grid=(batch, PAIRS // HEADS_PER_STEP, nqb),
        in_specs=[small, small, small, small, pl.BlockSpec((LANES, 1), lambda b, h, i: (0, 0)),
                  q_spec, k_spec, vt_spec],
        out_specs=q_spec,
        out_shape=jax.ShapeDtypeStruct((t, D_MODEL), BF16),
        scratch_shapes=_flash_scratch(rows),
        compiler_params=pltpu.CompilerParams(dimension_semantics=("arbitrary",) * 3, vmem_limit_bytes=VMEM_LIMIT),
    )(*[p.reshape(1, HEAD_DIM) for p in lam_params], sub_g.reshape(LANES, 1), q, k, vt)


def _moba_select(gate, own):
    nb, width = gate.shape
    groups = range(nb // SUBLANES)
    sub = lax.broadcasted_iota(jnp.int32, (SUBLANES, width), 0)
    ids = [sub + SUBLANES * r for r in groups]
    g = [jnp.where(ids[r] < own, gate[SUBLANES * r:SUBLANES * (r + 1)], NEG_INF) for r in groups]
    sel = [jnp.zeros((SUBLANES, width), F32) for _ in groups]

    def over_blocks(parts, op):
        x = functools.reduce(op, parts)
        for shift in (4, 2, 1):
            x = op(x, pltpu.roll(x, shift, 0))
        return x

    for t in range(MOBA_TOPK):
        best = over_blocks(g, jnp.maximum)
        first = over_blocks([jnp.where(g[r] == best, ids[r], nb) for r in groups], jnp.minimum)
        valid = jnp.where(t < own, 1.0, 0.0)
        for r in groups:
            hit = ids[r] == first
            sel[r] = jnp.maximum(sel[r], jnp.where(hit, valid, 0.0))
            g[r] = jnp.where(hit, NEG_INF, g[r])
    return jnp.concatenate(sel, axis=0)


def _moba_attn_kernel(q_ref, k_ref, vt_ref, km_ref, o_ref,
                      sel_ref, s_ref, tmax_ref, m_ref, alpha_ref, p_ref, acc_ref):
    i = pl.program_id(2)
    half = HEAD_DIM + BF16_ROWS
    maps = 2 * HEADS_PER_STEP
    gates = []
    for hh in range(HEADS_PER_STEP):
        lanes = slice(hh * LANES, (hh + 1) * LANES)
        qf = q_ref[:, lanes]
        gates += [_nt(km_half, qf, lax.Precision.HIGHEST) for km_half in _lo_hi(km_ref[0, :, lanes])]
    own = i * KEY_CHUNKS + (lax.broadcasted_iota(jnp.int32, (1, maps * ATTN_BLOCK), 1) % ATTN_BLOCK) // MOBA_BLOCK
    sel = _moba_select(jnp.concatenate(gates, axis=1), own)
    for mi in range(maps):
        sel_ref[mi] = sel[:, mi * ATTN_BLOCK:(mi + 1) * ATTN_BLOCK]

    def vt_of(mi, chunk):
        return vt_ref[0, chunk, mi * half:(mi + 1) * half]

    def past_keep(t, mi, ch):
        return sel_ref[mi, pl.ds(t * KEY_CHUNKS + ch, 1), :] > 0.0

    def diag_keep(mi, ch):
        key, qry = _chunk_iotas(ch)
        same_block = qry < (ch + 1) * MOBA_BLOCK
        keep = (key <= qry) & same_block
        if ch + 1 < KEY_CHUNKS:
            keep = keep | (jnp.logical_not(same_block) & past_keep(i, mi, ch))
        return keep

    _flash_pipeline(i, q_ref, k_ref, vt_of, s_ref, tmax_ref, m_ref, alpha_ref, p_ref, acc_ref, diag_keep, past_keep)

    for hh in range(HEADS_PER_STEP):
        a1, a2 = acc_ref[2 * hh], acc_ref[2 * hh + 1]
        o = jnp.concatenate([a1[:HEAD_DIM] / a1[HEAD_DIM:HEAD_DIM + 1], a2[:HEAD_DIM] / a2[HEAD_DIM:HEAD_DIM + 1]],
                            axis=0)
        o_ref[:, hh * LANES:(hh + 1) * LANES] = o.T.astype(o_ref.dtype)


def _moba_attention(q, k, vt, kmean, *, batch):
    t = q.shape[0]
    seq = t // batch
    nqb = seq // ATTN_BLOCK
    nb = seq // MOBA_BLOCK
    half = HEAD_DIM + BF16_ROWS
    q_spec, k_spec, vt_spec = _flash_specs(seq, nqb, 2 * half)
    return pl.pallas_call(
        _moba_attn_kernel,
        grid=(batch, PAIRS // HEADS_PER_STEP, nqb),
        in_specs=[q_spec, k_spec, vt_spec,
                  pl.BlockSpec((1, nb, LANES * HEADS_PER_STEP), lambda b, h, i: (b, 0, h))],
        out_specs=q_spec,
        out_shape=jax.ShapeDtypeStruct((t, D_MODEL), BF16),
        scratch_shapes=[pltpu.VMEM((2 * HEADS_PER_STEP, nb, ATTN_BLOCK), F32)] + _flash_scratch(half),
        compiler_params=pltpu.CompilerParams(dimension_semantics=("arbitrary",) * 3, vmem_limit_bytes=VMEM_LIMIT),
    )(q, k, vt, kmean)


def _swa_attn_kernel(sink_ref, q_ref, k_ref, vt_ref, o_ref):
    h = pl.program_id(1)
    i = pl.program_id(2)
    sub, w = SWA_SUB_BLOCK, SWA_WINDOW
    subs = SWA_Q_BLOCK // sub
    key = lax.broadcasted_iota(jnp.int32, (sub, sub), 0)
    qry = lax.broadcasted_iota(jnp.int32, (sub, sub), 1)
    own_keep = (key <= qry) & (qry - key < w)
    pkey = lax.broadcasted_iota(jnp.int32, (w, sub), 0)
    pqry = lax.broadcasted_iota(jnp.int32, (w, sub), 1)
    rows = HEAD_DIM + BF16_ROWS
    row0 = pl.multiple_of((h // (PAIRS // SWA_KV_HEADS)) * rows, BF16_ROWS)

    for sb in range(subs):
        blk = i * subs + sb
        q = q_ref[sb * sub:(sb + 1) * sub, :]
        q0 = pl.multiple_of(blk * sub, sub)
        prev0 = pl.multiple_of(jnp.maximum(q0 - w, 0), w)
        k_own = _lo_hi(k_ref[pl.ds(q0, sub), :])
        k_prev = _lo_hi(k_ref[pl.ds(prev0, w), :])
        prev_keep = pkey > pqry
        if sb == 0:
            prev_keep = pkey > pqry + jnp.where(i > 0, 0, w)
        chunk0 = blk * (sub // w)
        vt_prev = vt_ref[0, jnp.maximum(chunk0 - 1, 0), pl.ds(row0, rows), :]
        vt_own = jnp.concatenate([vt_ref[0, chunk0 + c, pl.ds(row0, rows), :] for c in range(sub // w)], axis=1)

        outs = []
        for half in range(2):
            sink = sink_ref[2 * h + half] * LOG2_E
            s_own = jnp.where(own_keep, _nt(k_own[half], q), NEG_INF)
            s_prev = jnp.where(prev_keep, _nt(k_prev[half], q), NEG_INF)
            m = jnp.maximum(jnp.maximum(jnp.max(s_own, axis=0, keepdims=True),
                                        jnp.max(s_prev, axis=0, keepdims=True)), sink)
            acc = (_nn(vt_own, jnp.exp2(s_own - m).astype(BF16))
                   + _nn(vt_prev, jnp.exp2(s_prev - m).astype(BF16)))
            denom = acc[HEAD_DIM:HEAD_DIM + 1] + jnp.exp2(sink - m)
            outs.append(acc[:HEAD_DIM] / denom)
        o_ref[sb * sub:(sb + 1) * sub, :] = jnp.concatenate(outs, axis=0).T.astype(o_ref.dtype)


def _swa_attention(q, kdup, vt, sinks, *, batch):
    t = q.shape[0]
    seq = t // batch
    qb = SWA_Q_BLOCK
    nqb = seq // qb
    per_kv = PAIRS // SWA_KV_HEADS
    vt_rows = SWA_KV_HEADS * (HEAD_DIM + BF16_ROWS)
    return pl.pallas_call(
        _swa_attn_kernel,
        grid=(batch, PAIRS, nqb),
        in_specs=[pl.BlockSpec(memory_space=pltpu.SMEM),
                  pl.BlockSpec((qb, LANES), lambda b, h, i: (b * nqb + i, h)),
                  pl.BlockSpec((seq, LANES), lambda b, h, i: (b, h // per_kv)),
                  pl.BlockSpec((1, seq // SWA_WINDOW, vt_rows, SWA_WINDOW), lambda b, h, i: (b, 0, 0, 0))],
        out_specs=pl.BlockSpec((qb, LANES), lambda b, h, i: (b * nqb + i, h)),
        out_shape=jax.ShapeDtypeStruct((t, D_MODEL), BF16),
        compiler_params=pltpu.CompilerParams(dimension_semantics=("arbitrary",) * 3, vmem_limit_bytes=VMEM_LIMIT),
    )(sinks, q, kdup, vt)


def _post_kernel(*refs, final):
    h_ref, o_ref, wout_ref, g_ref, wup_ref, wdn_ref = refs[:6]
    fg_ref = refs[6] if final else None
    out_ref = refs[-1]
    h1 = h_ref[...] + _nn(o_ref[...], wout_ref[...])
    xn = _rms(h1, g_ref[...]).astype(BF16)
    acc = h1
    for c in range(D_FF // FF_CHUNK):
        lo = c * FF_CHUNK
        a = jnp.maximum(_nn(xn, wup_ref[:, lo:lo + FF_CHUNK]), 0.0)
        acc = acc + _nn((a * a).astype(BF16), wdn_ref[lo:lo + FF_CHUNK, :])
    if final:
        acc = _rms(acc, fg_ref[...])
    out_ref[...] = acc


def _post(h, o, w_out, g, w_up, w_down, final_g=None):
    t = h.shape[0]
    rows = PROJ_ROWS
    const = lambda i: (0, 0)
    row = lambda i: (i, 0)
    in_specs = [pl.BlockSpec((rows, D_MODEL), row), pl.BlockSpec((rows, D_MODEL), row),
                pl.BlockSpec(w_out.shape, const), pl.BlockSpec((1, D_MODEL), const),
                pl.BlockSpec(w_up.shape, const), pl.BlockSpec(w_down.shape, const)]
    args = [h, o, w_out, g.reshape(1, D_MODEL), w_up, w_down]
    if final_g is not None:
        in_specs.append(pl.BlockSpec((1, D_MODEL), const))
        args.append(final_g.reshape(1, D_MODEL))
    return pl.pallas_call(
        functools.partial(_post_kernel, final=final_g is not None),
        grid=(t // rows,),
        in_specs=in_specs,
        out_specs=pl.BlockSpec((rows, D_MODEL), row),
        out_shape=jax.ShapeDtypeStruct((t, D_MODEL), F32),
        compiler_params=pltpu.CompilerParams(dimension_semantics=("arbitrary",), vmem_limit_bytes=VMEM_LIMIT),
    )(*args)


def kernel(x, positions, attn_norm_g, mlp_norm_g, diff_w_in, diff_w_out, diff_lam_q1, diff_lam_k1, diff_lam_q2,
           diff_lam_k2, diff_subln_g, moba_w_in, moba_w_out, swa_w_in, swa_b_in, swa_sinks, swa_w_out, mlp_w_up,
           mlp_w_down, final_norm_g):
    batch, seq, _ = x.shape
    depth = attn_norm_g.shape[0]
    tables = _rope_tables(positions)
    h = x.reshape(batch * seq, D_MODEL)
    qk = 2 * D_MODEL
    log2_scale = Q_SCALE * LOG2_E

    for i in range(depth):
        mixer, slot = i % N_MIXERS, i // N_MIXERS
        if mixer == 0:
            w_in = diff_w_in[slot]
            q, k, vt = _project(h, attn_norm_g[i], w_in[:, :qk].astype(BF16), w_in[:, qk:].T.astype(BF16), tables,
                                batch=batch, nq=D_MODEL, kv_block=KEY_CHUNK, q_scale=log2_scale, v_group=LANES)
            o = _diff_attention(q, k, vt,
                                (diff_lam_q1[slot], diff_lam_k1[slot], diff_lam_q2[slot], diff_lam_k2[slot]),
                                diff_subln_g[slot], batch=batch, lambda_init=0.8 - 0.6 * math.exp(-0.3 * i))
            w_out = diff_w_out[slot]
        elif mixer == 1:
            w_in = moba_w_in[slot]
            q, k, vt, kmean = _project(h, attn_norm_g[i], w_in[:, :qk].astype(BF16), w_in[:, qk:].T.astype(BF16),
                                       tables, batch=batch, nq=D_MODEL, kv_block=KEY_CHUNK, q_scale=log2_scale,
                                       v_group=HEAD_DIM, q_dtype=F32, with_kmean=True)
            o = _moba_attention(q, k, vt, kmean.reshape(batch, seq // MOBA_BLOCK, D_MODEL), batch=batch)
            w_out = moba_w_out[slot]
        else:
            w_in, b_in = swa_w_in[slot], swa_b_in[slot]
            dup = lambda a: jnp.concatenate(
                [a[..., D_MODEL + kv * HEAD_DIM:D_MODEL + (kv + 1) * HEAD_DIM]
                 for kv in range(SWA_KV_HEADS) for _ in range(2)], axis=-1)
            v0 = D_MODEL + SWA_KV_HEADS * HEAD_DIM
            wqk = jnp.concatenate([w_in[:, :D_MODEL], dup(w_in)], axis=1).astype(BF16)
            bqk = jnp.concatenate([b_in[:D_MODEL], dup(b_in)])
            q, k, vt = _project(h, attn_norm_g[i], wqk, w_in[:, v0:].T.astype(BF16), tables, batch=batch,
                                nq=D_MODEL, kv_block=SWA_WINDOW, q_scale=log2_scale, v_group=HEAD_DIM,
                                bias=(bqk, b_in[v0:]))
            o = _swa_attention(q, k, vt, swa_sinks[slot], batch=batch)
            w_out = swa_w_out[slot]
        h = _post(h, o, w_out.astype(BF16), mlp_norm_g[i], mlp_w_up[i].astype(BF16), mlp_w_down[i].astype(BF16),
                  final_norm_g if i == depth - 1 else None)
    return h.reshape(batch, seq, D_MODEL)
```

```python
import functools
import math

import jax
import jax.numpy as jnp
from jax import lax
from jax.experimental import pallas as pl
from jax.experimental.pallas import tpu as pltpu

F32 = jnp.float32
BF16 = jnp.bfloat16

D_MODEL = 1024
HEAD_DIM = 64
ROT_DIM = HEAD_DIM // 4
ROT_HALF = ROT_DIM // 2
ROPE_THETA = 500000.0
NORM_EPS = 1e-6
D_FF = 4 * D_MODEL
Q_SCALE = HEAD_DIM ** -0.5
LOG2_E = math.log2(math.e)
N_MIXERS = 3

LANES = 128
SUBLANES = 8
BF16_ROWS = 16
MXU_COLS = 256
PAIRS = D_MODEL // LANES
MOBA_BLOCK = 256
MOBA_TOPK = 3
SWA_WINDOW = 128
SWA_KV_HEADS = 2

ATTN_BLOCK = 512
KEY_CHUNK = 256
KEY_CHUNKS = ATTN_BLOCK // KEY_CHUNK
HEADS_PER_STEP = 4
SWA_Q_BLOCK = 1024
SWA_SUB_BLOCK = 256
PROJ_ROWS = 512
FF_CHUNK = 1024
VMEM_LIMIT = 56 * 1024 * 1024

NEG_INF = float("-inf")
M_INIT = -1e30


def _nt(a, b, precision=None):
    return lax.dot_general(a, b, (((1,), (1,)), ((), ())), preferred_element_type=F32, precision=precision)


def _nn(a, b):
    return jnp.dot(a, b, preferred_element_type=F32)


def _rms(x, g):
    return x * lax.rsqrt(jnp.mean(x * x, axis=-1, keepdims=True) + NORM_EPS) * g


def _lo_hi(x):
    lane = lax.broadcasted_iota(jnp.int32, (1, LANES), 1)
    zero = jnp.zeros_like(x)
    return jnp.where(lane < HEAD_DIM, x, zero), jnp.where(lane < HEAD_DIM, zero, x)


def _rope_table_kernel(pos_ref, inv_ref, c_ref, sa_ref, sb_ref):
    ang = pos_ref[...].astype(F32) * inv_ref[...]
    lane = lax.broadcasted_iota(jnp.int32, ang.shape, 1) % HEAD_DIM
    cos = jnp.cos(ang)
    sin = jnp.sin(ang)
    c_ref[...] = jnp.where(lane < ROT_DIM, cos, 1.0)
    sa_ref[...] = jnp.where(lane < ROT_HALF, -sin, 0.0)
    sb_ref[...] = jnp.where((lane >= ROT_HALF) & (lane < ROT_DIM), sin, 0.0)


def _rope_tables(positions):
    t = positions.size
    rows = 1024
    inv = ROPE_THETA ** (-jnp.arange(0, ROT_DIM, 2, dtype=F32) / ROT_DIM)
    lane = jnp.arange(LANES) % HEAD_DIM
    inv_lane = jnp.where(lane < ROT_DIM, inv[lane % ROT_HALF], 0.0).reshape(1, LANES)
    tab = jax.ShapeDtypeStruct((t, LANES), F32)
    spec = pl.BlockSpec((rows, LANES), lambda i: (i, 0))
    return pl.pallas_call(
        _rope_table_kernel,
        grid=(t // rows,),
        in_specs=[pl.BlockSpec((rows, 1), lambda i: (i, 0)), pl.BlockSpec((1, LANES), lambda i: (0, 0))],
        out_specs=[spec, spec, spec],
        out_shape=[tab, tab, tab],
    )(positions.reshape(t, 1), inv_lane)


def _proj_kernel(*refs, nq, nk, kv_block, q_scale, v_group, has_bias, with_kmean):
    it = iter(refs)
    h_ref, g_ref, wqk_ref, wvt_ref = next(it), next(it), next(it), next(it)
    bqk_ref = next(it) if has_bias else None
    bv_ref = next(it) if has_bias else None
    c_ref, sa_ref, sb_ref = next(it), next(it), next(it)
    q_ref, k_ref, vt_ref = next(it), next(it), next(it)
    km_ref = next(it) if with_kmean else None

    rows = h_ref.shape[0]
    xn = _rms(h_ref[...], g_ref[...]).astype(BF16)
    cos, sin_a, sin_b = c_ref[...], sa_ref[...], sb_ref[...]

    for c in range((nq + nk) // LANES):
        col = c * LANES
        if c % 2 == 0:
            wide = _nn(xn, wqk_ref[:, col:col + MXU_COLS])
            if has_bias:
                wide = wide + bqk_ref[:, col:col + MXU_COLS]
        y = wide[:, (c % 2) * LANES:(c % 2 + 1) * LANES]
        y = y * cos + pltpu.roll(y, LANES - ROT_HALF, 1) * sin_a + pltpu.roll(y, ROT_HALF, 1) * sin_b
        if col < nq:
            q_ref[:, col:col + LANES] = (y * q_scale).astype(q_ref.dtype)
        else:
            kc = col - nq
            k_ref[:, kc:kc + LANES] = y.astype(k_ref.dtype)
            if with_kmean:
                for r in range(rows // MOBA_BLOCK):
                    blk = y[r * MOBA_BLOCK:(r + 1) * MOBA_BLOCK]
                    km_ref[0, r:r + 1, kc:kc + LANES] = jnp.sum(blk, axis=0, keepdims=True) * (1.0 / MOBA_BLOCK)

    vt = _nt(wvt_ref[...], xn)
    if has_bias:
        vt = vt + bv_ref[...]
    nv = vt.shape[0]
    for r in range(rows // kv_block):
        blk = vt[:, r * kv_block:(r + 1) * kv_block].astype(vt_ref.dtype)
        if v_group is None:
            vt_ref[0, r] = blk
        else:
            stride = v_group + BF16_ROWS
            for g in range(nv // v_group):
                vt_ref[0, r, g * stride:g * stride + v_group] = blk[g * v_group:(g + 1) * v_group]
                vt_ref[0, r, g * stride + v_group:(g + 1) * stride] = jnp.ones((BF16_ROWS, kv_block), vt_ref.dtype)


def _project(h, g, wqk, wvt, tables, *, batch, nq, kv_block, q_scale, v_group=None, q_dtype=BF16, bias=None,
             with_kmean=False):
    t = h.shape[0]
    rows = PROJ_ROWS
    nk = wqk.shape[1] - nq
    nv = wvt.shape[0]
    nv_out = nv if v_group is None else nv // v_group * (v_group + BF16_ROWS)
    seq = t // batch
    tiles_per_seq = seq // rows
    chunks = rows // kv_block

    const = lambda i: (0, 0)
    row = lambda i: (i, 0)
    in_specs = [pl.BlockSpec((rows, D_MODEL), row), pl.BlockSpec((1, D_MODEL), const),
                pl.BlockSpec(wqk.shape, const), pl.BlockSpec(wvt.shape, const)]
    args = [h, g.reshape(1, D_MODEL), wqk, wvt]
    if bias is not None:
        in_specs += [pl.BlockSpec((1, nq + nk), const), pl.BlockSpec((nv, 1), const)]
        args += [bias[0].reshape(1, nq + nk), bias[1].reshape(nv, 1)]
    in_specs += [pl.BlockSpec((rows, LANES), row)] * 3
    args += list(tables)

    out_shape = [jax.ShapeDtypeStruct((t, nq), q_dtype), jax.ShapeDtypeStruct((t, nk), BF16),
                 jax.ShapeDtypeStruct((batch, seq // kv_block, nv_out, kv_block), BF16)]
    out_specs = [pl.BlockSpec((rows, nq), row), pl.BlockSpec((rows, nk), row),
                 pl.BlockSpec((1, chunks, nv_out, kv_block),
                              lambda i: (i // tiles_per_seq, i % tiles_per_seq, 0, 0))]
    if with_kmean:
        out_shape.append(jax.ShapeDtypeStruct((t // rows, rows // MOBA_BLOCK, nk), F32))
        out_specs.append(pl.BlockSpec((1, rows // MOBA_BLOCK, nk), lambda i: (i, 0, 0)))

    return pl.pallas_call(
        functools.partial(_proj_kernel, nq=nq, nk=nk, kv_block=kv_block, q_scale=q_scale, v_group=v_group,
                          has_bias=bias is not None, with_kmean=with_kmean),
        grid=(t // rows,),
        in_specs=in_specs, out_specs=out_specs, out_shape=out_shape,
        compiler_params=pltpu.CompilerParams(dimension_semantics=("arbitrary",), vmem_limit_bytes=VMEM_LIMIT),
    )(*args)


def _scores_stage(q_ref, k_ref, s_ref, tmax_ref, tile, keep_fn):
    for hh in range(HEADS_PER_STEP):
        lanes = slice(hh * LANES, (hh + 1) * LANES)
        q = q_ref[:, lanes].astype(BF16)
        tmax = [None, None]
        for ch in range(KEY_CHUNKS):
            row0 = pl.multiple_of((tile * KEY_CHUNKS + ch) * KEY_CHUNK, KEY_CHUNK)
            halves = _lo_hi(k_ref[pl.ds(row0, KEY_CHUNK), lanes])
            for c in range(2):
                s = _nt(halves[c], q)
                keep = keep_fn(2 * hh + c, ch)
                if keep is not None:
                    s = jnp.where(keep, s, NEG_INF)
                s_ref[2 * hh + c, ch * KEY_CHUNK:(ch + 1) * KEY_CHUNK] = s
                top = jnp.max(s, axis=0, keepdims=True)
                tmax[c] = top if tmax[c] is None else jnp.maximum(tmax[c], top)
        for c in range(2):
            tmax_ref[2 * hh + c] = tmax[c]


def _softmax_stage(s_ref, tmax_ref, m_ref, alpha_ref, p_ref):
    for mi in range(2 * HEADS_PER_STEP):
        m_old = m_ref[mi]
        m_new = jnp.maximum(m_old, tmax_ref[mi])
        alpha_ref[mi] = jnp.exp2(m_old - m_new)
        m_ref[mi] = m_new
        p_ref[mi] = jnp.exp2(s_ref[mi] - m_new).astype(BF16)


def _values_stage(vt_of, tile, alpha_ref, p_ref, acc_ref):
    for mi in range(2 * HEADS_PER_STEP):
        pv = None
        for ch in range(KEY_CHUNKS):
            term = _nn(vt_of(mi, tile * KEY_CHUNKS + ch), p_ref[mi, ch * KEY_CHUNK:(ch + 1) * KEY_CHUNK])
            pv = term if pv is None else pv + term
        acc_ref[mi] = alpha_ref[mi] * acc_ref[mi] + pv


def _flash_pipeline(i, q_ref, k_ref, vt_of, s_ref, tmax_ref, m_ref, alpha_ref, p_ref, acc_ref, diag_keep, past_keep):
    m_ref[...] = jnp.full(m_ref.shape, M_INIT, F32)
    alpha_ref[...] = jnp.ones(alpha_ref.shape, F32)

    @pl.when((pl.program_id(0) == 0) & (pl.program_id(1) == 0) & (i == 0))
    def _init():
        p_ref[...] = jnp.zeros(p_ref.shape, BF16)
        acc_ref[...] = jnp.zeros(acc_ref.shape, F32)

    def tile_at(step):
        return jnp.where(step == 0, i, step - 1)

    def values(step):
        _values_stage(vt_of, tile_at(step), alpha_ref, p_ref, acc_ref)

    def softmax():
        _softmax_stage(s_ref, tmax_ref, m_ref, alpha_ref, p_ref)

    def scores(tile, keep_fn):
        _scores_stage(q_ref, k_ref, s_ref, tmax_ref, tile, keep_fn)

    scores(i, diag_keep)

    def body(t, carry):
        values(jnp.maximum(t - 1, 0))
        softmax()
        scores(t, functools.partial(past_keep, t))
        return carry

    lax.fori_loop(0, i, body, 0)
    values(jnp.maximum(i - 1, 0))
    softmax()
    values(i)


def _chunk_iotas(ch):
    shape = (KEY_CHUNK, ATTN_BLOCK)
    return (lax.broadcasted_iota(jnp.int32, shape, 0) + ch * KEY_CHUNK, lax.broadcasted_iota(jnp.int32, shape, 1))


def _flash_scratch(acc_rows):
    maps = 2 * HEADS_PER_STEP
    blk = ATTN_BLOCK
    stat = pltpu.VMEM((maps, 1, blk), F32)
    return [pltpu.VMEM((maps, blk, blk), F32), stat, stat, stat, pltpu.VMEM((maps, blk, blk), BF16),
            pltpu.VMEM((maps, acc_rows, blk), F32)]


def _flash_specs(seq, nqb, vt_rows):
    width = LANES * HEADS_PER_STEP
    q_spec = pl.BlockSpec((ATTN_BLOCK, width), lambda b, h, i: (b * nqb + i, h))
    once = pl.Buffered(1)
    k_spec = pl.BlockSpec((seq, width), lambda b, h, i: (b, h), pipeline_mode=once)
    vt_spec = pl.BlockSpec((1, seq // KEY_CHUNK, vt_rows * HEADS_PER_STEP, KEY_CHUNK), lambda b, h, i: (b, 0, h, 0),
                           pipeline_mode=once)
    return q_spec, k_spec, vt_spec


def _diff_attn_kernel(lq1_ref, lk1_ref, lq2_ref, lk2_ref, subg_ref, q_ref, k_ref, vt_ref, o_ref,
                      s_ref, tmax_ref, m_ref, alpha_ref, p_ref, acc_ref, *, lambda_init):
    i = pl.program_id(2)
    rows = LANES + BF16_ROWS

    def vt_of(mi, chunk):
        return vt_ref[0, chunk, (mi // 2) * rows:(mi // 2 + 1) * rows]

    def diag_keep(mi, ch):
        key, qry = _chunk_iotas(ch)
        return key <= qry

    _flash_pipeline(i, q_ref, k_ref, vt_of, s_ref, tmax_ref, m_ref, alpha_ref, p_ref, acc_ref, diag_keep,
                    lambda t, mi, ch: None)

    lam = (jnp.exp(jnp.sum(lq1_ref[...] * lk1_ref[...], axis=1, keepdims=True))
           - jnp.exp(jnp.sum(lq2_ref[...] * lk2_ref[...], axis=1, keepdims=True)) + lambda_init)
    for hh in range(HEADS_PER_STEP):
        a1, a2 = acc_ref[2 * hh], acc_ref[2 * hh + 1]
        o = a1[:LANES] / a1[LANES:LANES + 1] - lam * (a2[:LANES] / a2[LANES:LANES + 1])
        o = o * lax.rsqrt(jnp.mean(o * o, axis=0, keepdims=True) + NORM_EPS) * subg_ref[...]
        o_ref[:, hh * LANES:(hh + 1) * LANES] = (o * (1.0 - lambda_init)).T.astype(o_ref.dtype)


def _diff_attention(q, k, vt, lam_params, sub_g, *, batch, lambda_init):
    t = q.shape[0]
    seq = t // batch
    nqb = seq // ATTN_BLOCK
    rows = LANES + BF16_ROWS
    small = pl.BlockSpec((1, HEAD_DIM), lambda b, h, i: (0, 0))
    q_spec, k_spec, vt_spec = _flash_specs(seq, nqb, rows)
    return pl.pallas_call(
        functools.partial(_diff_attn_kernel, lambda_init=lambda_init),
        grid=(batch, PAIRS // HEADS_PER_STEP, nqb),
        in_specs=[small, small, small, small, pl.BlockSpec((LANES, 1), lambda b, h, i: (0, 0)),
                  q_spec, k_spec, vt_spec],
        out_specs=q_spec,
        out_shape=jax.ShapeDtypeStruct((t, D_MODEL), BF16),
        scratch_shapes=_flash_scratch(rows),
        compiler_params=pltpu.CompilerParams(dimension_semantics=("arbitrary",) * 3, vmem_limit_bytes=VMEM_LIMIT),
    )(*[p.reshape(1, HEAD_DIM) for p in lam_params], sub_g.reshape(LANES, 1), q, k, vt)


def _moba_select(gate, own):
    nb, width = gate.shape
    groups = range(nb // SUBLANES)
    sub = lax.broadcasted_iota(jnp.int32, (SUBLANES, width), 0)
    ids = [sub + SUBLANES * r for r in groups]
    g = [jnp.where(ids[r] < own, gate[SUBLANES * r:SUBLANES * (r + 1)], NEG_INF) for r in groups]
    sel = [jnp.zeros((SUBLANES, width), F32) for _ in groups]

    def over_blocks(parts, op):
        x = functools.reduce(op, parts)
        for shift in (4, 2, 1):
            x = op(x, pltpu.roll(x, shift, 0))
        return x

    for t in range(MOBA_TOPK):
        best = over_blocks(g, jnp.maximum)
        first = over_blocks([jnp.where(g[r] == best, ids[r], nb) for r in groups], jnp.minimum)
        valid = jnp.where(t < own, 1.0, 0.0)
        for r in groups:
            hit = ids[r] == first
            sel[r] = jnp.maximum(sel[r], jnp.where(hit, valid, 0.0))
            g[r] = jnp.where(hit, NEG_INF, g[r])
    return jnp.concatenate(sel, axis=0)


def _moba_attn_kernel(q_ref, k_ref, vt_ref, km_ref, o_ref,
                      sel_ref, s_ref, tmax_ref, m_ref, alpha_ref, p_ref, acc_ref):
    i = pl.program_id(2)
    half = HEAD_DIM + BF16_ROWS
    maps = 2 * HEADS_PER_STEP
    gates = []
    for hh in range(HEADS_PER_STEP):
        lanes = slice(hh * LANES, (hh + 1) * LANES)
        qf = q_ref[:, lanes]
        gates += [_nt(km_half, qf, lax.Precision.HIGHEST) for km_half in _lo_hi(km_ref[0, :, lanes])]
    own = i * KEY_CHUNKS + (lax.broadcasted_iota(jnp.int32, (1, maps * ATTN_BLOCK), 1) % ATTN_BLOCK) // MOBA_BLOCK
    sel = _moba_select(jnp.concatenate(gates, axis=1), own)
    for mi in range(maps):
        sel_ref[mi] = sel[:, mi * ATTN_BLOCK:(mi + 1) * ATTN_BLOCK]

    def vt_of(mi, chunk):
        return vt_ref[0, chunk, mi * half:(mi + 1) * half]

    def past_keep(t, mi, ch):
        return sel_ref[mi, pl.ds(t * KEY_CHUNKS + ch, 1), :] > 0.0

    def diag_keep(mi, ch):
        key, qry = _chunk_iotas(ch)
        same_block = qry < (ch + 1) * MOBA_BLOCK
        keep = (key <= qry) & same_block
        if ch + 1 < KEY_CHUNKS:
            keep = keep | (jnp.logical_not(same_block) & past_keep(i, mi, ch))
        return keep

    _flash_pipeline(i, q_ref, k_ref, vt_of, s_ref, tmax_ref, m_ref, alpha_ref, p_ref, acc_ref, diag_keep, past_keep)

    for hh in range(HEADS_PER_STEP):
        a1, a2 = acc_ref[2 * hh], acc_ref[2 * hh + 1]
        o = jnp.concatenate([a1[:HEAD_DIM] / a1[HEAD_DIM:HEAD_DIM + 1], a2[:HEAD_DIM] / a2[HEAD_DIM:HEAD_DIM + 1]],
                            axis=0)
        o_ref[:, hh * LANES:(hh + 1) * LANES] = o.T.astype(o_ref.dtype)


def _moba_attention(q, k, vt, kmean, *, batch):
    t = q.shape[0]
    seq = t // batch
    nqb = seq // ATTN_BLOCK
    nb = seq // MOBA_BLOCK
    half = HEAD_DIM + BF16_ROWS
    q_spec, k_spec, vt_spec = _flash_specs(seq, nqb, 2 * half)
    return pl.pallas_call(
        _moba_attn_kernel,
        grid=(batch, PAIRS // HEADS_PER_STEP, nqb),
        in_specs=[q_spec, k_spec, vt_spec,
                  pl.BlockSpec((1, nb, LANES * HEADS_PER_STEP), lambda b, h, i: (b, 0, h))],
        out_specs=q_spec,
        out_shape=jax.ShapeDtypeStruct((t, D_MODEL), BF16),
        scratch_shapes=[pltpu.VMEM((2 * HEADS_PER_STEP, nb, ATTN_BLOCK), F32)] + _flash_scratch(half),
        compiler_params=pltpu.CompilerParams(dimension_semantics=("arbitrary",) * 3, vmem_limit_bytes=VMEM_LIMIT),
    )(q, k, vt, kmean)


def _swa_attn_kernel(sink_ref, q_ref, k_ref, vt_ref, o_ref):
    h = pl.program_id(1)
    i = pl.program_id(2)
    sub, w = SWA_SUB_BLOCK, SWA_WINDOW
    subs = SWA_Q_BLOCK // sub
    key = lax.broadcasted_iota(jnp.int32, (sub, sub), 0)
    qry = lax.broadcasted_iota(jnp.int32, (sub, sub), 1)
    own_keep = (key <= qry) & (qry - key < w)
    pkey = lax.broadcasted_iota(jnp.int32, (w, sub), 0)
    pqry = lax.broadcasted_iota(jnp.int32, (w, sub), 1)
    rows = HEAD_DIM + BF16_ROWS
    row0 = pl.multiple_of((h // (PAIRS // SWA_KV_HEADS)) * rows, BF16_ROWS)

    sinks = [sink_ref[2 * h + half] * LOG2_E for half in range(2)]
    chains = []
    for sb in range(subs):
        blk = i * subs + sb
        q = q_ref[sb * sub:(sb + 1) * sub, :]
        q0 = pl.multiple_of(blk * sub, sub)
        prev0 = pl.multiple_of(jnp.maximum(q0 - w, 0), w)
        k_own = _lo_hi(k_ref[pl.ds(q0, sub), :])
        k_prev = _lo_hi(k_ref[pl.ds(prev0, w), :])
        prev_keep = pkey > pqry
        if sb == 0:
            prev_keep = pkey > pqry + jnp.where(i > 0, 0, w)
        for half in range(2):
            s_own = jnp.where(own_keep, _nt(k_own[half], q), NEG_INF)
            s_prev = jnp.where(prev_keep, _nt(k_prev[half], q), NEG_INF)
            m = jnp.maximum(jnp.maximum(jnp.max(s_own, axis=0, keepdims=True),
                                        jnp.max(s_prev, axis=0, keepdims=True)), sinks[half])
            chains.append((s_own, s_prev, m))

    probs = [(jnp.exp2(s_own - m).astype(BF16), jnp.exp2(s_prev - m).astype(BF16)) for s_own, s_prev, m in chains]

    for sb in range(subs):
        chunk0 = (i * subs + sb) * (sub // w)
        vt_prev = vt_ref[0, jnp.maximum(chunk0 - 1, 0), pl.ds(row0, rows), :]
        vt_own = jnp.concatenate([vt_ref[0, chunk0 + c, pl.ds(row0, rows), :] for c in range(sub // w)], axis=1)
        outs = []
        for half in range(2):
            p_own, p_prev = probs[2 * sb + half]
            acc = _nn(vt_own, p_own) + _nn(vt_prev, p_prev)
            denom = acc[HEAD_DIM:HEAD_DIM + 1] + jnp.exp2(sinks[half] - chains[2 * sb + half][2])
            outs.append(acc[:HEAD_DIM] / denom)
        o_ref[sb * sub:(sb + 1) * sub, :] = jnp.concatenate(outs, axis=0).T.astype(o_ref.dtype)


def _swa_attention(q, kdup, vt, sinks, *, batch):
    t = q.shape[0]
    seq = t // batch
    qb = SWA_Q_BLOCK
    nqb = seq // qb
    per_kv = PAIRS // SWA_KV_HEADS
    vt_rows = SWA_KV_HEADS * (HEAD_DIM + BF16_ROWS)
    return pl.pallas_call(
        _swa_attn_kernel,
        grid=(batch, PAIRS, nqb),
        in_specs=[pl.BlockSpec(memory_space=pltpu.SMEM),
                  pl.BlockSpec((qb, LANES), lambda b, h, i: (b * nqb + i, h)),
                  pl.BlockSpec((seq, LANES), lambda b, h, i: (b, h // per_kv)),
                  pl.BlockSpec((1, seq // SWA_WINDOW, vt_rows, SWA_WINDOW), lambda b, h, i: (b, 0, 0, 0))],
        out_specs=pl.BlockSpec((qb, LANES), lambda b, h, i: (b * nqb + i, h)),
        out_shape=jax.ShapeDtypeStruct((t, D_MODEL), BF16),
        compiler_params=pltpu.CompilerParams(dimension_semantics=("arbitrary",) * 3, vmem_limit_bytes=VMEM_LIMIT),
    )(sinks, q, kdup, vt)


def _post_kernel(*refs, final):
    h_ref, o_ref, wout_ref, g_ref, wup_ref, wdn_ref = refs[:6]
    fg_ref = refs[6] if final else None
    out_ref = refs[-1]
    h1 = h_ref[...] + _nn(o_ref[...], wout_ref[...])
    xn = _rms(h1, g_ref[...]).astype(BF16)
    acc = h1
    for c in range(D_FF // FF_CHUNK):
        lo = c * FF_CHUNK
        a = jnp.maximum(_nn(xn, wup_ref[:, lo:lo + FF_CHUNK]), 0.0)
        acc = acc + _nn((a * a).astype(BF16), wdn_ref[lo:lo + FF_CHUNK, :])
    if final:
        acc = _rms(acc, fg_ref[...])
    out_ref[...] = acc


def _post(h, o, w_out, g, w_up, w_down, final_g=None):
    t = h.shape[0]
    rows = PROJ_ROWS
    const = lambda i: (0, 0)
    row = lambda i: (i, 0)
    in_specs = [pl.BlockSpec((rows, D_MODEL), row), pl.BlockSpec((rows, D_MODEL), row),
                pl.BlockSpec(w_out.shape, const), pl.BlockSpec((1, D_MODEL), const),
                pl.BlockSpec(w_up.shape, const), pl.BlockSpec(w_down.shape, const)]
    args = [h, o, w_out, g.reshape(1, D_MODEL), w_up, w_down]
    if final_g is not None:
        in_specs.append(pl.BlockSpec((1, D_MODEL), const))
        args.append(final_g.reshape(1, D_MODEL))
    return pl.pallas_call(
        functools.partial(_post_kernel, final=final_g is not None),
        grid=(t // rows,),
        in_specs=in_specs,
        out_specs=pl.BlockSpec((rows, D_MODEL), row),
        out_shape=jax.ShapeDtypeStruct((t, D_MODEL), F32),
        compiler_params=pltpu.CompilerParams(dimension_semantics=("arbitrary",), vmem_limit_bytes=VMEM_LIMIT),
    )(*args)


def kernel(x, positions, attn_norm_g, mlp_norm_g, diff_w_in, diff_w_out, diff_lam_q1, diff_lam_k1, diff_lam_q2,
           diff_lam_k2, diff_subln_g, moba_w_in, moba_w_out, swa_w_in, swa_b_in, swa_sinks, swa_w_out, mlp_w_up,
           mlp_w_down, final_norm_g):
    batch, seq, _ = x.shape
    depth = attn_norm_g.shape[0]
    tables = _rope_tables(positions)
    h = x.reshape(batch * seq, D_MODEL)
    qk = 2 * D_MODEL
    log2_scale = Q_SCALE * LOG2_E

    for i in range(depth):
        mixer, slot = i % N_MIXERS, i // N_MIXERS
        if mixer == 0:
            w_in = diff_w_in[slot]
            q, k, vt = _project(h, attn_norm_g[i], w_in[:, :qk].astype(BF16), w_in[:, qk:].T.astype(BF16), tables,
                                batch=batch, nq=D_MODEL, kv_block=KEY_CHUNK, q_scale=log2_scale, v_group=LANES)
            o = _diff_attention(q, k, vt,
                                (diff_lam_q1[slot], diff_lam_k1[slot], diff_lam_q2[slot], diff_lam_k2[slot]),
                                diff_subln_g[slot], batch=batch, lambda_init=0.8 - 0.6 * math.exp(-0.3 * i))
            w_out = diff_w_out[slot]
        elif mixer == 1:
            w_in = moba_w_in[slot]
            q, k, vt, kmean = _project(h, attn_norm_g[i], w_in[:, :qk].astype(BF16), w_in[:, qk:].T.astype(BF16),
                                       tables, batch=batch, nq=D_MODEL, kv_block=KEY_CHUNK, q_scale=log2_scale,
                                       v_group=HEAD_DIM, q_dtype=F32, with_kmean=True)
            o = _moba_attention(q, k, vt, kmean.reshape(batch, seq // MOBA_BLOCK, D_MODEL), batch=batch)
            w_out = moba_w_out[slot]
        else:
            w_in, b_in = swa_w_in[slot], swa_b_in[slot]
            dup = lambda a: jnp.concatenate(
                [a[..., D_MODEL + kv * HEAD_DIM:D_MODEL + (kv + 1) * HEAD_DIM]
                 for kv in range(SWA_KV_HEADS) for _ in range(2)], axis=-1)
            v0 = D_MODEL + SWA_KV_HEADS * HEAD_DIM
            wqk = jnp.concatenate([w_in[:, :D_MODEL], dup(w_in)], axis=1).astype(BF16)
            bqk = jnp.concatenate([b_in[:D_MODEL], dup(b_in)])
            q, k, vt = _project(h, attn_norm_g[i], wqk, w_in[:, v0:].T.astype(BF16), tables, batch=batch,
                                nq=D_MODEL, kv_block=SWA_WINDOW, q_scale=log2_scale, v_group=HEAD_DIM,
                                bias=(bqk, b_in[v0:]))
            o = _swa_attention(q, k, vt, swa_sinks[slot], batch=batch)
            w_out = swa_w_out[slot]
        h = _post(h, o, w_out.astype(BF16), mlp_norm_g[i], mlp_w_up[i].astype(BF16), mlp_w_down[i].astype(BF16),
                  final_norm_g if i == depth - 1 else None)
    return h.reshape(batch, seq, D_MODEL)
```

```python
import functools
import math

import jax
import jax.numpy as jnp
from jax import lax
from jax.experimental import pallas as pl
from jax.experimental.pallas import tpu as pltpu

F32 = jnp.float32
BF16 = jnp.bfloat16

D_MODEL = 1024
HEAD_DIM = 64
ROT_DIM = HEAD_DIM // 4
ROT_HALF = ROT_DIM // 2
ROPE_THETA = 500000.0
NORM_EPS = 1e-6
D_FF = 4 * D_MODEL
Q_SCALE = HEAD_DIM ** -0.5
LOG2_E = math.log2(math.e)
N_MIXERS = 3

LANES = 128
SUBLANES = 8
BF16_ROWS = 16
MXU_COLS = 256
PAIRS = D_MODEL // LANES
MOBA_BLOCK = 256
MOBA_TOPK = 3
SWA_WINDOW = 128
SWA_KV_HEADS = 2

ATTN_BLOCK = 512
KEY_CHUNK = 256
KEY_CHUNKS = ATTN_BLOCK // KEY_CHUNK
HEADS_PER_STEP = 4
SWA_Q_BLOCK = 1024
SWA_SUB_BLOCK = 256
PROJ_ROWS = 512
FF_CHUNK = 1024
VMEM_LIMIT = 56 * 1024 * 1024

NEG_INF = float("-inf")
M_INIT = -1e30
MASK_BIAS = -1e30


def _nt(a, b, precision=None):
    return lax.dot_general(a, b, (((1,), (1,)), ((), ())), preferred_element_type=F32, precision=precision)


def _nn(a, b):
    return jnp.dot(a, b, preferred_element_type=F32)


def _rms(x, g):
    return x * lax.rsqrt(jnp.mean(x * x, axis=-1, keepdims=True) + NORM_EPS) * g


def _lo_hi(x):
    lane = lax.broadcasted_iota(jnp.int32, (1, LANES), 1)
    zero = jnp.zeros_like(x)
    return jnp.where(lane < HEAD_DIM, x, zero), jnp.where(lane < HEAD_DIM, zero, x)


def _rope_table_kernel(pos_ref, inv_ref, c_ref, sa_ref, sb_ref):
    ang = pos_ref[...].astype(F32) * inv_ref[...]
    lane = lax.broadcasted_iota(jnp.int32, ang.shape, 1) % HEAD_DIM
    cos = jnp.cos(ang)
    sin = jnp.sin(ang)
    c_ref[...] = jnp.where(lane < ROT_DIM, cos, 1.0)
    sa_ref[...] = jnp.where(lane < ROT_HALF, -sin, 0.0)
    sb_ref[...] = jnp.where((lane >= ROT_HALF) & (lane < ROT_DIM), sin, 0.0)


def _rope_tables(positions):
    t = positions.size
    rows = 1024
    inv = ROPE_THETA ** (-jnp.arange(0, ROT_DIM, 2, dtype=F32) / ROT_DIM)
    lane = jnp.arange(LANES) % HEAD_DIM
    inv_lane = jnp.where(lane < ROT_DIM, inv[lane % ROT_HALF], 0.0).reshape(1, LANES)
    tab = jax.ShapeDtypeStruct((t, LANES), F32)
    spec = pl.BlockSpec((rows, LANES), lambda i: (i, 0))
    return pl.pallas_call(
        _rope_table_kernel,
        grid=(t // rows,),
        in_specs=[pl.BlockSpec((rows, 1), lambda i: (i, 0)), pl.BlockSpec((1, LANES), lambda i: (0, 0))],
        out_specs=[spec, spec, spec],
        out_shape=[tab, tab, tab],
    )(positions.reshape(t, 1), inv_lane)


def _proj_kernel(*refs, nq, nk, kv_block, q_scale, v_group, has_bias, with_kmean):
    it = iter(refs)
    h_ref, g_ref, wqk_ref, wvt_ref = next(it), next(it), next(it), next(it)
    bqk_ref = next(it) if has_bias else None
    bv_ref = next(it) if has_bias else None
    c_ref, sa_ref, sb_ref = next(it), next(it), next(it)
    q_ref, k_ref, vt_ref = next(it), next(it), next(it)
    km_ref = next(it) if with_kmean else None

    rows = h_ref.shape[0]
    xn = _rms(h_ref[...], g_ref[...]).astype(BF16)
    cos, sin_a, sin_b = c_ref[...], sa_ref[...], sb_ref[...]

    for c in range((nq + nk) // LANES):
        col = c * LANES
        if c % 2 == 0:
            wide = _nn(xn, wqk_ref[:, col:col + MXU_COLS])
            if has_bias:
                wide = wide + bqk_ref[:, col:col + MXU_COLS]
        y = wide[:, (c % 2) * LANES:(c % 2 + 1) * LANES]
        y = y * cos + pltpu.roll(y, LANES - ROT_HALF, 1) * sin_a + pltpu.roll(y, ROT_HALF, 1) * sin_b
        if col < nq:
            q_ref[:, col:col + LANES] = (y * q_scale).astype(q_ref.dtype)
        else:
            kc = col - nq
            k_ref[:, kc:kc + LANES] = y.astype(k_ref.dtype)
            if with_kmean:
                for r in range(rows // MOBA_BLOCK):
                    blk = y[r * MOBA_BLOCK:(r + 1) * MOBA_BLOCK]
                    km_ref[0, r:r + 1, kc:kc + LANES] = jnp.sum(blk, axis=0, keepdims=True) * (1.0 / MOBA_BLOCK)

    vt = _nt(wvt_ref[...], xn)
    if has_bias:
        vt = vt + bv_ref[...]
    nv = vt.shape[0]
    for r in range(rows // kv_block):
        blk = vt[:, r * kv_block:(r + 1) * kv_block].astype(vt_ref.dtype)
        if v_group is None:
            vt_ref[0, r] = blk
        else:
            stride = v_group + BF16_ROWS
            for g in range(nv // v_group):
                vt_ref[0, r, g * stride:g * stride + v_group] = blk[g * v_group:(g + 1) * v_group]
                vt_ref[0, r, g * stride + v_group:(g + 1) * stride] = jnp.ones((BF16_ROWS, kv_block), vt_ref.dtype)


def _project(h, g, wqk, wvt, tables, *, batch, nq, kv_block, q_scale, v_group=None, q_dtype=BF16, bias=None,
             with_kmean=False):
    t = h.shape[0]
    rows = PROJ_ROWS
    nk = wqk.shape[1] - nq
    nv = wvt.shape[0]
    nv_out = nv if v_group is None else nv // v_group * (v_group + BF16_ROWS)
    seq = t // batch
    tiles_per_seq = seq // rows
    chunks = rows // kv_block

    const = lambda i: (0, 0)
    row = lambda i: (i, 0)
    in_specs = [pl.BlockSpec((rows, D_MODEL), row), pl.BlockSpec((1, D_MODEL), const),
                pl.BlockSpec(wqk.shape, const), pl.BlockSpec(wvt.shape, const)]
    args = [h, g.reshape(1, D_MODEL), wqk, wvt]
    if bias is not None:
        in_specs += [pl.BlockSpec((1, nq + nk), const), pl.BlockSpec((nv, 1), const)]
        args += [bias[0].reshape(1, nq + nk), bias[1].reshape(nv, 1)]
    in_specs += [pl.BlockSpec((rows, LANES), row)] * 3
    args += list(tables)

    out_shape = [jax.ShapeDtypeStruct((t, nq), q_dtype), jax.ShapeDtypeStruct((t, nk), BF16),
                 jax.ShapeDtypeStruct((batch, seq // kv_block, nv_out, kv_block), BF16)]
    out_specs = [pl.BlockSpec((rows, nq), row), pl.BlockSpec((rows, nk), row),
                 pl.BlockSpec((1, chunks, nv_out, kv_block),
                              lambda i: (i // tiles_per_seq, i % tiles_per_seq, 0, 0))]
    if with_kmean:
        out_shape.append(jax.ShapeDtypeStruct((t // rows, rows // MOBA_BLOCK, nk), F32))
        out_specs.append(pl.BlockSpec((1, rows // MOBA_BLOCK, nk), lambda i: (i, 0, 0)))

    return pl.pallas_call(
        functools.partial(_proj_kernel, nq=nq, nk=nk, kv_block=kv_block, q_scale=q_scale, v_group=v_group,
                          has_bias=bias is not None, with_kmean=with_kmean),
        grid=(t // rows,),
        in_specs=in_specs, out_specs=out_specs, out_shape=out_shape,
        compiler_params=pltpu.CompilerParams(dimension_semantics=("arbitrary",), vmem_limit_bytes=VMEM_LIMIT),
    )(*args)


def _prepare_queries(q_ref, qx_ref, extra_lanes=None):
    for hh in range(HEADS_PER_STEP):
        halves = _lo_hi(q_ref[:, hh * LANES:(hh + 1) * LANES].astype(BF16))
        for c in range(2):
            qx_ref[2 * hh + c, :, :LANES] = halves[c]
            if extra_lanes is not None:
                qx_ref[2 * hh + c, :, LANES:] = extra_lanes(2 * hh + c)


def _scores_stage(qx_ref, k_ref, s_ref, tmax_ref, tile, keep_fn, k_extra):
    row0 = pl.multiple_of(tile * ATTN_BLOCK, ATTN_BLOCK)
    extra = k_extra()
    for hh in range(HEADS_PER_STEP):
        k = k_ref[pl.ds(row0, ATTN_BLOCK), hh * LANES:(hh + 1) * LANES]
        if extra is not None:
            k = jnp.concatenate([k, extra], axis=1)
        for mi in (2 * hh, 2 * hh + 1):
            s = _nt(k, qx_ref[mi])
            keep = keep_fn(mi)
            if keep is not None:
                s = jnp.where(keep, s, NEG_INF)
            s_ref[mi] = s
            tmax_ref[mi] = jnp.max(s, axis=0, keepdims=True)


def _softmax_stage(s_ref, tmax_ref, m_ref, alpha_ref, p_ref):
    for mi in range(2 * HEADS_PER_STEP):
        m_old = m_ref[mi]
        m_new = jnp.maximum(m_old, tmax_ref[mi])
        alpha_ref[mi] = jnp.exp2(m_old - m_new)
        m_ref[mi] = m_new
        p_ref[mi] = jnp.exp2(s_ref[mi] - m_new).astype(BF16)


def _values_stage(vt_of, tile, alpha_ref, p_ref, acc_ref):
    for mi in range(2 * HEADS_PER_STEP):
        vt = jnp.concatenate([vt_of(mi, tile * KEY_CHUNKS + ch) for ch in range(KEY_CHUNKS)], axis=1)
        acc_ref[mi] = alpha_ref[mi] * acc_ref[mi] + _nn(vt, p_ref[mi])


def _no_extra(*_):
    return None


def _flash_pipeline(i, qx_ref, k_ref, vt_of, s_ref, tmax_ref, m_ref, alpha_ref, p_ref, acc_ref, diag_keep,
                    past_keep=_no_extra, diag_extra=_no_extra, past_extra=_no_extra):
    m_ref[...] = jnp.full(m_ref.shape, M_INIT, F32)
    alpha_ref[...] = jnp.ones(alpha_ref.shape, F32)

    @pl.when((pl.program_id(0) == 0) & (pl.program_id(1) == 0) & (i == 0))
    def _init():
        p_ref[...] = jnp.zeros(p_ref.shape, BF16)
        acc_ref[...] = jnp.zeros(acc_ref.shape, F32)

    def tile_at(step):
        return jnp.where(step == 0, i, step - 1)

    def values(step):
        _values_stage(vt_of, tile_at(step), alpha_ref, p_ref, acc_ref)

    def softmax():
        _softmax_stage(s_ref, tmax_ref, m_ref, alpha_ref, p_ref)

    def scores(tile, keep_fn, k_extra):
        _scores_stage(qx_ref, k_ref, s_ref, tmax_ref, tile, keep_fn, k_extra)

    scores(i, diag_keep, diag_extra)

    def body(t, carry):
        values(jnp.maximum(t - 1, 0))
        softmax()
        scores(t, functools.partial(past_keep, t), functools.partial(past_extra, t))
        return carry

    lax.fori_loop(0, i, body, 0)
    values(jnp.maximum(i - 1, 0))
    softmax()
    values(i)


def _tile_iotas():
    shape = (ATTN_BLOCK, ATTN_BLOCK)
    return lax.broadcasted_iota(jnp.int32, shape, 0), lax.broadcasted_iota(jnp.int32, shape, 1)


def _flash_scratch(acc_rows, query_lanes=LANES):
    maps = 2 * HEADS_PER_STEP
    blk = ATTN_BLOCK
    stat = pltpu.VMEM((maps, 1, blk), F32)
    return [pltpu.VMEM((maps, blk, query_lanes), BF16), pltpu.VMEM((maps, blk, blk), F32), stat, stat, stat,
            pltpu.VMEM((maps, blk, blk), BF16), pltpu.VMEM((maps, acc_rows, blk), F32)]


def _flash_specs(seq, nqb, vt_rows):
    width = LANES * HEADS_PER_STEP
    q_spec = pl.BlockSpec((ATTN_BLOCK, width), lambda b, h, i: (b * nqb + i, h))
    once = pl.Buffered(1)
    k_spec = pl.BlockSpec((seq, width), lambda b, h, i: (b, h), pipeline_mode=once)
    vt_spec = pl.BlockSpec((1, seq // KEY_CHUNK, vt_rows * HEADS_PER_STEP, KEY_CHUNK), lambda b, h, i: (b, 0, h, 0),
                           pipeline_mode=once)
    return q_spec, k_spec, vt_spec


def _diff_attn_kernel(lq1_ref, lk1_ref, lq2_ref, lk2_ref, subg_ref, q_ref, k_ref, vt_ref, o_ref,
                      qx_ref, s_ref, tmax_ref, m_ref, alpha_ref, p_ref, acc_ref, *, lambda_init):
    i = pl.program_id(2)
    rows = LANES + BF16_ROWS

    def vt_of(mi, chunk):
        return vt_ref[0, chunk, (mi // 2) * rows:(mi // 2 + 1) * rows]

    def diag_keep(mi):
        key, qry = _tile_iotas()
        return key <= qry

    _prepare_queries(q_ref, qx_ref)
    _flash_pipeline(i, qx_ref, k_ref, vt_of, s_ref, tmax_ref, m_ref, alpha_ref, p_ref, acc_ref, diag_keep)

    lam = (jnp.exp(jnp.sum(lq1_ref[...] * lk1_ref[...], axis=1, keepdims=True))
           - jnp.exp(jnp.sum(lq2_ref[...] * lk2_ref[...], axis=1, keepdims=True)) + lambda_init)
    for hh in range(HEADS_PER_STEP):
        a1, a2 = acc_ref[2 * hh], acc_ref[2 * hh + 1]
        o = a1[:LANES] / a1[LANES:LANES + 1] - lam * (a2[:LANES] / a2[LANES:LANES + 1])
        o = o * lax.rsqrt(jnp.mean(o * o, axis=0, keepdims=True) + NORM_EPS) * subg_ref[...]
        o_ref[:, hh * LANES:(hh + 1) * LANES] = (o * (1.0 - lambda_init)).T.astype(o_ref.dtype)


def _diff_attention(q, k, vt, lam_params, sub_g, *, batch, lambda_init):
    t = q.shape[0]
    seq = t // batch
    nqb = seq // ATTN_BLOCK
    rows = LANES + BF16_ROWS
    small = pl.BlockSpec((1, HEAD_DIM), lambda b, h, i: (0, 0))
    q_spec, k_spec, vt_spec = _flash_specs(seq, nqb, rows)
    return pl.pallas_call(
        functools.partial(_diff_attn_kernel, lambda_init=lambda_init),
        grid=(batch, PAIRS // HEADS_PER_STEP, nqb),
        in_specs=[small, small, small, small, pl.BlockSpec((LANES, 1), lambda b, h, i: (0, 0)),
                  q_spec, k_spec, vt_spec],
        out_specs=q_spec,
        out_shape=jax.ShapeDtypeStruct((t, D_MODEL), BF16),
        scratch_shapes=_flash_scratch(rows),
        compiler_params=pltpu.CompilerParams(dimension_semantics=("arbitrary",) * 3, vmem_limit_bytes=VMEM_LIMIT),
    )(*[p.reshape(1, HEAD_DIM) for p in lam_params], sub_g.reshape(LANES, 1), q, k, vt)


def _moba_select(gate, own):
    nb, width = gate.shape
    groups = range(nb // SUBLANES)
    sub = lax.broadcasted_iota(jnp.int32, (SUBLANES, width), 0)
    ids = [sub + SUBLANES * r for r in groups]
    g = [jnp.where(ids[r] < own, gate[SUBLANES * r:SUBLANES * (r + 1)], NEG_INF) for r in groups]
    sel = [jnp.zeros((SUBLANES, width), F32) for _ in groups]

    def over_blocks(parts, op):
        x = functools.reduce(op, parts)
        for shift in (4, 2, 1):
            x = op(x, pltpu.roll(x, shift, 0))
        return x

    for t in range(MOBA_TOPK):
        best = over_blocks(g, jnp.maximum)
        first = over_blocks([jnp.where(g[r] == best, ids[r], nb) for r in groups], jnp.minimum)
        valid = jnp.where(t < own, 1.0, 0.0)
        for r in groups:
            hit = ids[r] == first
            sel[r] = jnp.maximum(sel[r], jnp.where(hit, valid, 0.0))
            g[r] = jnp.where(hit, NEG_INF, g[r])
    return jnp.concatenate(sel, axis=0)


def _moba_attn_kernel(q_ref, k_ref, vt_ref, km_ref, o_ref,
                      sel_ref, qx_ref, s_ref, tmax_ref, m_ref, alpha_ref, p_ref, acc_ref):
    i = pl.program_id(2)
    half = HEAD_DIM + BF16_ROWS
    maps = 2 * HEADS_PER_STEP
    nb = sel_ref.shape[1]
    gates = []
    for hh in range(HEADS_PER_STEP):
        lanes = slice(hh * LANES, (hh + 1) * LANES)
        qf = q_ref[:, lanes]
        gates += [_nt(km_half, qf, lax.Precision.HIGHEST) for km_half in _lo_hi(km_ref[0, :, lanes])]
    own = i * KEY_CHUNKS + (lax.broadcasted_iota(jnp.int32, (1, maps * ATTN_BLOCK), 1) % ATTN_BLOCK) // MOBA_BLOCK
    sel = _moba_select(jnp.concatenate(gates, axis=1), own)
    for mi in range(maps):
        sel_ref[mi] = sel[:, mi * ATTN_BLOCK:(mi + 1) * ATTN_BLOCK]

    def vt_of(mi, chunk):
        return vt_ref[0, chunk, mi * half:(mi + 1) * half]

    def selected(block, mi):
        return sel_ref[mi, pl.ds(block, 1), :] > 0.0

    def diag_keep(mi):
        key, qry = _tile_iotas()
        key_blk, qry_blk = key // MOBA_BLOCK, qry // MOBA_BLOCK
        keep = (key <= qry) & (key_blk == qry_blk)
        for ch in range(KEY_CHUNKS - 1):
            keep = keep | ((key_blk == ch) & (qry_blk > ch) & selected(i * KEY_CHUNKS + ch, mi))
        return keep

    def bias_lanes(mi):
        rows = jnp.where(sel_ref[mi] > 0.0, 0.0, MASK_BIAS)
        rows = jnp.concatenate([rows, jnp.zeros((LANES - nb, ATTN_BLOCK), F32)], axis=0)
        return rows.T.astype(BF16)

    def block_marker(t):
        lane = lax.broadcasted_iota(jnp.int32, (MOBA_BLOCK, LANES), 1)
        return jnp.concatenate([jnp.where(lane == t * KEY_CHUNKS + ch, 1.0, 0.0).astype(BF16)
                                for ch in range(KEY_CHUNKS)], axis=0)

    def no_marker():
        return jnp.zeros((ATTN_BLOCK, LANES), BF16)

    _prepare_queries(q_ref, qx_ref, bias_lanes)
    _flash_pipeline(i, qx_ref, k_ref, vt_of, s_ref, tmax_ref, m_ref, alpha_ref, p_ref, acc_ref, diag_keep,
                    diag_extra=no_marker, past_extra=block_marker)

    for hh in range(HEADS_PER_STEP):
        a1, a2 = acc_ref[2 * hh], acc_ref[2 * hh + 1]
        o = jnp.concatenate([a1[:HEAD_DIM] / a1[HEAD_DIM:HEAD_DIM + 1], a2[:HEAD_DIM] / a2[HEAD_DIM:HEAD_DIM + 1]],
                            axis=0)
        o_ref[:, hh * LANES:(hh + 1) * LANES] = o.T.astype(o_ref.dtype)


def _moba_attention(q, k, vt, kmean, *, batch):
    t = q.shape[0]
    seq = t // batch
    nqb = seq // ATTN_BLOCK
    nb = seq // MOBA_BLOCK
    half = HEAD_DIM + BF16_ROWS
    q_spec, k_spec, vt_spec = _flash_specs(seq, nqb, 2 * half)
    return pl.pallas_call(
        _moba_attn_kernel,
        grid=(batch, PAIRS // HEADS_PER_STEP, nqb),
        in_specs=[q_spec, k_spec, vt_spec,
                  pl.BlockSpec((1, nb, LANES * HEADS_PER_STEP), lambda b, h, i: (b, 0, h))],
        out_specs=q_spec,
        out_shape=jax.ShapeDtypeStruct((t, D_MODEL), BF16),
        scratch_shapes=[pltpu.VMEM((2 * HEADS_PER_STEP, nb, ATTN_BLOCK), F32)] + _flash_scratch(half, 2 * LANES),
        compiler_params=pltpu.CompilerParams(dimension_semantics=("arbitrary",) * 3, vmem_limit_bytes=VMEM_LIMIT),
    )(q, k, vt, kmean)


def _swa_attn_kernel(sink_ref, q_ref, k_ref, vt_ref, o_ref):
    h = pl.program_id(1)
    i = pl.program_id(2)
    sub, w = SWA_SUB_BLOCK, SWA_WINDOW
    subs = SWA_Q_BLOCK // sub
    key = lax.broadcasted_iota(jnp.int32, (sub, sub), 0)
    qry = lax.broadcasted_iota(jnp.int32, (sub, sub), 1)
    own_keep = (key <= qry) & (qry - key < w)
    pkey = lax.broadcasted_iota(jnp.int32, (w, sub), 0)
    pqry = lax.broadcasted_iota(jnp.int32, (w, sub), 1)
    rows = HEAD_DIM + BF16_ROWS
    row0 = pl.multiple_of((h // (PAIRS // SWA_KV_HEADS)) * rows, BF16_ROWS)

    sinks = [sink_ref[2 * h + half] * LOG2_E for half in range(2)]
    chains = []
    for sb in range(subs):
        blk = i * subs + sb
        q = q_ref[sb * sub:(sb + 1) * sub, :]
        q0 = pl.multiple_of(blk * sub, sub)
        prev0 = pl.multiple_of(jnp.maximum(q0 - w, 0), w)
        k_own = _lo_hi(k_ref[pl.ds(q0, sub), :])
        k_prev = _lo_hi(k_ref[pl.ds(prev0, w), :])
        prev_keep = pkey > pqry
        if sb == 0:
            prev_keep = pkey > pqry + jnp.where(i > 0, 0, w)
        for half in range(2):
            s_own = jnp.where(own_keep, _nt(k_own[half], q), NEG_INF)
            s_prev = jnp.where(prev_keep, _nt(k_prev[half], q), NEG_INF)
            m = jnp.maximum(jnp.maximum(jnp.max(s_own, axis=0, keepdims=True),
                                        jnp.max(s_prev, axis=0, keepdims=True)), sinks[half])
            chains.append((s_own, s_prev, m))

    probs = [(jnp.exp2(s_own - m).astype(BF16), jnp.exp2(s_prev - m).astype(BF16)) for s_own, s_prev, m in chains]

    for sb in range(subs):
        chunk0 = (i * subs + sb) * (sub // w)
        vt_prev = vt_ref[0, jnp.maximum(chunk0 - 1, 0), pl.ds(row0, rows), :]
        vt_own = jnp.concatenate([vt_ref[0, chunk0 + c, pl.ds(row0, rows), :] for c in range(sub // w)], axis=1)
        outs = []
        for half in range(2):
            p_own, p_prev = probs[2 * sb + half]
            acc = _nn(vt_own, p_own) + _nn(vt_prev, p_prev)
            denom = acc[HEAD_DIM:HEAD_DIM + 1] + jnp.exp2(sinks[half] - chains[2 * sb + half][2])
            outs.append(acc[:HEAD_DIM] / denom)
        o_ref[sb * sub:(sb + 1) * sub, :] = jnp.concatenate(outs, axis=0).T.astype(o_ref.dtype)


def _swa_attention(q, kdup, vt, sinks, *, batch):
    t = q.shape[0]
    seq = t // batch
    qb = SWA_Q_BLOCK
    nqb = seq // qb
    per_kv = PAIRS // SWA_KV_HEADS
    vt_rows = SWA_KV_HEADS * (HEAD_DIM + BF16_ROWS)
    return pl.pallas_call(
        _swa_attn_kernel,
        grid=(batch, PAIRS, nqb),
        in_specs=[pl.BlockSpec(memory_space=pltpu.SMEM),
                  pl.BlockSpec((qb, LANES), lambda b, h, i: (b * nqb + i, h)),
                  pl.BlockSpec((seq, LANES), lambda b, h, i: (b, h // per_kv)),
                  pl.BlockSpec((1, seq // SWA_WINDOW, vt_rows, SWA_WINDOW), lambda b, h, i: (b, 0, 0, 0))],
        out_specs=pl.BlockSpec((qb, LANES), lambda b, h, i: (b * nqb + i, h)),
        out_shape=jax.ShapeDtypeStruct((t, D_MODEL), BF16),
        compiler_params=pltpu.CompilerParams(dimension_semantics=("arbitrary",) * 3, vmem_limit_bytes=VMEM_LIMIT),
    )(sinks, q, kdup, vt)


def _post_kernel(*refs, final):
    h_ref, o_ref, wout_ref, g_ref, wup_ref, wdn_ref = refs[:6]
    fg_ref = refs[6] if final else None
    out_ref = refs[-1]
    h1 = h_ref[...] + _nn(o_ref[...], wout_ref[...])
    xn = _rms(h1, g_ref[...]).astype(BF16)
    acc = h1
    for c in range(D_FF // FF_CHUNK):
        lo = c * FF_CHUNK
        a = jnp.maximum(_nn(xn, wup_ref[:, lo:lo + FF_CHUNK]), 0.0)
        acc = acc + _nn((a * a).astype(BF16), wdn_ref[lo:lo + FF_CHUNK, :])
    if final:
        acc = _rms(acc, fg_ref[...])
    out_ref[...] = acc


def _post(h, o, w_out, g, w_up, w_down, final_g=None):
    t = h.shape[0]
    rows = PROJ_ROWS
    const = lambda i: (0, 0)
    row = lambda i: (i, 0)
    in_specs = [pl.BlockSpec((rows, D_MODEL), row), pl.BlockSpec((rows, D_MODEL), row),
                pl.BlockSpec(w_out.shape, const), pl.BlockSpec((1, D_MODEL), const),
                pl.BlockSpec(w_up.shape, const), pl.BlockSpec(w_down.shape, const)]
    args = [h, o, w_out, g.reshape(1, D_MODEL), w_up, w_down]
    if final_g is not None:
        in_specs.append(pl.BlockSpec((1, D_MODEL), const))
        args.append(final_g.reshape(1, D_MODEL))
    return pl.pallas_call(
        functools.partial(_post_kernel, final=final_g is not None),
        grid=(t // rows,),
        in_specs=in_specs,
        out_specs=pl.BlockSpec((rows, D_MODEL), row),
        out_shape=jax.ShapeDtypeStruct((t, D_MODEL), F32),
        compiler_params=pltpu.CompilerParams(dimension_semantics=("arbitrary",), vmem_limit_bytes=VMEM_LIMIT),
    )(*args)


def kernel(x, positions, attn_norm_g, mlp_norm_g, diff_w_in, diff_w_out, diff_lam_q1, diff_lam_k1, diff_lam_q2,
           diff_lam_k2, diff_subln_g, moba_w_in, moba_w_out, swa_w_in, swa_b_in, swa_sinks, swa_w_out, mlp_w_up,
           mlp_w_down, final_norm_g):
    batch, seq, _ = x.shape
    depth = attn_norm_g.shape[0]
    tables = _rope_tables(positions)
    h = x.reshape(batch * seq, D_MODEL)
    qk = 2 * D_MODEL
    log2_scale = Q_SCALE * LOG2_E

    for i in range(depth):
        mixer, slot = i % N_MIXERS, i // N_MIXERS
        if mixer == 0:
            w_in = diff_w_in[slot]
            q, k, vt = _project(h, attn_norm_g[i], w_in[:, :qk].astype(BF16), w_in[:, qk:].T.astype(BF16), tables,
                                batch=batch, nq=D_MODEL, kv_block=KEY_CHUNK, q_scale=log2_scale, v_group=LANES)
            o = _diff_attention(q, k, vt,
                                (diff_lam_q1[slot], diff_lam_k1[slot], diff_lam_q2[slot], diff_lam_k2[slot]),
                                diff_subln_g[slot], batch=batch, lambda_init=0.8 - 0.6 * math.exp(-0.3 * i))
            w_out = diff_w_out[slot]
        elif mixer == 1:
            w_in = moba_w_in[slot]
            q, k, vt, kmean = _project(h, attn_norm_g[i], w_in[:, :qk].astype(BF16), w_in[:, qk:].T.astype(BF16),
                                       tables, batch=batch, nq=D_MODEL, kv_block=KEY_CHUNK, q_scale=log2_scale,
                                       v_group=HEAD_DIM, q_dtype=F32, with_kmean=True)
            o = _moba_attention(q, k, vt, kmean.reshape(batch, seq // MOBA_BLOCK, D_MODEL), batch=batch)
            w_out = moba_w_out[slot]
        else:
            w_in, b_in = swa_w_in[slot], swa_b_in[slot]
            dup = lambda a: jnp.concatenate(
                [a[..., D_MODEL + kv * HEAD_DIM:D_MODEL + (kv + 1) * HEAD_DIM]
                 for kv in range(SWA_KV_HEADS) for _ in range(2)], axis=-1)
            v0 = D_MODEL + SWA_KV_HEADS * HEAD_DIM
            wqk = jnp.concatenate([w_in[:, :D_MODEL], dup(w_in)], axis=1).astype(BF16)
            bqk = jnp.concatenate([b_in[:D_MODEL], dup(b_in)])
            q, k, vt = _project(h, attn_norm_g[i], wqk, w_in[:, v0:].T.astype(BF16), tables, batch=batch,
                                nq=D_MODEL, kv_block=SWA_WINDOW, q_scale=log2_scale, v_group=HEAD_DIM,
                                bias=(bqk, b_in[v0:]))
            o = _swa_attention(q, k, vt, swa_sinks[slot], batch=batch)
            w_out = swa_w_out[slot]
        h = _post(h, o, w_out.astype(BF16), mlp_norm_g[i], mlp_w_up[i].astype(BF16), mlp_w_down[i].astype(BF16),
                  final_norm_g if i == depth - 1 else None)
    return h.reshape(batch, seq, D_MODEL)
```

```python
import functools
import math

import jax
import jax.numpy as jnp
from jax import lax
from jax.experimental import pallas as pl
from jax.experimental.pallas import tpu as pltpu

F32 = jnp.float32
BF16 = jnp.bfloat16

D_MODEL = 1024
HEAD_DIM = 64
ROT_DIM = HEAD_DIM // 4
ROT_HALF = ROT_DIM // 2
ROPE_THETA = 500000.0
NORM_EPS = 1e-6
D_FF = 4 * D_MODEL
Q_SCALE = HEAD_DIM ** -0.5
LOG2_E = math.log2(math.e)
N_MIXERS = 3

LANES = 128
SUBLANES = 8
BF16_ROWS = 16
MXU_COLS = 256
PAIRS = D_MODEL // LANES
MOBA_BLOCK = 256
MOBA_TOPK = 3
SWA_WINDOW = 128
SWA_KV_HEADS = 2

ATTN_BLOCK = 512
KEY_CHUNK = 256
KEY_CHUNKS = ATTN_BLOCK // KEY_CHUNK
Q_TILES_PER_STEP = 2
HEADS_PER_STEP = 4
SWA_Q_BLOCK = 1024
SWA_SUB_BLOCK = 256
PROJ_ROWS = 512
FF_CHUNK = 1024
VMEM_LIMIT = 56 * 1024 * 1024

NEG_INF = float("-inf")
M_INIT = -1e30
MASK_BIAS = -1e30


def _nt(a, b, precision=None):
    return lax.dot_general(a, b, (((1,), (1,)), ((), ())), preferred_element_type=F32, precision=precision)


def _nn(a, b):
    return jnp.dot(a, b, preferred_element_type=F32)


def _rms(x, g):
    return x * lax.rsqrt(jnp.mean(x * x, axis=-1, keepdims=True) + NORM_EPS) * g


def _lo_hi(x):
    lane = lax.broadcasted_iota(jnp.int32, (1, LANES), 1)
    zero = jnp.zeros_like(x)
    return jnp.where(lane < HEAD_DIM, x, zero), jnp.where(lane < HEAD_DIM, zero, x)


def _rope_table_kernel(pos_ref, inv_ref, c_ref, sa_ref, sb_ref):
    ang = pos_ref[...].astype(F32) * inv_ref[...]
    lane = lax.broadcasted_iota(jnp.int32, ang.shape, 1) % HEAD_DIM
    cos = jnp.cos(ang)
    sin = jnp.sin(ang)
    c_ref[...] = jnp.where(lane < ROT_DIM, cos, 1.0)
    sa_ref[...] = jnp.where(lane < ROT_HALF, -sin, 0.0)
    sb_ref[...] = jnp.where((lane >= ROT_HALF) & (lane < ROT_DIM), sin, 0.0)


def _rope_tables(positions):
    t = positions.size
    rows = 1024
    inv = ROPE_THETA ** (-jnp.arange(0, ROT_DIM, 2, dtype=F32) / ROT_DIM)
    lane = jnp.arange(LANES) % HEAD_DIM
    inv_lane = jnp.where(lane < ROT_DIM, inv[lane % ROT_HALF], 0.0).reshape(1, LANES)
    tab = jax.ShapeDtypeStruct((t, LANES), F32)
    spec = pl.BlockSpec((rows, LANES), lambda i: (i, 0))
    return pl.pallas_call(
        _rope_table_kernel,
        grid=(t // rows,),
        in_specs=[pl.BlockSpec((rows, 1), lambda i: (i, 0)), pl.BlockSpec((1, LANES), lambda i: (0, 0))],
        out_specs=[spec, spec, spec],
        out_shape=[tab, tab, tab],
    )(positions.reshape(t, 1), inv_lane)


def _proj_kernel(*refs, nq, nk, kv_block, q_scale, v_group, has_bias, with_kmean):
    it = iter(refs)
    h_ref, g_ref, wqk_ref, wvt_ref = next(it), next(it), next(it), next(it)
    bqk_ref = next(it) if has_bias else None
    bv_ref = next(it) if has_bias else None
    c_ref, sa_ref, sb_ref = next(it), next(it), next(it)
    q_ref, k_ref, vt_ref = next(it), next(it), next(it)
    km_ref = next(it) if with_kmean else None

    rows = h_ref.shape[0]
    xn = _rms(h_ref[...], g_ref[...]).astype(BF16)
    cos, sin_a, sin_b = c_ref[...], sa_ref[...], sb_ref[...]

    for c in range((nq + nk) // LANES):
        col = c * LANES
        if c % 2 == 0:
            wide = _nn(xn, wqk_ref[:, col:col + MXU_COLS])
            if has_bias:
                wide = wide + bqk_ref[:, col:col + MXU_COLS]
        y = wide[:, (c % 2) * LANES:(c % 2 + 1) * LANES]
        y = y * cos + pltpu.roll(y, LANES - ROT_HALF, 1) * sin_a + pltpu.roll(y, ROT_HALF, 1) * sin_b
        if col < nq:
            q_ref[:, col:col + LANES] = (y * q_scale).astype(q_ref.dtype)
        else:
            kc = col - nq
            k_ref[:, kc:kc + LANES] = y.astype(k_ref.dtype)
            if with_kmean:
                for r in range(rows // MOBA_BLOCK):
                    blk = y[r * MOBA_BLOCK:(r + 1) * MOBA_BLOCK]
                    km_ref[0, r:r + 1, kc:kc + LANES] = jnp.sum(blk, axis=0, keepdims=True) * (1.0 / MOBA_BLOCK)

    vt = _nt(wvt_ref[...], xn)
    if has_bias:
        vt = vt + bv_ref[...]
    nv = vt.shape[0]
    for r in range(rows // kv_block):
        blk = vt[:, r * kv_block:(r + 1) * kv_block].astype(vt_ref.dtype)
        if v_group is None:
            vt_ref[0, r] = blk
        else:
            stride = v_group + BF16_ROWS
            for g in range(nv // v_group):
                vt_ref[0, r, g * stride:g * stride + v_group] = blk[g * v_group:(g + 1) * v_group]
                vt_ref[0, r, g * stride + v_group:(g + 1) * stride] = jnp.ones((BF16_ROWS, kv_block), vt_ref.dtype)


def _project(h, g, wqk, wvt, tables, *, batch, nq, kv_block, q_scale, v_group=None, q_dtype=BF16, bias=None,
             with_kmean=False):
    t = h.shape[0]
    rows = PROJ_ROWS
    nk = wqk.shape[1] - nq
    nv = wvt.shape[0]
    nv_out = nv if v_group is None else nv // v_group * (v_group + BF16_ROWS)
    seq = t // batch
    tiles_per_seq = seq // rows
    chunks = rows // kv_block

    const = lambda i: (0, 0)
    row = lambda i: (i, 0)
    in_specs = [pl.BlockSpec((rows, D_MODEL), row), pl.BlockSpec((1, D_MODEL), const),
                pl.BlockSpec(wqk.shape, const), pl.BlockSpec(wvt.shape, const)]
    args = [h, g.reshape(1, D_MODEL), wqk, wvt]
    if bias is not None:
        in_specs += [pl.BlockSpec((1, nq + nk), const), pl.BlockSpec((nv, 1), const)]
        args += [bias[0].reshape(1, nq + nk), bias[1].reshape(nv, 1)]
    in_specs += [pl.BlockSpec((rows, LANES), row)] * 3
    args += list(tables)

    out_shape = [jax.ShapeDtypeStruct((t, nq), q_dtype), jax.ShapeDtypeStruct((t, nk), BF16),
                 jax.ShapeDtypeStruct((batch, seq // kv_block, nv_out, kv_block), BF16)]
    out_specs = [pl.BlockSpec((rows, nq), row), pl.BlockSpec((rows, nk), row),
                 pl.BlockSpec((1, chunks, nv_out, kv_block),
                              lambda i: (i // tiles_per_seq, i % tiles_per_seq, 0, 0))]
    if with_kmean:
        out_shape.append(jax.ShapeDtypeStruct((t // rows, rows // MOBA_BLOCK, nk), F32))
        out_specs.append(pl.BlockSpec((1, rows // MOBA_BLOCK, nk), lambda i: (i, 0, 0)))

    return pl.pallas_call(
        functools.partial(_proj_kernel, nq=nq, nk=nk, kv_block=kv_block, q_scale=q_scale, v_group=v_group,
                          has_bias=bias is not None, with_kmean=with_kmean),
        grid=(t // rows,),
        in_specs=in_specs, out_specs=out_specs, out_shape=out_shape,
        compiler_params=pltpu.CompilerParams(dimension_semantics=("arbitrary",), vmem_limit_bytes=VMEM_LIMIT),
    )(*args)


def _prepare_queries(q_ref, qx_ref, extra_lanes=None):
    for hh in range(HEADS_PER_STEP):
        halves = _lo_hi(q_ref[:, hh * LANES:(hh + 1) * LANES].astype(BF16))
        for c in range(2):
            qx_ref[2 * hh + c, :, :LANES] = halves[c]
            if extra_lanes is not None:
                qx_ref[2 * hh + c, :, LANES:] = extra_lanes(2 * hh + c)


def _scores_stage(qx_ref, k_ref, s_ref, tmax_ref, tile, keep_fn, k_extra, diagonal=False):
    row0 = pl.multiple_of(tile * ATTN_BLOCK, ATTN_BLOCK)
    extra = k_extra()
    for hh in range(HEADS_PER_STEP):
        k = k_ref[pl.ds(row0, ATTN_BLOCK), hh * LANES:(hh + 1) * LANES]
        if extra is not None:
            k = jnp.concatenate([k, extra], axis=1)
        for mi in (2 * hh, 2 * hh + 1):
            if diagonal:
                s = jnp.concatenate([
                    jnp.concatenate(
                        [jnp.full((KEY_CHUNK, c * KEY_CHUNK), NEG_INF, F32)] * (c > 0)
                        + [_nt(k[c * KEY_CHUNK:(c + 1) * KEY_CHUNK], qx_ref[mi, c * KEY_CHUNK:, :])], axis=1)
                    for c in range(KEY_CHUNKS)], axis=0)
            else:
                s = _nt(k, qx_ref[mi])
            keep = keep_fn(mi)
            if keep is not None:
                s = jnp.where(keep, s, NEG_INF)
            s_ref[mi] = s
            tmax_ref[mi] = jnp.max(s, axis=0, keepdims=True)


def _softmax_stage(s_ref, tmax_ref, m_ref, alpha_ref, p_ref):
    for mi in range(2 * HEADS_PER_STEP):
        m_old = m_ref[mi]
        m_new = jnp.maximum(m_old, tmax_ref[mi])
        alpha_ref[mi] = jnp.exp2(m_old - m_new)
        m_ref[mi] = m_new
        p_ref[mi] = jnp.exp2(s_ref[mi] - m_new).astype(BF16)


def _values_stage(vt_of, tile, alpha_ref, p_ref, acc_ref):
    for mi in range(2 * HEADS_PER_STEP):
        vt = jnp.concatenate([vt_of(mi, tile * KEY_CHUNKS + ch) for ch in range(KEY_CHUNKS)], axis=1)
        acc_ref[mi] = alpha_ref[mi] * acc_ref[mi] + _nn(vt, p_ref[mi])


def _no_extra(*_):
    return None


def _query_tile(sub):
    return pl.program_id(2) * Q_TILES_PER_STEP + sub, slice(sub * ATTN_BLOCK, (sub + 1) * ATTN_BLOCK)


def _flash_pipeline(i, qx_ref, k_ref, vt_of, s_ref, tmax_ref, m_ref, alpha_ref, p_ref, acc_ref, diag_keep,
                    past_keep=_no_extra, diag_extra=_no_extra, past_extra=_no_extra, first_in_step=True):
    m_ref[...] = jnp.full(m_ref.shape, M_INIT, F32)
    alpha_ref[...] = jnp.ones(alpha_ref.shape, F32)

    if first_in_step:
        @pl.when((pl.program_id(0) == 0) & (pl.program_id(1) == 0) & (pl.program_id(2) == 0))
        def _init():
            p_ref[...] = jnp.zeros(p_ref.shape, BF16)
            acc_ref[...] = jnp.zeros(acc_ref.shape, F32)

    def tile_at(step):
        return jnp.where(step == 0, i, step - 1)

    def values(step):
        _values_stage(vt_of, tile_at(step), alpha_ref, p_ref, acc_ref)

    def softmax():
        _softmax_stage(s_ref, tmax_ref, m_ref, alpha_ref, p_ref)

    def scores(tile, keep_fn, k_extra, diagonal=False):
        _scores_stage(qx_ref, k_ref, s_ref, tmax_ref, tile, keep_fn, k_extra, diagonal)

    scores(i, diag_keep, diag_extra, diagonal=True)

    def body(t, carry):
        values(jnp.maximum(t - 1, 0))
        softmax()
        scores(t, functools.partial(past_keep, t), functools.partial(past_extra, t))
        return carry

    lax.fori_loop(0, i, body, 0)
    values(jnp.maximum(i - 1, 0))
    softmax()
    values(i)


def _tile_iotas():
    shape = (ATTN_BLOCK, ATTN_BLOCK)
    return lax.broadcasted_iota(jnp.int32, shape, 0), lax.broadcasted_iota(jnp.int32, shape, 1)


def _flash_scratch(acc_rows, query_lanes=LANES):
    maps = 2 * HEADS_PER_STEP
    blk = ATTN_BLOCK
    stat = pltpu.VMEM((maps, 1, blk), F32)
    return [pltpu.VMEM((maps, blk, query_lanes), BF16), pltpu.VMEM((maps, blk, blk), F32), stat, stat, stat,
            pltpu.VMEM((maps, blk, blk), BF16), pltpu.VMEM((maps, acc_rows, blk), F32)]


def _flash_specs(seq, nqb, vt_rows):
    width = LANES * HEADS_PER_STEP
    q_spec = pl.BlockSpec((Q_TILES_PER_STEP * ATTN_BLOCK, width), lambda b, h, i: (b * nqb + i, h))
    once = pl.Buffered(1)
    k_spec = pl.BlockSpec((seq, width), lambda b, h, i: (b, h), pipeline_mode=once)
    vt_spec = pl.BlockSpec((1, seq // KEY_CHUNK, vt_rows * HEADS_PER_STEP, KEY_CHUNK), lambda b, h, i: (b, 0, h, 0),
                           pipeline_mode=once)
    return q_spec, k_spec, vt_spec


def _diff_attn_kernel(lq1_ref, lk1_ref, lq2_ref, lk2_ref, subg_ref, q_ref, k_ref, vt_ref, o_ref,
                      qx_ref, s_ref, tmax_ref, m_ref, alpha_ref, p_ref, acc_ref, *, lambda_init):
    rows = LANES + BF16_ROWS

    def vt_of(mi, chunk):
        return vt_ref[0, chunk, (mi // 2) * rows:(mi // 2 + 1) * rows]

    def diag_keep(mi):
        key, qry = _tile_iotas()
        return key <= qry

    lam = (jnp.exp(jnp.sum(lq1_ref[...] * lk1_ref[...], axis=1, keepdims=True))
           - jnp.exp(jnp.sum(lq2_ref[...] * lk2_ref[...], axis=1, keepdims=True)) + lambda_init)
    for sub in range(Q_TILES_PER_STEP):
        i, q_rows = _query_tile(sub)
        _prepare_queries(q_ref.at[q_rows], qx_ref)
        _flash_pipeline(i, qx_ref, k_ref, vt_of, s_ref, tmax_ref, m_ref, alpha_ref, p_ref, acc_ref, diag_keep,
                        first_in_step=sub == 0)
        for hh in range(HEADS_PER_STEP):
            a1, a2 = acc_ref[2 * hh], acc_ref[2 * hh + 1]
            o = a1[:LANES] / a1[LANES:LANES + 1] - lam * (a2[:LANES] / a2[LANES:LANES + 1])
            o = o * lax.rsqrt(jnp.mean(o * o, axis=0, keepdims=True) + NORM_EPS) * subg_ref[...]
            o_ref[q_rows, hh * LANES:(hh + 1) * LANES] = (o * (1.0 - lambda_init)).T.astype(o_ref.dtype)


def _diff_attention(q, k, vt, lam_params, sub_g, *, batch, lambda_init):
    t = q.shape[0]
    seq = t // batch
    nqb = seq // (Q_TILES_PER_STEP * ATTN_BLOCK)
    rows = LANES + BF16_ROWS
    small = pl.BlockSpec((1, HEAD_DIM), lambda b, h, i: (0, 0))
    q_spec, k_spec, vt_spec = _flash_specs(seq, nqb, rows)
    return pl.pallas_call(
        functools.partial(_diff_attn_kernel, lambda_init=lambda_init),
        grid=(batch, PAIRS // HEADS_PER_STEP, nqb),
        in_specs=[small, small, small, small, pl.BlockSpec((LANES, 1), lambda b, h, i: (0, 0)),
                  q_spec, k_spec, vt_spec],
        out_specs=q_spec,
        out_shape=jax.ShapeDtypeStruct((t, D_MODEL), BF16),
        scratch_shapes=_flash_scratch(rows),
        compiler_params=pltpu.CompilerParams(dimension_semantics=("arbitrary",) * 3, vmem_limit_bytes=VMEM_LIMIT),
    )(*[p.reshape(1, HEAD_DIM) for p in lam_params], sub_g.reshape(LANES, 1), q, k, vt)


def _moba_select(gate, own):
    nb, width = gate.shape
    groups = range(nb // SUBLANES)
    sub = lax.broadcasted_iota(jnp.int32, (SUBLANES, width), 0)
    ids = [sub + SUBLANES * r for r in groups]
    g = [jnp.where(ids[r] < own, gate[SUBLANES * r:SUBLANES * (r + 1)], NEG_INF) for r in groups]
    sel = [jnp.zeros((SUBLANES, width), F32) for _ in groups]

    def over_blocks(parts, op):
        x = functools.reduce(op, parts)
        for shift in (4, 2, 1):
            x = op(x, pltpu.roll(x, shift, 0))
        return x

    for t in range(MOBA_TOPK):
        best = over_blocks(g, jnp.maximum)
        first = over_blocks([jnp.where(g[r] == best, ids[r], nb) for r in groups], jnp.minimum)
        valid = jnp.where(t < own, 1.0, 0.0)
        for r in groups:
            hit = ids[r] == first
            sel[r] = jnp.maximum(sel[r], jnp.where(hit, valid, 0.0))
            g[r] = jnp.where(hit, NEG_INF, g[r])
    return jnp.concatenate(sel, axis=0)


def _moba_attn_kernel(q_ref, k_ref, vt_ref, km_ref, o_ref,
                      sel_ref, qx_ref, s_ref, tmax_ref, m_ref, alpha_ref, p_ref, acc_ref):
    for sub in range(Q_TILES_PER_STEP):
        _moba_query_tile(sub, q_ref, k_ref, vt_ref, km_ref, o_ref,
                         sel_ref, qx_ref, s_ref, tmax_ref, m_ref, alpha_ref, p_ref, acc_ref)


def _moba_query_tile(sub, q_ref, k_ref, vt_ref, km_ref, o_ref,
                     sel_ref, qx_ref, s_ref, tmax_ref, m_ref, alpha_ref, p_ref, acc_ref):
    i, q_rows = _query_tile(sub)
    half = HEAD_DIM + BF16_ROWS
    maps = 2 * HEADS_PER_STEP
    nb = sel_ref.shape[1]
    gates = []
    for hh in range(HEADS_PER_STEP):
        lanes = slice(hh * LANES, (hh + 1) * LANES)
        qf = q_ref[q_rows, lanes]
        gates += [_nt(km_half, qf, lax.Precision.HIGHEST) for km_half in _lo_hi(km_ref[0, :, lanes])]
    own = i * KEY_CHUNKS + (lax.broadcasted_iota(jnp.int32, (1, maps * ATTN_BLOCK), 1) % ATTN_BLOCK) // MOBA_BLOCK
    sel = _moba_select(jnp.concatenate(gates, axis=1), own)
    for mi in range(maps):
        sel_ref[mi] = sel[:, mi * ATTN_BLOCK:(mi + 1) * ATTN_BLOCK]

    def vt_of(mi, chunk):
        return vt_ref[0, chunk, mi * half:(mi + 1) * half]

    def selected(block, mi):
        return sel_ref[mi, pl.ds(block, 1), :] > 0.0

    def diag_keep(mi):
        key, qry = _tile_iotas()
        key_blk, qry_blk = key // MOBA_BLOCK, qry // MOBA_BLOCK
        keep = (key <= qry) & (key_blk == qry_blk)
        for ch in range(KEY_CHUNKS - 1):
            keep = keep | ((key_blk == ch) & (qry_blk > ch) & selected(i * KEY_CHUNKS + ch, mi))
        return keep

    def bias_lanes(mi):
        rows = jnp.where(sel_ref[mi] > 0.0, 0.0, MASK_BIAS)
        rows = jnp.concatenate([rows, jnp.zeros((LANES - nb, ATTN_BLOCK), F32)], axis=0)
        return rows.T.astype(BF16)

    def block_marker(t):
        lane = lax.broadcasted_iota(jnp.int32, (MOBA_BLOCK, LANES), 1)
        return jnp.concatenate([jnp.where(lane == t * KEY_CHUNKS + ch, 1.0, 0.0).astype(BF16)
                                for ch in range(KEY_CHUNKS)], axis=0)

    def no_marker():
        return jnp.zeros((ATTN_BLOCK, LANES), BF16)

    _prepare_queries(q_ref.at[q_rows], qx_ref, bias_lanes)
    _flash_pipeline(i, qx_ref, k_ref, vt_of, s_ref, tmax_ref, m_ref, alpha_ref, p_ref, acc_ref, diag_keep,
                    diag_extra=no_marker, past_extra=block_marker, first_in_step=sub == 0)

    for hh in range(HEADS_PER_STEP):
        a1, a2 = acc_ref[2 * hh], acc_ref[2 * hh + 1]
        o = jnp.concatenate([a1[:HEAD_DIM] / a1[HEAD_DIM:HEAD_DIM + 1], a2[:HEAD_DIM] / a2[HEAD_DIM:HEAD_DIM + 1]],
                            axis=0)
        o_ref[q_rows, hh * LANES:(hh + 1) * LANES] = o.T.astype(o_ref.dtype)


def _moba_attention(q, k, vt, kmean, *, batch):
    t = q.shape[0]
    seq = t // batch
    nqb = seq // (Q_TILES_PER_STEP * ATTN_BLOCK)
    nb = seq // MOBA_BLOCK
    half = HEAD_DIM + BF16_ROWS
    q_spec, k_spec, vt_spec = _flash_specs(seq, nqb, 2 * half)
    return pl.pallas_call(
        _moba_attn_kernel,
        grid=(batch, PAIRS // HEADS_PER_STEP, nqb),
        in_specs=[q_spec, k_spec, vt_spec,
                  pl.BlockSpec((1, nb, LANES * HEADS_PER_STEP), lambda b, h, i: (b, 0, h))],
        out_specs=q_spec,
        out_shape=jax.ShapeDtypeStruct((t, D_MODEL), BF16),
        scratch_shapes=[pltpu.VMEM((2 * HEADS_PER_STEP, nb, ATTN_BLOCK), F32)] + _flash_scratch(half, 2 * LANES),
        compiler_params=pltpu.CompilerParams(dimension_semantics=("arbitrary",) * 3, vmem_limit_bytes=VMEM_LIMIT),
    )(q, k, vt, kmean)


def _swa_attn_kernel(sink_ref, q_ref, k_ref, vt_ref, o_ref):
    h = pl.program_id(1)
    i = pl.program_id(2)
    sub, w = SWA_SUB_BLOCK, SWA_WINDOW
    subs = SWA_Q_BLOCK // sub
    key = lax.broadcasted_iota(jnp.int32, (sub, sub), 0)
    qry = lax.broadcasted_iota(jnp.int32, (sub, sub), 1)
    own_keep = (key <= qry) & (qry - key < w)
    pkey = lax.broadcasted_iota(jnp.int32, (w, sub), 0)
    pqry = lax.broadcasted_iota(jnp.int32, (w, sub), 1)
    rows = HEAD_DIM + BF16_ROWS
    row0 = pl.multiple_of((h // (PAIRS // SWA_KV_HEADS)) * rows, BF16_ROWS)

    sinks = [sink_ref[2 * h + half] * LOG2_E for half in range(2)]
    chains = []
    for sb in range(subs):
        blk = i * subs + sb
        q = q_ref[sb * sub:(sb + 1) * sub, :]
        q0 = pl.multiple_of(blk * sub, sub)
        prev0 = pl.multiple_of(jnp.maximum(q0 - w, 0), w)
        k_own = _lo_hi(k_ref[pl.ds(q0, sub), :])
        k_prev = _lo_hi(k_ref[pl.ds(prev0, w), :])
        prev_keep = pkey > pqry
        if sb == 0:
            prev_keep = pkey > pqry + jnp.where(i > 0, 0, w)
        for half in range(2):
            s_own = jnp.where(own_keep, _nt(k_own[half], q), NEG_INF)
            s_prev = jnp.where(prev_keep, _nt(k_prev[half], q), NEG_INF)
            m = jnp.maximum(jnp.maximum(jnp.max(s_own, axis=0, keepdims=True),
                                        jnp.max(s_prev, axis=0, keepdims=True)), sinks[half])
            chains.append((s_own, s_prev, m))

    probs = [(jnp.exp2(s_own - m).astype(BF16), jnp.exp2(s_prev - m).astype(BF16)) for s_own, s_prev, m in chains]

    for sb in range(subs):
        chunk0 = (i * subs + sb) * (sub // w)
        vt_prev = vt_ref[0, jnp.maximum(chunk0 - 1, 0), pl.ds(row0, rows), :]
        vt_own = jnp.concatenate([vt_ref[0, chunk0 + c, pl.ds(row0, rows), :] for c in range(sub // w)], axis=1)
        outs = []
        for half in range(2):
            p_own, p_prev = probs[2 * sb + half]
            acc = _nn(vt_own, p_own) + _nn(vt_prev, p_prev)
            denom = acc[HEAD_DIM:HEAD_DIM + 1] + jnp.exp2(sinks[half] - chains[2 * sb + half][2])
            outs.append(acc[:HEAD_DIM] / denom)
        o_ref[sb * sub:(sb + 1) * sub, :] = jnp.concatenate(outs, axis=0).T.astype(o_ref.dtype)


def _swa_attention(q, kdup, vt, sinks, *, batch):
    t = q.shape[0]
    seq = t // batch
    qb = SWA_Q_BLOCK
    nqb = seq // qb
    per_kv = PAIRS // SWA_KV_HEADS
    vt_rows = SWA_KV_HEADS * (HEAD_DIM + BF16_ROWS)
    return pl.pallas_call(
        _swa_attn_kernel,
        grid=(batch, PAIRS, nqb),
        in_specs=[pl.BlockSpec(memory_space=pltpu.SMEM),
                  pl.BlockSpec((qb, LANES), lambda b, h, i: (b * nqb + i, h)),
                  pl.BlockSpec((seq, LANES), lambda b, h, i: (b, h // per_kv)),
                  pl.BlockSpec((1, seq // SWA_WINDOW, vt_rows, SWA_WINDOW), lambda b, h, i: (b, 0, 0, 0))],
        out_specs=pl.BlockSpec((qb, LANES), lambda b, h, i: (b * nqb + i, h)),
        out_shape=jax.ShapeDtypeStruct((t, D_MODEL), BF16),
        compiler_params=pltpu.CompilerParams(dimension_semantics=("arbitrary",) * 3, vmem_limit_bytes=VMEM_LIMIT),
    )(sinks, q, kdup, vt)


def _post_kernel(*refs, final):
    h_ref, o_ref, wout_ref, g_ref, wup_ref, wdn_ref = refs[:6]
    fg_ref = refs[6] if final else None
    out_ref = refs[-1]
    h1 = h_ref[...] + _nn(o_ref[...], wout_ref[...])
    xn = _rms(h1, g_ref[...]).astype(BF16)
    acc = h1
    for c in range(D_FF // FF_CHUNK):
        lo = c * FF_CHUNK
        a = jnp.maximum(_nn(xn, wup_ref[:, lo:lo + FF_CHUNK]), 0.0)
        acc = acc + _nn((a * a).astype(BF16), wdn_ref[lo:lo + FF_CHUNK, :])
    if final:
        acc = _rms(acc, fg_ref[...])
    out_ref[...] = acc


def _post(h, o, w_out, g, w_up, w_down, final_g=None):
    t = h.shape[0]
    rows = PROJ_ROWS
    const = lambda i: (0, 0)
    row = lambda i: (i, 0)
    in_specs = [pl.BlockSpec((rows, D_MODEL), row), pl.BlockSpec((rows, D_MODEL), row),
                pl.BlockSpec(w_out.shape, const), pl.BlockSpec((1, D_MODEL), const),
                pl.BlockSpec(w_up.shape, const), pl.BlockSpec(w_down.shape, const)]
    args = [h, o, w_out, g.reshape(1, D_MODEL), w_up, w_down]
    if final_g is not None:
        in_specs.append(pl.BlockSpec((1, D_MODEL), const))
        args.append(final_g.reshape(1, D_MODEL))
    return pl.pallas_call(
        functools.partial(_post_kernel, final=final_g is not None),
        grid=(t // rows,),
        in_specs=in_specs,
        out_specs=pl.BlockSpec((rows, D_MODEL), row),
        out_shape=jax.ShapeDtypeStruct((t, D_MODEL), F32),
        compiler_params=pltpu.CompilerParams(dimension_semantics=("arbitrary",), vmem_limit_bytes=VMEM_LIMIT),
    )(*args)


def kernel(x, positions, attn_norm_g, mlp_norm_g, diff_w_in, diff_w_out, diff_lam_q1, diff_lam_k1, diff_lam_q2,
           diff_lam_k2, diff_subln_g, moba_w_in, moba_w_out, swa_w_in, swa_b_in, swa_sinks, swa_w_out, mlp_w_up,
           mlp_w_down, final_norm_g):
    batch, seq, _ = x.shape
    depth = attn_norm_g.shape[0]
    tables = _rope_tables(positions)
    h = x.reshape(batch * seq, D_MODEL)
    qk = 2 * D_MODEL
    log2_scale = Q_SCALE * LOG2_E

    for i in range(depth):
        mixer, slot = i % N_MIXERS, i // N_MIXERS
        if mixer == 0:
            w_in = diff_w_in[slot]
            q, k, vt = _project(h, attn_norm_g[i], w_in[:, :qk].astype(BF16), w_in[:, qk:].T.astype(BF16), tables,
                                batch=batch, nq=D_MODEL, kv_block=KEY_CHUNK, q_scale=log2_scale, v_group=LANES)
            o = _diff_attention(q, k, vt,
                                (diff_lam_q1[slot], diff_lam_k1[slot], diff_lam_q2[slot], diff_lam_k2[slot]),
                                diff_subln_g[slot], batch=batch, lambda_init=0.8 - 0.6 * math.exp(-0.3 * i))
            w_out = diff_w_out[slot]
        elif mixer == 1:
            w_in = moba_w_in[slot]
            q, k, vt, kmean = _project(h, attn_norm_g[i], w_in[:, :qk].astype(BF16), w_in[:, qk:].T.astype(BF16),
                                       tables, batch=batch, nq=D_MODEL, kv_block=KEY_CHUNK, q_scale=log2_scale,
                                       v_group=HEAD_DIM, q_dtype=F32, with_kmean=True)
            o = _moba_attention(q, k, vt, kmean.reshape(batch, seq // MOBA_BLOCK, D_MODEL), batch=batch)
            w_out = moba_w_out[slot]
        else:
            w_in, b_in = swa_w_in[slot], swa_b_in[slot]
            dup = lambda a: jnp.concatenate(
                [a[..., D_MODEL + kv * HEAD_DIM:D_MODEL + (kv + 1) * HEAD_DIM]
                 for kv in range(SWA_KV_HEADS) for _ in range(2)], axis=-1)
            v0 = D_MODEL + SWA_KV_HEADS * HEAD_DIM
            wqk = jnp.concatenate([w_in[:, :D_MODEL], dup(w_in)], axis=1).astype(BF16)
            bqk = jnp.concatenate([b_in[:D_MODEL], dup(b_in)])
            q, k, vt = _project(h, attn_norm_g[i], wqk, w_in[:, v0:].T.astype(BF16), tables, batch=batch,
                                nq=D_MODEL, kv_block=SWA_WINDOW, q_scale=log2_scale, v_group=HEAD_DIM,
                                bias=(bqk, b_in[v0:]))
            o = _swa_attention(q, k, vt, swa_sinks[slot], batch=batch)
            w_out = swa_w_out[slot]
        h = _post(h, o, w_out.astype(BF16), mlp_norm_g[i], mlp_w_up[i].astype(BF16), mlp_w_down[i].astype(BF16),
                  final_norm_g if i == depth - 1 else None)
    return h.reshape(batch, seq, D_MODEL)
```

```python
import functools
import math

import jax
import jax.numpy as jnp
from jax import lax
from jax.experimental import pallas as pl
from jax.experimental.pallas import tpu as pltpu

F32 = jnp.float32
BF16 = jnp.bfloat16

D_MODEL = 1024
HEAD_DIM = 64
ROT_DIM = HEAD_DIM // 4
ROT_HALF = ROT_DIM // 2
ROPE_THETA = 500000.0
NORM_EPS = 1e-6
D_FF = 4 * D_MODEL
Q_SCALE = HEAD_DIM ** -0.5
LOG2_E = math.log2(math.e)
N_MIXERS = 3

LANES = 128
SUBLANES = 8
BF16_ROWS = 16
MXU_COLS = 256
PAIRS = D_MODEL // LANES
MOBA_BLOCK = 256
MOBA_TOPK = 3
SWA_WINDOW = 128
SWA_KV_HEADS = 2

ATTN_BLOCK = 512
KEY_CHUNK = 256
KEY_CHUNKS = ATTN_BLOCK // KEY_CHUNK
Q_TILES_PER_STEP = 2
HEADS_PER_STEP = 4
SWA_Q_BLOCK = 1024
SWA_SUB_BLOCK = 256
PROJ_ROWS = 512
FF_CHUNK = 1024
VMEM_LIMIT = 56 * 1024 * 1024

NEG_INF = float("-inf")
M_INIT = -1e30
MASK_BIAS = -1e30


def _nt(a, b, precision=None):
    return lax.dot_general(a, b, (((1,), (1,)), ((), ())), preferred_element_type=F32, precision=precision)


def _nn(a, b):
    return jnp.dot(a, b, preferred_element_type=F32)


def _rms(x, g):
    return x * lax.rsqrt(jnp.mean(x * x, axis=-1, keepdims=True) + NORM_EPS) * g


def _lo_hi(x):
    lane = lax.broadcasted_iota(jnp.int32, (1, LANES), 1)
    zero = jnp.zeros_like(x)
    return jnp.where(lane < HEAD_DIM, x, zero), jnp.where(lane < HEAD_DIM, zero, x)


def _rope_table_kernel(pos_ref, inv_ref, c_ref, sa_ref, sb_ref):
    ang = pos_ref[...].astype(F32) * inv_ref[...]
    lane = lax.broadcasted_iota(jnp.int32, ang.shape, 1) % HEAD_DIM
    cos = jnp.cos(ang)
    sin = jnp.sin(ang)
    c_ref[...] = jnp.where(lane < ROT_DIM, cos, 1.0)
    sa_ref[...] = jnp.where(lane < ROT_HALF, -sin, 0.0)
    sb_ref[...] = jnp.where((lane >= ROT_HALF) & (lane < ROT_DIM), sin, 0.0)


def _rope_tables(positions):
    t = positions.size
    rows = 1024
    inv = ROPE_THETA ** (-jnp.arange(0, ROT_DIM, 2, dtype=F32) / ROT_DIM)
    lane = jnp.arange(LANES) % HEAD_DIM
    inv_lane = jnp.where(lane < ROT_DIM, inv[lane % ROT_HALF], 0.0).reshape(1, LANES)
    tab = jax.ShapeDtypeStruct((t, LANES), F32)
    spec = pl.BlockSpec((rows, LANES), lambda i: (i, 0))
    return pl.pallas_call(
        _rope_table_kernel,
        grid=(t // rows,),
        in_specs=[pl.BlockSpec((rows, 1), lambda i: (i, 0)), pl.BlockSpec((1, LANES), lambda i: (0, 0))],
        out_specs=[spec, spec, spec],
        out_shape=[tab, tab, tab],
    )(positions.reshape(t, 1), inv_lane)


def _proj_kernel(*refs, nq, nk, kv_block, q_scale, v_group, has_bias, with_kmean):
    it = iter(refs)
    h_ref, g_ref, wqk_ref, wvt_ref = next(it), next(it), next(it), next(it)
    bqk_ref = next(it) if has_bias else None
    bv_ref = next(it) if has_bias else None
    c_ref, sa_ref, sb_ref = next(it), next(it), next(it)
    q_ref, k_ref, vt_ref = next(it), next(it), next(it)
    km_ref = next(it) if with_kmean else None

    rows = h_ref.shape[0]
    xn = _rms(h_ref[...], g_ref[...]).astype(BF16)
    cos, sin_a, sin_b = c_ref[...], sa_ref[...], sb_ref[...]

    for c in range((nq + nk) // LANES):
        col = c * LANES
        if c % 2 == 0:
            wide = _nn(xn, wqk_ref[:, col:col + MXU_COLS])
            if has_bias:
                wide = wide + bqk_ref[:, col:col + MXU_COLS]
        y = wide[:, (c % 2) * LANES:(c % 2 + 1) * LANES]
        y = y * cos + pltpu.roll(y, LANES - ROT_HALF, 1) * sin_a + pltpu.roll(y, ROT_HALF, 1) * sin_b
        if col < nq:
            q_ref[:, col:col + LANES] = (y * q_scale).astype(q_ref.dtype)
        else:
            kc = col - nq
            k_ref[:, kc:kc + LANES] = y.astype(k_ref.dtype)
            if with_kmean:
                for r in range(rows // MOBA_BLOCK):
                    blk = y[r * MOBA_BLOCK:(r + 1) * MOBA_BLOCK]
                    km_ref[0, r:r + 1, kc:kc + LANES] = jnp.sum(blk, axis=0, keepdims=True) * (1.0 / MOBA_BLOCK)

    vt = _nt(wvt_ref[...], xn)
    if has_bias:
        vt = vt + bv_ref[...]
    nv = vt.shape[0]
    for r in range(rows // kv_block):
        blk = vt[:, r * kv_block:(r + 1) * kv_block].astype(vt_ref.dtype)
        if v_group is None:
            vt_ref[0, r] = blk
        else:
            stride = v_group + BF16_ROWS
            for g in range(nv // v_group):
                vt_ref[0, r, g * stride:g * stride + v_group] = blk[g * v_group:(g + 1) * v_group]
                vt_ref[0, r, g * stride + v_group:(g + 1) * stride] = jnp.ones((BF16_ROWS, kv_block), vt_ref.dtype)


def _project(h, g, wqk, wvt, tables, *, batch, nq, kv_block, q_scale, v_group=None, q_dtype=BF16, bias=None,
             with_kmean=False):
    t = h.shape[0]
    rows = PROJ_ROWS
    nk = wqk.shape[1] - nq
    nv = wvt.shape[0]
    nv_out = nv if v_group is None else nv // v_group * (v_group + BF16_ROWS)
    seq = t // batch
    tiles_per_seq = seq // rows
    chunks = rows // kv_block

    const = lambda i: (0, 0)
    row = lambda i: (i, 0)
    in_specs = [pl.BlockSpec((rows, D_MODEL), row), pl.BlockSpec((1, D_MODEL), const),
                pl.BlockSpec(wqk.shape, const), pl.BlockSpec(wvt.shape, const)]
    args = [h, g.reshape(1, D_MODEL), wqk, wvt]
    if bias is not None:
        in_specs += [pl.BlockSpec((1, nq + nk), const), pl.BlockSpec((nv, 1), const)]
        args += [bias[0].reshape(1, nq + nk), bias[1].reshape(nv, 1)]
    in_specs += [pl.BlockSpec((rows, LANES), row)] * 3
    args += list(tables)

    out_shape = [jax.ShapeDtypeStruct((t, nq), q_dtype), jax.ShapeDtypeStruct((t, nk), BF16),
                 jax.ShapeDtypeStruct((batch, seq // kv_block, nv_out, kv_block), BF16)]
    out_specs = [pl.BlockSpec((rows, nq), row), pl.BlockSpec((rows, nk), row),
                 pl.BlockSpec((1, chunks, nv_out, kv_block),
                              lambda i: (i // tiles_per_seq, i % tiles_per_seq, 0, 0))]
    if with_kmean:
        out_shape.append(jax.ShapeDtypeStruct((t // rows, rows // MOBA_BLOCK, nk), F32))
        out_specs.append(pl.BlockSpec((1, rows // MOBA_BLOCK, nk), lambda i: (i, 0, 0)))

    return pl.pallas_call(
        functools.partial(_proj_kernel, nq=nq, nk=nk, kv_block=kv_block, q_scale=q_scale, v_group=v_group,
                          has_bias=bias is not None, with_kmean=with_kmean),
        grid=(t // rows,),
        in_specs=in_specs, out_specs=out_specs, out_shape=out_shape,
        compiler_params=pltpu.CompilerParams(dimension_semantics=("arbitrary",), vmem_limit_bytes=VMEM_LIMIT),
    )(*args)


def _prepare_queries(q_ref, qx_ref, extra_lanes=None):
    for hh in range(HEADS_PER_STEP):
        halves = _lo_hi(q_ref[:, hh * LANES:(hh + 1) * LANES].astype(BF16))
        for c in range(2):
            qx_ref[2 * hh + c, :, :LANES] = halves[c]
            if extra_lanes is not None:
                qx_ref[2 * hh + c, :, LANES:] = extra_lanes(2 * hh + c)


def _scores_stage(qx_ref, k_ref, s_ref, tmax_ref, tile, keep_fn, k_extra, diagonal=False):
    row0 = pl.multiple_of(tile * ATTN_BLOCK, ATTN_BLOCK)
    extra = k_extra()
    for hh in range(HEADS_PER_STEP):
        k = k_ref[pl.ds(row0, ATTN_BLOCK), hh * LANES:(hh + 1) * LANES]
        if extra is not None:
            k = jnp.concatenate([k, extra], axis=1)
        for mi in (2 * hh, 2 * hh + 1):
            if diagonal:
                s = jnp.concatenate([
                    jnp.concatenate(
                        [jnp.full((KEY_CHUNK, c * KEY_CHUNK), NEG_INF, F32)] * (c > 0)
                        + [_nt(k[c * KEY_CHUNK:(c + 1) * KEY_CHUNK], qx_ref[mi, c * KEY_CHUNK:, :])], axis=1)
                    for c in range(KEY_CHUNKS)], axis=0)
            else:
                s = _nt(k, qx_ref[mi])
            keep = keep_fn(mi)
            if keep is not None:
                s = jnp.where(keep, s, NEG_INF)
            s_ref[mi] = s
            tmax_ref[mi] = jnp.max(s, axis=0, keepdims=True)


def _softmax_stage(s_ref, tmax_ref, m_ref, alpha_ref, p_ref):
    for mi in range(2 * HEADS_PER_STEP):
        m_old = m_ref[mi]
        m_new = jnp.maximum(m_old, tmax_ref[mi])
        alpha_ref[mi] = jnp.exp2(m_old - m_new)
        m_ref[mi] = m_new
        p_ref[mi] = jnp.exp2(s_ref[mi] - m_new).astype(BF16)


def _values_stage(vt_of, tile, alpha_ref, p_ref, acc_ref):
    for mi in range(2 * HEADS_PER_STEP):
        vt = jnp.concatenate([vt_of(mi, tile * KEY_CHUNKS + ch) for ch in range(KEY_CHUNKS)], axis=1)
        acc_ref[mi] = alpha_ref[mi] * acc_ref[mi] + _nn(vt, p_ref[mi])


def _no_extra(*_):
    return None


def _query_tile(sub):
    return pl.program_id(2) * Q_TILES_PER_STEP + sub, slice(sub * ATTN_BLOCK, (sub + 1) * ATTN_BLOCK)


def _flash_pipeline(i, qx_ref, k_ref, vt_of, s_ref, tmax_ref, m_ref, alpha_ref, p_ref, acc_ref, diag_keep,
                    past_keep=_no_extra, diag_extra=_no_extra, past_extra=_no_extra, first_in_step=True):
    m_ref[...] = jnp.full(m_ref.shape, M_INIT, F32)
    alpha_ref[...] = jnp.ones(alpha_ref.shape, F32)

    if first_in_step:
        @pl.when((pl.program_id(0) == 0) & (pl.program_id(1) == 0) & (pl.program_id(2) == 0))
        def _init():
            p_ref[...] = jnp.zeros(p_ref.shape, BF16)
            acc_ref[...] = jnp.zeros(acc_ref.shape, F32)

    def tile_at(step):
        return jnp.where(step == 0, i, step - 1)

    def values(step):
        _values_stage(vt_of, tile_at(step), alpha_ref, p_ref, acc_ref)

    def softmax():
        _softmax_stage(s_ref, tmax_ref, m_ref, alpha_ref, p_ref)

    def scores(tile, keep_fn, k_extra, diagonal=False):
        _scores_stage(qx_ref, k_ref, s_ref, tmax_ref, tile, keep_fn, k_extra, diagonal)

    scores(i, diag_keep, diag_extra, diagonal=True)

    def body(t, carry):
        values(jnp.maximum(t - 1, 0))
        softmax()
        scores(t, functools.partial(past_keep, t), functools.partial(past_extra, t))
        return carry

    lax.fori_loop(0, i, body, 0)
    values(jnp.maximum(i - 1, 0))
    softmax()
    values(i)


def _tile_iotas():
    shape = (ATTN_BLOCK, ATTN_BLOCK)
    return lax.broadcasted_iota(jnp.int32, shape, 0), lax.broadcasted_iota(jnp.int32, shape, 1)


def _flash_scratch(acc_rows, query_lanes=LANES):
    maps = 2 * HEADS_PER_STEP
    blk = ATTN_BLOCK
    stat = pltpu.VMEM((maps, 1, blk), F32)
    return [pltpu.VMEM((maps, blk, query_lanes), BF16), pltpu.VMEM((maps, blk, blk), F32), stat, stat, stat,
            pltpu.VMEM((maps, blk, blk), BF16), pltpu.VMEM((maps, acc_rows, blk), F32)]


def _flash_specs(seq, nqb, vt_rows):
    width = LANES * HEADS_PER_STEP
    q_spec = pl.BlockSpec((Q_TILES_PER_STEP * ATTN_BLOCK, width), lambda b, h, i: (b * nqb + i, h))
    k_spec = pl.BlockSpec((seq, width), lambda b, h, i: (b, h))
    vt_spec = pl.BlockSpec((1, seq // KEY_CHUNK, vt_rows * HEADS_PER_STEP, KEY_CHUNK), lambda b, h, i: (b, 0, h, 0),
                           pipeline_mode=pl.Buffered(1))
    return q_spec, k_spec, vt_spec


def _diff_attn_kernel(lq1_ref, lk1_ref, lq2_ref, lk2_ref, subg_ref, q_ref, k_ref, vt_ref, o_ref,
                      qx_ref, s_ref, tmax_ref, m_ref, alpha_ref, p_ref, acc_ref, *, lambda_init):
    rows = LANES + BF16_ROWS

    def vt_of(mi, chunk):
        return vt_ref[0, chunk, (mi // 2) * rows:(mi // 2 + 1) * rows]

    def diag_keep(mi):
        key, qry = _tile_iotas()
        return key <= qry

    lam = (jnp.exp(jnp.sum(lq1_ref[...] * lk1_ref[...], axis=1, keepdims=True))
           - jnp.exp(jnp.sum(lq2_ref[...] * lk2_ref[...], axis=1, keepdims=True)) + lambda_init)
    for sub in range(Q_TILES_PER_STEP):
        i, q_rows = _query_tile(sub)
        _prepare_queries(q_ref.at[q_rows], qx_ref)
        _flash_pipeline(i, qx_ref, k_ref, vt_of, s_ref, tmax_ref, m_ref, alpha_ref, p_ref, acc_ref, diag_keep,
                        first_in_step=sub == 0)
        for hh in range(HEADS_PER_STEP):
            a1, a2 = acc_ref[2 * hh], acc_ref[2 * hh + 1]
            o = a1[:LANES] / a1[LANES:LANES + 1] - lam * (a2[:LANES] / a2[LANES:LANES + 1])
            o = o * lax.rsqrt(jnp.mean(o * o, axis=0, keepdims=True) + NORM_EPS) * subg_ref[...]
            o_ref[q_rows, hh * LANES:(hh + 1) * LANES] = (o * (1.0 - lambda_init)).T.astype(o_ref.dtype)


def _diff_attention(q, k, vt, lam_params, sub_g, *, batch, lambda_init):
    t = q.shape[0]
    seq = t // batch
    nqb = seq // (Q_TILES_PER_STEP * ATTN_BLOCK)
    rows = LANES + BF16_ROWS
    small = pl.BlockSpec((1, HEAD_DIM), lambda b, h, i: (0, 0))
    q_spec, k_spec, vt_spec = _flash_specs(seq, nqb, rows)
    return pl.pallas_call(
        functools.partial(_diff_attn_kernel, lambda_init=lambda_init),
        grid=(batch, PAIRS // HEADS_PER_STEP, nqb),
        in_specs=[small, small, small, small, pl.BlockSpec((LANES, 1), lambda b, h, i: (0, 0)),
                  q_spec, k_spec, vt_spec],
        out_specs=q_spec,
        out_shape=jax.ShapeDtypeStruct((t, D_MODEL), BF16),
        scratch_shapes=_flash_scratch(rows),
        compiler_params=pltpu.CompilerParams(dimension_semantics=("arbitrary",) * 3, vmem_limit_bytes=VMEM_LIMIT),
    )(*[p.reshape(1, HEAD_DIM) for p in lam_params], sub_g.reshape(LANES, 1), q, k, vt)


def _moba_select(gate, own):
    nb, width = gate.shape
    groups = range(nb // SUBLANES)
    sub = lax.broadcasted_iota(jnp.int32, (SUBLANES, width), 0)
    ids = [sub + SUBLANES * r for r in groups]
    g = [jnp.where(ids[r] < own, gate[SUBLANES * r:SUBLANES * (r + 1)], NEG_INF) for r in groups]
    sel = [jnp.zeros((SUBLANES, width), F32) for _ in groups]

    def over_blocks(parts, op):
        x = functools.reduce(op, parts)
        for shift in (4, 2, 1):
            x = op(x, pltpu.roll(x, shift, 0))
        return x

    for t in range(MOBA_TOPK):
        best = over_blocks(g, jnp.maximum)
        first = over_blocks([jnp.where(g[r] == best, ids[r], nb) for r in groups], jnp.minimum)
        valid = jnp.where(t < own, 1.0, 0.0)
        for r in groups:
            hit = ids[r] == first
            sel[r] = jnp.maximum(sel[r], jnp.where(hit, valid, 0.0))
            g[r] = jnp.where(hit, NEG_INF, g[r])
    return jnp.concatenate(sel, axis=0)


def _moba_attn_kernel(q_ref, k_ref, vt_ref, km_ref, o_ref,
                      sel_ref, qx_ref, s_ref, tmax_ref, m_ref, alpha_ref, p_ref, acc_ref):
    for sub in range(Q_TILES_PER_STEP):
        _moba_query_tile(sub, q_ref, k_ref, vt_ref, km_ref, o_ref,
                         sel_ref, qx_ref, s_ref, tmax_ref, m_ref, alpha_ref, p_ref, acc_ref)


def _moba_query_tile(sub, q_ref, k_ref, vt_ref, km_ref, o_ref,
                     sel_ref, qx_ref, s_ref, tmax_ref, m_ref, alpha_ref, p_ref, acc_ref):
    i, q_rows = _query_tile(sub)
    half = HEAD_DIM + BF16_ROWS
    maps = 2 * HEADS_PER_STEP
    nb = sel_ref.shape[1]
    gates = []
    for hh in range(HEADS_PER_STEP):
        lanes = slice(hh * LANES, (hh + 1) * LANES)
        qf = q_ref[q_rows, lanes]
        gates += [_nt(km_half, qf, lax.Precision.HIGHEST) for km_half in _lo_hi(km_ref[0, :, lanes])]
    own = i * KEY_CHUNKS + (lax.broadcasted_iota(jnp.int32, (1, maps * ATTN_BLOCK), 1) % ATTN_BLOCK) // MOBA_BLOCK
    sel = _moba_select(jnp.concatenate(gates, axis=1), own)
    for mi in range(maps):
        sel_ref[mi] = sel[:, mi * ATTN_BLOCK:(mi + 1) * ATTN_BLOCK]

    def vt_of(mi, chunk):
        return vt_ref[0, chunk, mi * half:(mi + 1) * half]

    def selected(block, mi):
        return sel_ref[mi, pl.ds(block, 1), :] > 0.0

    def diag_keep(mi):
        key, qry = _tile_iotas()
        key_blk, qry_blk = key // MOBA_BLOCK, qry // MOBA_BLOCK
        keep = (key <= qry) & (key_blk == qry_blk)
        for ch in range(KEY_CHUNKS - 1):
            keep = keep | ((key_blk == ch) & (qry_blk > ch) & selected(i * KEY_CHUNKS + ch, mi))
        return keep

    def bias_lanes(mi):
        rows = jnp.where(sel_ref[mi] > 0.0, 0.0, MASK_BIAS)
        rows = jnp.concatenate([rows, jnp.zeros((LANES - nb, ATTN_BLOCK), F32)], axis=0)
        return rows.T.astype(BF16)

    def block_marker(t):
        lane = lax.broadcasted_iota(jnp.int32, (MOBA_BLOCK, LANES), 1)
        return jnp.concatenate([jnp.where(lane == t * KEY_CHUNKS + ch, 1.0, 0.0).astype(BF16)
                                for ch in range(KEY_CHUNKS)], axis=0)

    def no_marker():
        return jnp.zeros((ATTN_BLOCK, LANES), BF16)

    _prepare_queries(q_ref.at[q_rows], qx_ref, bias_lanes)
    _flash_pipeline(i, qx_ref, k_ref, vt_of, s_ref, tmax_ref, m_ref, alpha_ref, p_ref, acc_ref, diag_keep,
                    diag_extra=no_marker, past_extra=block_marker, first_in_step=sub == 0)

    for hh in range(HEADS_PER_STEP):
        a1, a2 = acc_ref[2 * hh], acc_ref[2 * hh + 1]
        o = jnp.concatenate([a1[:HEAD_DIM] / a1[HEAD_DIM:HEAD_DIM + 1], a2[:HEAD_DIM] / a2[HEAD_DIM:HEAD_DIM + 1]],
                            axis=0)
        o_ref[q_rows, hh * LANES:(hh + 1) * LANES] = o.T.astype(o_ref.dtype)


def _moba_attention(q, k, vt, kmean, *, batch):
    t = q.shape[0]
    seq = t // batch
    nqb = seq // (Q_TILES_PER_STEP * ATTN_BLOCK)
    nb = seq // MOBA_BLOCK
    half = HEAD_DIM + BF16_ROWS
    q_spec, k_spec, vt_spec = _flash_specs(seq, nqb, 2 * half)
    return pl.pallas_call(
        _moba_attn_kernel,
        grid=(batch, PAIRS // HEADS_PER_STEP, nqb),
        in_specs=[q_spec, k_spec, vt_spec,
                  pl.BlockSpec((1, nb, LANES * HEADS_PER_STEP), lambda b, h, i: (b, 0, h))],
        out_specs=q_spec,
        out_shape=jax.ShapeDtypeStruct((t, D_MODEL), BF16),
        scratch_shapes=[pltpu.VMEM((2 * HEADS_PER_STEP, nb, ATTN_BLOCK), F32)] + _flash_scratch(half, 2 * LANES),
        compiler_params=pltpu.CompilerParams(dimension_semantics=("arbitrary",) * 3, vmem_limit_bytes=VMEM_LIMIT),
    )(q, k, vt, kmean)


def _swa_attn_kernel(sink_ref, q_ref, k_ref, vt_ref, o_ref):
    h = pl.program_id(1)
    i = pl.program_id(2)
    sub, w = SWA_SUB_BLOCK, SWA_WINDOW
    subs = SWA_Q_BLOCK // sub
    key = lax.broadcasted_iota(jnp.int32, (sub, sub), 0)
    qry = lax.broadcasted_iota(jnp.int32, (sub, sub), 1)
    own_keep = (key <= qry) & (qry - key < w)
    pkey = lax.broadcasted_iota(jnp.int32, (w, sub), 0)
    pqry = lax.broadcasted_iota(jnp.int32, (w, sub), 1)
    rows = HEAD_DIM + BF16_ROWS
    row0 = pl.multiple_of((h // (PAIRS // SWA_KV_HEADS)) * rows, BF16_ROWS)

    sinks = [sink_ref[2 * h + half] * LOG2_E for half in range(2)]
    chains = []
    for sb in range(subs):
        blk = i * subs + sb
        q = q_ref[sb * sub:(sb + 1) * sub, :]
        q0 = pl.multiple_of(blk * sub, sub)
        prev0 = pl.multiple_of(jnp.maximum(q0 - w, 0), w)
        k_own = _lo_hi(k_ref[pl.ds(q0, sub), :])
        k_prev = _lo_hi(k_ref[pl.ds(prev0, w), :])
        prev_keep = pkey > pqry
        if sb == 0:
            prev_keep = pkey > pqry + jnp.where(i > 0, 0, w)
        for half in range(2):
            s_own = jnp.where(own_keep, _nt(k_own[half], q), NEG_INF)
            s_prev = jnp.where(prev_keep, _nt(k_prev[half], q), NEG_INF)
            m = jnp.maximum(jnp.maximum(jnp.max(s_own, axis=0, keepdims=True),
                                        jnp.max(s_prev, axis=0, keepdims=True)), sinks[half])
            chains.append((s_own, s_prev, m))

    probs = [(jnp.exp2(s_own - m).astype(BF16), jnp.exp2(s_prev - m).astype(BF16)) for s_own, s_prev, m in chains]

    for sb in range(subs):
        chunk0 = (i * subs + sb) * (sub // w)
        vt_prev = vt_ref[0, jnp.maximum(chunk0 - 1, 0), pl.ds(row0, rows), :]
        vt_own = jnp.concatenate([vt_ref[0, chunk0 + c, pl.ds(row0, rows), :] for c in range(sub // w)], axis=1)
        outs = []
        for half in range(2):
            p_own, p_prev = probs[2 * sb + half]
            acc = _nn(vt_own, p_own) + _nn(vt_prev, p_prev)
            denom = acc[HEAD_DIM:HEAD_DIM + 1] + jnp.exp2(sinks[half] - chains[2 * sb + half][2])
            outs.append(acc[:HEAD_DIM] / denom)
        o_ref[sb * sub:(sb + 1) * sub, :] = jnp.concatenate(outs, axis=0).T.astype(o_ref.dtype)


def _swa_attention(q, kdup, vt, sinks, *, batch):
    t = q.shape[0]
    seq = t // batch
    qb = SWA_Q_BLOCK
    nqb = seq // qb
    per_kv = PAIRS // SWA_KV_HEADS
    vt_rows = SWA_KV_HEADS * (HEAD_DIM + BF16_ROWS)
    return pl.pallas_call(
        _swa_attn_kernel,
        grid=(batch, PAIRS, nqb),
        in_specs=[pl.BlockSpec(memory_space=pltpu.SMEM),
                  pl.BlockSpec((qb, LANES), lambda b, h, i: (b * nqb + i, h)),
                  pl.BlockSpec((seq, LANES), lambda b, h, i: (b, h // per_kv)),
                  pl.BlockSpec((1, seq // SWA_WINDOW, vt_rows, SWA_WINDOW), lambda b, h, i: (b, 0, 0, 0))],
        out_specs=pl.BlockSpec((qb, LANES), lambda b, h, i: (b * nqb + i, h)),
        out_shape=jax.ShapeDtypeStruct((t, D_MODEL), BF16),
        compiler_params=pltpu.CompilerParams(dimension_semantics=("arbitrary",) * 3, vmem_limit_bytes=VMEM_LIMIT),
    )(sinks, q, kdup, vt)


def _post_kernel(*refs, final):
    h_ref, o_ref, wout_ref, g_ref, wup_ref, wdn_ref = refs[:6]
    fg_ref = refs[6] if final else None
    out_ref = refs[-1]
    h1 = h_ref[...] + _nn(o_ref[...], wout_ref[...])
    xn = _rms(h1, g_ref[...]).astype(BF16)
    acc = h1
    for c in range(D_FF // FF_CHUNK):
        lo = c * FF_CHUNK
        a = jnp.maximum(_nn(xn, wup_ref[:, lo:lo + FF_CHUNK]), 0.0)
        acc = acc + _nn((a * a).astype(BF16), wdn_ref[lo:lo + FF_CHUNK, :])
    if final:
        acc = _rms(acc, fg_ref[...])
    out_ref[...] = acc


def _post(h, o, w_out, g, w_up, w_down, layer, final_g=None):
    t = h.shape[0]
    rows = PROJ_ROWS
    const = lambda i: (0, 0)
    row = lambda i: (i, 0)
    pick = lambda i: (layer, 0, 0)
    in_specs = [pl.BlockSpec((rows, D_MODEL), row), pl.BlockSpec((rows, D_MODEL), row),
                pl.BlockSpec(w_out.shape, const), pl.BlockSpec((1, D_MODEL), const),
                pl.BlockSpec((None,) + w_up.shape[1:], pick), pl.BlockSpec((None,) + w_down.shape[1:], pick)]
    args = [h, o, w_out, g.reshape(1, D_MODEL), w_up, w_down]
    if final_g is not None:
        in_specs.append(pl.BlockSpec((1, D_MODEL), const))
        args.append(final_g.reshape(1, D_MODEL))
    return pl.pallas_call(
        functools.partial(_post_kernel, final=final_g is not None),
        grid=(t // rows,),
        in_specs=in_specs,
        out_specs=pl.BlockSpec((rows, D_MODEL), row),
        out_shape=jax.ShapeDtypeStruct((t, D_MODEL), F32),
        compiler_params=pltpu.CompilerParams(dimension_semantics=("arbitrary",), vmem_limit_bytes=VMEM_LIMIT),
    )(*args)


def kernel(x, positions, attn_norm_g, mlp_norm_g, diff_w_in, diff_w_out, diff_lam_q1, diff_lam_k1, diff_lam_q2,
           diff_lam_k2, diff_subln_g, moba_w_in, moba_w_out, swa_w_in, swa_b_in, swa_sinks, swa_w_out, mlp_w_up,
           mlp_w_down, final_norm_g):
    batch, seq, _ = x.shape
    depth = attn_norm_g.shape[0]
    tables = _rope_tables(positions)
    h = x.reshape(batch * seq, D_MODEL)
    qk = 2 * D_MODEL
    log2_scale = Q_SCALE * LOG2_E
    w_up_all, w_down_all = mlp_w_up.astype(BF16), mlp_w_down.astype(BF16)

    for i in range(depth):
        mixer, slot = i % N_MIXERS, i // N_MIXERS
        if mixer == 0:
            w_in = diff_w_in[slot]
            q, k, vt = _project(h, attn_norm_g[i], w_in[:, :qk].astype(BF16), w_in[:, qk:].T.astype(BF16), tables,
                                batch=batch, nq=D_MODEL, kv_block=KEY_CHUNK, q_scale=log2_scale, v_group=LANES)
            o = _diff_attention(q, k, vt,
                                (diff_lam_q1[slot], diff_lam_k1[slot], diff_lam_q2[slot], diff_lam_k2[slot]),
                                diff_subln_g[slot], batch=batch, lambda_init=0.8 - 0.6 * math.exp(-0.3 * i))
            w_out = diff_w_out[slot]
        elif mixer == 1:
            w_in = moba_w_in[slot]
            q, k, vt, kmean = _project(h, attn_norm_g[i], w_in[:, :qk].astype(BF16), w_in[:, qk:].T.astype(BF16),
                                       tables, batch=batch, nq=D_MODEL, kv_block=KEY_CHUNK, q_scale=log2_scale,
                                       v_group=HEAD_DIM, q_dtype=F32, with_kmean=True)
            o = _moba_attention(q, k, vt, kmean.reshape(batch, seq // MOBA_BLOCK, D_MODEL), batch=batch)
            w_out = moba_w_out[slot]
        else:
            w_in, b_in = swa_w_in[slot], swa_b_in[slot]
            dup = lambda a: jnp.concatenate(
                [a[..., D_MODEL + kv * HEAD_DIM:D_MODEL + (kv + 1) * HEAD_DIM]
                 for kv in range(SWA_KV_HEADS) for _ in range(2)], axis=-1)
            v0 = D_MODEL + SWA_KV_HEADS * HEAD_DIM
            wqk = jnp.concatenate([w_in[:, :D_MODEL], dup(w_in)], axis=1).astype(BF16)
            bqk = jnp.concatenate([b_in[:D_MODEL], dup(b_in)])
            q, k, vt = _project(h, attn_norm_g[i], wqk, w_in[:, v0:].T.astype(BF16), tables, batch=batch,
                                nq=D_MODEL, kv_block=SWA_WINDOW, q_scale=log2_scale, v_group=HEAD_DIM,
                                bias=(bqk, b_in[v0:]))
            o = _swa_attention(q, k, vt, swa_sinks[slot], batch=batch)
            w_out = swa_w_out[slot]
        h = _post(h, o, w_out.astype(BF16), mlp_norm_g[i], w_up_all, w_down_all, i,
                  final_norm_g if i == depth - 1 else None)
    return h.reshape(batch, seq, D_MODEL)
```

```python
import functools
import math

import jax
import jax.numpy as jnp
from jax import lax
from jax.experimental import pallas as pl
from jax.experimental.pallas import tpu as pltpu

F32 = jnp.float32
BF16 = jnp.bfloat16

D_MODEL = 1024
HEAD_DIM = 64
ROT_DIM = HEAD_DIM // 4
ROT_HALF = ROT_DIM // 2
ROPE_THETA = 500000.0
NORM_EPS = 1e-6
D_FF = 4 * D_MODEL
Q_SCALE = HEAD_DIM ** -0.5
LOG2_E = math.log2(math.e)
N_MIXERS = 3

LANES = 128
SUBLANES = 8
BF16_ROWS = 16
MXU_COLS = 256
PAIRS = D_MODEL // LANES
MOBA_BLOCK = 256
MOBA_TOPK = 3
SWA_WINDOW = 128
SWA_KV_HEADS = 2

ATTN_BLOCK = 512
KEY_CHUNK = 256
KEY_CHUNKS = ATTN_BLOCK // KEY_CHUNK
Q_TILES_PER_STEP = 2
HEADS_PER_STEP = 4
SWA_Q_BLOCK = 1024
SWA_SUB_BLOCK = 256
PROJ_ROWS = 512
FF_CHUNK = 1024
VMEM_LIMIT = 56 * 1024 * 1024

NEG_INF = float("-inf")
M_INIT = -1e30
MASK_BIAS = -1e30


def _nt(a, b, precision=None):
    return lax.dot_general(a, b, (((1,), (1,)), ((), ())), preferred_element_type=F32, precision=precision)


def _nn(a, b):
    return jnp.dot(a, b, preferred_element_type=F32)


def _rms(x, g):
    return x * lax.rsqrt(jnp.mean(x * x, axis=-1, keepdims=True) + NORM_EPS) * g


def _lo_hi(x):
    lane = lax.broadcasted_iota(jnp.int32, (1, LANES), 1)
    zero = jnp.zeros_like(x)
    return jnp.where(lane < HEAD_DIM, x, zero), jnp.where(lane < HEAD_DIM, zero, x)


def _rope_table_kernel(pos_ref, inv_ref, c_ref, sa_ref, sb_ref):
    ang = pos_ref[...].astype(F32) * inv_ref[...]
    lane = lax.broadcasted_iota(jnp.int32, ang.shape, 1) % HEAD_DIM
    cos = jnp.cos(ang)
    sin = jnp.sin(ang)
    c_ref[...] = jnp.where(lane < ROT_DIM, cos, 1.0)
    sa_ref[...] = jnp.where(lane < ROT_HALF, -sin, 0.0)
    sb_ref[...] = jnp.where((lane >= ROT_HALF) & (lane < ROT_DIM), sin, 0.0)


def _rope_tables(positions):
    t = positions.size
    rows = 1024
    inv = ROPE_THETA ** (-jnp.arange(0, ROT_DIM, 2, dtype=F32) / ROT_DIM)
    lane = jnp.arange(LANES) % HEAD_DIM
    inv_lane = jnp.where(lane < ROT_DIM, inv[lane % ROT_HALF], 0.0).reshape(1, LANES)
    tab = jax.ShapeDtypeStruct((t, LANES), F32)
    spec = pl.BlockSpec((rows, LANES), lambda i: (i, 0))
    return pl.pallas_call(
        _rope_table_kernel,
        grid=(t // rows,),
        in_specs=[pl.BlockSpec((rows, 1), lambda i: (i, 0)), pl.BlockSpec((1, LANES), lambda i: (0, 0))],
        out_specs=[spec, spec, spec],
        out_shape=[tab, tab, tab],
    )(positions.reshape(t, 1), inv_lane)


def _proj_kernel(*refs, nq, nk, kv_block, q_scale, v_group, has_bias, with_kmean):
    it = iter(refs)
    h_ref, g_ref, wqk_ref, wvt_ref = next(it), next(it), next(it), next(it)
    bqk_ref = next(it) if has_bias else None
    bv_ref = next(it) if has_bias else None
    c_ref, sa_ref, sb_ref = next(it), next(it), next(it)
    q_ref, k_ref, vt_ref = next(it), next(it), next(it)
    km_ref = next(it) if with_kmean else None

    rows = h_ref.shape[0]
    xn = _rms(h_ref[...], g_ref[...]).astype(BF16)
    cos, sin_a, sin_b = c_ref[...], sa_ref[...], sb_ref[...]

    for c in range((nq + nk) // LANES):
        col = c * LANES
        if c % 2 == 0:
            wide = _nn(xn, wqk_ref[:, col:col + MXU_COLS])
            if has_bias:
                wide = wide + bqk_ref[:, col:col + MXU_COLS]
        y = wide[:, (c % 2) * LANES:(c % 2 + 1) * LANES]
        y = y * cos + pltpu.roll(y, LANES - ROT_HALF, 1) * sin_a + pltpu.roll(y, ROT_HALF, 1) * sin_b
        if col < nq:
            q_ref[:, col:col + LANES] = (y * q_scale).astype(q_ref.dtype)
        else:
            kc = col - nq
            k_ref[:, kc:kc + LANES] = y.astype(k_ref.dtype)
            if with_kmean:
                for r in range(rows // MOBA_BLOCK):
                    blk = y[r * MOBA_BLOCK:(r + 1) * MOBA_BLOCK]
                    km_ref[0, r:r + 1, kc:kc + LANES] = jnp.sum(blk, axis=0, keepdims=True) * (1.0 / MOBA_BLOCK)

    vt = _nt(wvt_ref[...], xn)
    if has_bias:
        vt = vt + bv_ref[...]
    nv = vt.shape[0]
    for r in range(rows // kv_block):
        blk = vt[:, r * kv_block:(r + 1) * kv_block].astype(vt_ref.dtype)
        if v_group is None:
            vt_ref[0, r] = blk
        else:
            stride = v_group + BF16_ROWS
            for g in range(nv // v_group):
                vt_ref[0, r, g * stride:g * stride + v_group] = blk[g * v_group:(g + 1) * v_group]
                vt_ref[0, r, g * stride + v_group:(g + 1) * stride] = jnp.ones((BF16_ROWS, kv_block), vt_ref.dtype)


def _project(h, g, wqk, wvt, tables, *, batch, nq, kv_block, q_scale, v_group=None, q_dtype=BF16, bias=None,
             with_kmean=False):
    t = h.shape[0]
    rows = PROJ_ROWS
    nk = wqk.shape[1] - nq
    nv = wvt.shape[0]
    nv_out = nv if v_group is None else nv // v_group * (v_group + BF16_ROWS)
    seq = t // batch
    tiles_per_seq = seq // rows
    chunks = rows // kv_block

    const = lambda i: (0, 0)
    row = lambda i: (i, 0)
    in_specs = [pl.BlockSpec((rows, D_MODEL), row), pl.BlockSpec((1, D_MODEL), const),
                pl.BlockSpec(wqk.shape, const), pl.BlockSpec(wvt.shape, const)]
    args = [h, g.reshape(1, D_MODEL), wqk, wvt]
    if bias is not None:
        in_specs += [pl.BlockSpec((1, nq + nk), const), pl.BlockSpec((nv, 1), const)]
        args += [bias[0].reshape(1, nq + nk), bias[1].reshape(nv, 1)]
    in_specs += [pl.BlockSpec((rows, LANES), row)] * 3
    args += list(tables)

    out_shape = [jax.ShapeDtypeStruct((t, nq), q_dtype), jax.ShapeDtypeStruct((t, nk), BF16),
                 jax.ShapeDtypeStruct((batch, seq // kv_block, nv_out, kv_block), BF16)]
    out_specs = [pl.BlockSpec((rows, nq), row), pl.BlockSpec((rows, nk), row),
                 pl.BlockSpec((1, chunks, nv_out, kv_block),
                              lambda i: (i // tiles_per_seq, i % tiles_per_seq, 0, 0))]
    if with_kmean:
        out_shape.append(jax.ShapeDtypeStruct((t // rows, rows // MOBA_BLOCK, nk), F32))
        out_specs.append(pl.BlockSpec((1, rows // MOBA_BLOCK, nk), lambda i: (i, 0, 0)))

    return pl.pallas_call(
        functools.partial(_proj_kernel, nq=nq, nk=nk, kv_block=kv_block, q_scale=q_scale, v_group=v_group,
                          has_bias=bias is not None, with_kmean=with_kmean),
        grid=(t // rows,),
        in_specs=in_specs, out_specs=out_specs, out_shape=out_shape,
        compiler_params=pltpu.CompilerParams(dimension_semantics=("arbitrary",), vmem_limit_bytes=VMEM_LIMIT),
    )(*args)


def _prepare_queries(q_ref, qx_ref, spare_lanes=None):
    for hh in range(HEADS_PER_STEP):
        halves = _lo_hi(q_ref[:, hh * LANES:(hh + 1) * LANES].astype(BF16))
        for c in range(2):
            qx = halves[c]
            if spare_lanes is not None:
                qx = qx + spare_lanes(2 * hh + c)
            qx_ref[2 * hh + c] = qx


def _scores_stage(qx_ref, k_ref, s_ref, tmax_ref, tile, keep_fn, k_spare, diagonal=False):
    row0 = pl.multiple_of(tile * ATTN_BLOCK, ATTN_BLOCK)
    spare = k_spare()
    for hh in range(HEADS_PER_STEP):
        k_head = k_ref[pl.ds(row0, ATTN_BLOCK), hh * LANES:(hh + 1) * LANES]
        for c, mi in enumerate((2 * hh, 2 * hh + 1)):
            k = k_head
            if spare is not None:
                lane = lax.broadcasted_iota(jnp.int32, (1, LANES), 1)
                k = jnp.where(lane < HEAD_DIM if c == 0 else lane >= HEAD_DIM, k_head, spare[c])
            if diagonal:
                s = jnp.concatenate([
                    jnp.concatenate(
                        [jnp.full((KEY_CHUNK, c * KEY_CHUNK), NEG_INF, F32)] * (c > 0)
                        + [_nt(k[c * KEY_CHUNK:(c + 1) * KEY_CHUNK], qx_ref[mi, c * KEY_CHUNK:, :])], axis=1)
                    for c in range(KEY_CHUNKS)], axis=0)
            else:
                s = _nt(k, qx_ref[mi])
            keep = keep_fn(mi)
            if keep is not None:
                s = jnp.where(keep, s, NEG_INF)
            s_ref[mi] = s
            tmax_ref[mi] = jnp.max(s, axis=0, keepdims=True)


def _softmax_stage(s_ref, tmax_ref, m_ref, alpha_ref, p_ref):
    for mi in range(2 * HEADS_PER_STEP):
        m_old = m_ref[mi]
        m_new = jnp.maximum(m_old, tmax_ref[mi])
        alpha_ref[mi] = jnp.exp2(m_old - m_new)
        m_ref[mi] = m_new
        p_ref[mi] = jnp.exp2(s_ref[mi] - m_new).astype(BF16)


def _values_stage(vt_of, tile, alpha_ref, p_ref, acc_ref):
    for mi in range(2 * HEADS_PER_STEP):
        vt = jnp.concatenate([vt_of(mi, tile * KEY_CHUNKS + ch) for ch in range(KEY_CHUNKS)], axis=1)
        acc_ref[mi] = alpha_ref[mi] * acc_ref[mi] + _nn(vt, p_ref[mi])


def _no_extra(*_):
    return None


def _query_tile(sub):
    return pl.program_id(2) * Q_TILES_PER_STEP + sub, slice(sub * ATTN_BLOCK, (sub + 1) * ATTN_BLOCK)


def _flash_pipeline(i, qx_ref, k_ref, vt_of, s_ref, tmax_ref, m_ref, alpha_ref, p_ref, acc_ref, diag_keep,
                    past_keep=_no_extra, diag_extra=_no_extra, past_extra=_no_extra, first_in_step=True):
    m_ref[...] = jnp.full(m_ref.shape, M_INIT, F32)
    alpha_ref[...] = jnp.ones(alpha_ref.shape, F32)

    if first_in_step:
        @pl.when((pl.program_id(0) == 0) & (pl.program_id(1) == 0) & (pl.program_id(2) == 0))
        def _init():
            p_ref[...] = jnp.zeros(p_ref.shape, BF16)
            acc_ref[...] = jnp.zeros(acc_ref.shape, F32)

    def tile_at(step):
        return jnp.where(step == 0, i, step - 1)

    def values(step):
        _values_stage(vt_of, tile_at(step), alpha_ref, p_ref, acc_ref)

    def softmax():
        _softmax_stage(s_ref, tmax_ref, m_ref, alpha_ref, p_ref)

    def scores(tile, keep_fn, k_extra, diagonal=False):
        _scores_stage(qx_ref, k_ref, s_ref, tmax_ref, tile, keep_fn, k_extra, diagonal)

    scores(i, diag_keep, diag_extra, diagonal=True)

    def body(t, carry):
        softmax()
        scores(t, functools.partial(past_keep, t), functools.partial(past_extra, t))
        values(t)
        return carry

    lax.fori_loop(0, i, body, 0)
    softmax()
    values(i)


def _tile_iotas():
    shape = (ATTN_BLOCK, ATTN_BLOCK)
    return lax.broadcasted_iota(jnp.int32, shape, 0), lax.broadcasted_iota(jnp.int32, shape, 1)


def _flash_scratch(acc_rows):
    maps = 2 * HEADS_PER_STEP
    blk = ATTN_BLOCK
    stat = pltpu.VMEM((maps, 1, blk), F32)
    return [pltpu.VMEM((maps, blk, LANES), BF16), pltpu.VMEM((maps, blk, blk), F32), stat, stat, stat,
            pltpu.VMEM((maps, blk, blk), BF16), pltpu.VMEM((maps, acc_rows, blk), F32)]


def _flash_specs(seq, nqb, vt_rows):
    width = LANES * HEADS_PER_STEP
    q_spec = pl.BlockSpec((Q_TILES_PER_STEP * ATTN_BLOCK, width), lambda b, h, i: (b * nqb + i, h))
    k_spec = pl.BlockSpec((seq, width), lambda b, h, i: (b, h))
    vt_spec = pl.BlockSpec((1, seq // KEY_CHUNK, vt_rows * HEADS_PER_STEP, KEY_CHUNK), lambda b, h, i: (b, 0, h, 0),
                           pipeline_mode=pl.Buffered(1))
    return q_spec, k_spec, vt_spec


def _diff_attn_kernel(lq1_ref, lk1_ref, lq2_ref, lk2_ref, subg_ref, q_ref, k_ref, vt_ref, o_ref,
                      qx_ref, s_ref, tmax_ref, m_ref, alpha_ref, p_ref, acc_ref, *, lambda_init):
    rows = LANES + BF16_ROWS

    def vt_of(mi, chunk):
        return vt_ref[0, chunk, (mi // 2) * rows:(mi // 2 + 1) * rows]

    def diag_keep(mi):
        key, qry = _tile_iotas()
        return key <= qry

    lam = (jnp.exp(jnp.sum(lq1_ref[...] * lk1_ref[...], axis=1, keepdims=True))
           - jnp.exp(jnp.sum(lq2_ref[...] * lk2_ref[...], axis=1, keepdims=True)) + lambda_init)
    for sub in range(Q_TILES_PER_STEP):
        i, q_rows = _query_tile(sub)
        _prepare_queries(q_ref.at[q_rows], qx_ref)
        _flash_pipeline(i, qx_ref, k_ref, vt_of, s_ref, tmax_ref, m_ref, alpha_ref, p_ref, acc_ref, diag_keep,
                        first_in_step=sub == 0)
        for hh in range(HEADS_PER_STEP):
            a1, a2 = acc_ref[2 * hh], acc_ref[2 * hh + 1]
            o = a1[:LANES] / a1[LANES:LANES + 1] - lam * (a2[:LANES] / a2[LANES:LANES + 1])
            o = o * lax.rsqrt(jnp.mean(o * o, axis=0, keepdims=True) + NORM_EPS) * subg_ref[...]
            o_ref[q_rows, hh * LANES:(hh + 1) * LANES] = (o * (1.0 - lambda_init)).T.astype(o_ref.dtype)


def _diff_attention(q, k, vt, lam_params, sub_g, *, batch, lambda_init):
    t = q.shape[0]
    seq = t // batch
    nqb = seq // (Q_TILES_PER_STEP * ATTN_BLOCK)
    rows = LANES + BF16_ROWS
    small = pl.BlockSpec((1, HEAD_DIM), lambda b, h, i: (0, 0))
    q_spec, k_spec, vt_spec = _flash_specs(seq, nqb, rows)
    return pl.pallas_call(
        functools.partial(_diff_attn_kernel, lambda_init=lambda_init),
        grid=(batch, PAIRS // HEADS_PER_STEP, nqb),
        in_specs=[small, small, small, small, pl.BlockSpec((LANES, 1), lambda b, h, i: (0, 0)),
                  q_spec, k_spec, vt_spec],
        out_specs=q_spec,
        out_shape=jax.ShapeDtypeStruct((t, D_MODEL), BF16),
        scratch_shapes=_flash_scratch(rows),
        compiler_params=pltpu.CompilerParams(dimension_semantics=("arbitrary",) * 3, vmem_limit_bytes=VMEM_LIMIT),
    )(*[p.reshape(1, HEAD_DIM) for p in lam_params], sub_g.reshape(LANES, 1), q, k, vt)


def _moba_select(gate, own):
    nb, width = gate.shape
    groups = range(nb // SUBLANES)
    sub = lax.broadcasted_iota(jnp.int32, (SUBLANES, width), 0)
    ids = [sub + SUBLANES * r for r in groups]
    g = [jnp.where(ids[r] < own, gate[SUBLANES * r:SUBLANES * (r + 1)], NEG_INF) for r in groups]
    sel = [jnp.zeros((SUBLANES, width), F32) for _ in groups]

    def over_blocks(parts, op):
        x = functools.reduce(op, parts)
        for shift in (4, 2, 1):
            x = op(x, pltpu.roll(x, shift, 0))
        return x

    for t in range(MOBA_TOPK):
        best = over_blocks(g, jnp.maximum)
        first = over_blocks([jnp.where(g[r] == best, ids[r], nb) for r in groups], jnp.minimum)
        valid = jnp.where(t < own, 1.0, 0.0)
        for r in groups:
            hit = ids[r] == first
            sel[r] = jnp.maximum(sel[r], jnp.where(hit, valid, 0.0))
            g[r] = jnp.where(hit, NEG_INF, g[r])
    return jnp.concatenate(sel, axis=0)


def _moba_attn_kernel(q_ref, k_ref, vt_ref, km_ref, o_ref,
                      sel_ref, qx_ref, s_ref, tmax_ref, m_ref, alpha_ref, p_ref, acc_ref):
    for sub in range(Q_TILES_PER_STEP):
        _moba_query_tile(sub, q_ref, k_ref, vt_ref, km_ref, o_ref,
                         sel_ref, qx_ref, s_ref, tmax_ref, m_ref, alpha_ref, p_ref, acc_ref)


def _moba_query_tile(sub, q_ref, k_ref, vt_ref, km_ref, o_ref,
                     sel_ref, qx_ref, s_ref, tmax_ref, m_ref, alpha_ref, p_ref, acc_ref):
    i, q_rows = _query_tile(sub)
    half = HEAD_DIM + BF16_ROWS
    maps = 2 * HEADS_PER_STEP
    nb = sel_ref.shape[1]
    gates = []
    for hh in range(HEADS_PER_STEP):
        lanes = slice(hh * LANES, (hh + 1) * LANES)
        qf = q_ref[q_rows, lanes]
        gates += [_nt(km_half, qf, lax.Precision.HIGHEST) for km_half in _lo_hi(km_ref[0, :, lanes])]
    own = i * KEY_CHUNKS + (lax.broadcasted_iota(jnp.int32, (1, maps * ATTN_BLOCK), 1) % ATTN_BLOCK) // MOBA_BLOCK
    sel = _moba_select(jnp.concatenate(gates, axis=1), own)
    for mi in range(maps):
        sel_ref[mi] = sel[:, mi * ATTN_BLOCK:(mi + 1) * ATTN_BLOCK]

    def vt_of(mi, chunk):
        return vt_ref[0, chunk, mi * half:(mi + 1) * half]

    def selected(block, mi):
        return sel_ref[mi, pl.ds(block, 1), :] > 0.0

    def diag_keep(mi):
        key, qry = _tile_iotas()
        key_blk, qry_blk = key // MOBA_BLOCK, qry // MOBA_BLOCK
        keep = (key <= qry) & (key_blk == qry_blk)
        for ch in range(KEY_CHUNKS - 1):
            keep = keep | ((key_blk == ch) & (qry_blk > ch) & selected(i * KEY_CHUNKS + ch, mi))
        return keep

    def spare_base(c):
        return HEAD_DIM * (1 - c)

    def bias_lanes(mi):
        rows = jnp.where(sel_ref[mi] > 0.0, 0.0, MASK_BIAS)
        above = spare_base(mi % 2)
        rows = jnp.concatenate([jnp.zeros((above, ATTN_BLOCK), F32)] * (above > 0) + [rows]
                               + [jnp.zeros((LANES - above - nb, ATTN_BLOCK), F32)], axis=0)
        return rows.T.astype(BF16)

    def block_marker(t):
        lane = lax.broadcasted_iota(jnp.int32, (MOBA_BLOCK, LANES), 1)
        return [jnp.concatenate([jnp.where(lane == spare_base(c) + t * KEY_CHUNKS + ch, 1.0, 0.0).astype(BF16)
                                 for ch in range(KEY_CHUNKS)], axis=0) for c in range(2)]

    def no_marker():
        return [jnp.zeros((ATTN_BLOCK, LANES), BF16)] * 2

    _prepare_queries(q_ref.at[q_rows], qx_ref, bias_lanes)
    _flash_pipeline(i, qx_ref, k_ref, vt_of, s_ref, tmax_ref, m_ref, alpha_ref, p_ref, acc_ref, diag_keep,
                    diag_extra=no_marker, past_extra=block_marker, first_in_step=sub == 0)

    for hh in range(HEADS_PER_STEP):
        a1, a2 = acc_ref[2 * hh], acc_ref[2 * hh + 1]
        o = jnp.concatenate([a1[:HEAD_DIM] / a1[HEAD_DIM:HEAD_DIM + 1], a2[:HEAD_DIM] / a2[HEAD_DIM:HEAD_DIM + 1]],
                            axis=0)
        o_ref[q_rows, hh * LANES:(hh + 1) * LANES] = o.T.astype(o_ref.dtype)


def _moba_attention(q, k, vt, kmean, *, batch):
    t = q.shape[0]
    seq = t // batch
    nqb = seq // (Q_TILES_PER_STEP * ATTN_BLOCK)
    nb = seq // MOBA_BLOCK
    half = HEAD_DIM + BF16_ROWS
    q_spec, k_spec, vt_spec = _flash_specs(seq, nqb, 2 * half)
    return pl.pallas_call(
        _moba_attn_kernel,
        grid=(batch, PAIRS // HEADS_PER_STEP, nqb),
        in_specs=[q_spec, k_spec, vt_spec,
                  pl.BlockSpec((1, nb, LANES * HEADS_PER_STEP), lambda b, h, i: (b, 0, h))],
        out_specs=q_spec,
        out_shape=jax.ShapeDtypeStruct((t, D_MODEL), BF16),
        scratch_shapes=[pltpu.VMEM((2 * HEADS_PER_STEP, nb, ATTN_BLOCK), F32)] + _flash_scratch(half),
        compiler_params=pltpu.CompilerParams(dimension_semantics=("arbitrary",) * 3, vmem_limit_bytes=VMEM_LIMIT),
    )(q, k, vt, kmean)


def _swa_attn_kernel(sink_ref, q_ref, k_ref, vt_ref, o_ref):
    h = pl.program_id(1)
    i = pl.program_id(2)
    sub, w = SWA_SUB_BLOCK, SWA_WINDOW
    subs = SWA_Q_BLOCK // sub
    key = lax.broadcasted_iota(jnp.int32, (sub, sub), 0)
    qry = lax.broadcasted_iota(jnp.int32, (sub, sub), 1)
    own_keep = (key <= qry) & (qry - key < w)
    pkey = lax.broadcasted_iota(jnp.int32, (w, sub), 0)
    pqry = lax.broadcasted_iota(jnp.int32, (w, sub), 1)
    rows = HEAD_DIM + BF16_ROWS
    row0 = pl.multiple_of((h // (PAIRS // SWA_KV_HEADS)) * rows, BF16_ROWS)

    sinks = [sink_ref[2 * h + half] * LOG2_E for half in range(2)]
    chains = []
    for sb in range(subs):
        blk = i * subs + sb
        q = q_ref[sb * sub:(sb + 1) * sub, :]
        q0 = pl.multiple_of(blk * sub, sub)
        prev0 = pl.multiple_of(jnp.maximum(q0 - w, 0), w)
        k_own = _lo_hi(k_ref[pl.ds(q0, sub), :])
        k_prev = _lo_hi(k_ref[pl.ds(prev0, w), :])
        prev_keep = pkey > pqry
        if sb == 0:
            prev_keep = pkey > pqry + jnp.where(i > 0, 0, w)
        for half in range(2):
            s_own = jnp.where(own_keep, _nt(k_own[half], q), NEG_INF)
            s_prev = jnp.where(prev_keep, _nt(k_prev[half], q), NEG_INF)
            m = jnp.maximum(jnp.maximum(jnp.max(s_own, axis=0, keepdims=True),
                                        jnp.max(s_prev, axis=0, keepdims=True)), sinks[half])
            chains.append((s_own, s_prev, m))

    probs = [(jnp.exp2(s_own - m).astype(BF16), jnp.exp2(s_prev - m).astype(BF16)) for s_own, s_prev, m in chains]

    for sb in range(subs):
        chunk0 = (i * subs + sb) * (sub // w)
        vt_prev = vt_ref[0, jnp.maximum(chunk0 - 1, 0), pl.ds(row0, rows), :]
        vt_own = jnp.concatenate([vt_ref[0, chunk0 + c, pl.ds(row0, rows), :] for c in range(sub // w)], axis=1)
        outs = []
        for half in range(2):
            p_own, p_prev = probs[2 * sb + half]
            acc = _nn(vt_own, p_own) + _nn(vt_prev, p_prev)
            denom = acc[HEAD_DIM:HEAD_DIM + 1] + jnp.exp2(sinks[half] - chains[2 * sb + half][2])
            outs.append(acc[:HEAD_DIM] / denom)
        o_ref[sb * sub:(sb + 1) * sub, :] = jnp.concatenate(outs, axis=0).T.astype(o_ref.dtype)


def _swa_attention(q, kdup, vt, sinks, *, batch):
    t = q.shape[0]
    seq = t // batch
    qb = SWA_Q_BLOCK
    nqb = seq // qb
    per_kv = PAIRS // SWA_KV_HEADS
    vt_rows = SWA_KV_HEADS * (HEAD_DIM + BF16_ROWS)
    return pl.pallas_call(
        _swa_attn_kernel,
        grid=(batch, PAIRS, nqb),
        in_specs=[pl.BlockSpec(memory_space=pltpu.SMEM),
                  pl.BlockSpec((qb, LANES), lambda b, h, i: (b * nqb + i, h)),
                  pl.BlockSpec((seq, LANES), lambda b, h, i: (b, h // per_kv)),
                  pl.BlockSpec((1, seq // SWA_WINDOW, vt_rows, SWA_WINDOW), lambda b, h, i: (b, 0, 0, 0))],
        out_specs=pl.BlockSpec((qb, LANES), lambda b, h, i: (b * nqb + i, h)),
        out_shape=jax.ShapeDtypeStruct((t, D_MODEL), BF16),
        compiler_params=pltpu.CompilerParams(dimension_semantics=("arbitrary",) * 3, vmem_limit_bytes=VMEM_LIMIT),
    )(sinks, q, kdup, vt)


def _post_kernel(*refs, final):
    h_ref, o_ref, wout_ref, g_ref, wup_ref, wdn_ref = refs[:6]
    fg_ref = refs[6] if final else None
    out_ref = refs[-1]
    h1 = h_ref[...] + _nn(o_ref[...], wout_ref[...])
    xn = _rms(h1, g_ref[...]).astype(BF16)
    acc = h1
    for c in range(D_FF // FF_CHUNK):
        lo = c * FF_CHUNK
        a = jnp.maximum(_nn(xn, wup_ref[:, lo:lo + FF_CHUNK]), 0.0)
        acc = acc + _nn((a * a).astype(BF16), wdn_ref[lo:lo + FF_CHUNK, :])
    if final:
        acc = _rms(acc, fg_ref[...])
    out_ref[...] = acc


def _post(h, o, w_out, g, w_up, w_down, layer, final_g=None):
    t = h.shape[0]
    rows = PROJ_ROWS
    const = lambda i: (0, 0)
    row = lambda i: (i, 0)
    pick = lambda i: (layer, 0, 0)
    in_specs = [pl.BlockSpec((rows, D_MODEL), row), pl.BlockSpec((rows, D_MODEL), row),
                pl.BlockSpec(w_out.shape, const), pl.BlockSpec((1, D_MODEL), const),
                pl.BlockSpec((None,) + w_up.shape[1:], pick), pl.BlockSpec((None,) + w_down.shape[1:], pick)]
    args = [h, o, w_out, g.reshape(1, D_MODEL), w_up, w_down]
    if final_g is not None:
        in_specs.append(pl.BlockSpec((1, D_MODEL), const))
        args.append(final_g.reshape(1, D_MODEL))
    return pl.pallas_call(
        functools.partial(_post_kernel, final=final_g is not None),
        grid=(t // rows,),
        in_specs=in_specs,
        out_specs=pl.BlockSpec((rows, D_MODEL), row),
        out_shape=jax.ShapeDtypeStruct((t, D_MODEL), F32),
        compiler_params=pltpu.CompilerParams(dimension_semantics=("arbitrary",), vmem_limit_bytes=VMEM_LIMIT),
    )(*args)


def kernel(x, positions, attn_norm_g, mlp_norm_g, diff_w_in, diff_w_out, diff_lam_q1, diff_lam_k1, diff_lam_q2,
           diff_lam_k2, diff_subln_g, moba_w_in, moba_w_out, swa_w_in, swa_b_in, swa_sinks, swa_w_out, mlp_w_up,
           mlp_w_down, final_norm_g):
    batch, seq, _ = x.shape
    depth = attn_norm_g.shape[0]
    tables = _rope_tables(positions)
    h = x.reshape(batch * seq, D_MODEL)
    qk = 2 * D_MODEL
    log2_scale = Q_SCALE * LOG2_E
    w_up_all, w_down_all = mlp_w_up.astype(BF16), mlp_w_down.astype(BF16)

    for i in range(depth):
        mixer, slot = i % N_MIXERS, i // N_MIXERS
        if mixer == 0:
            w_in = diff_w_in[slot]
            q, k, vt = _project(h, attn_norm_g[i], w_in[:, :qk].astype(BF16), w_in[:, qk:].T.astype(BF16), tables,
                                batch=batch, nq=D_MODEL, kv_block=KEY_CHUNK, q_scale=log2_scale, v_group=LANES)
            o = _diff_attention(q, k, vt,
                                (diff_lam_q1[slot], diff_lam_k1[slot], diff_lam_q2[slot], diff_lam_k2[slot]),
                                diff_subln_g[slot], batch=batch, lambda_init=0.8 - 0.6 * math.exp(-0.3 * i))
            w_out = diff_w_out[slot]
        elif mixer == 1:
            w_in = moba_w_in[slot]
            q, k, vt, kmean = _project(h, attn_norm_g[i], w_in[:, :qk].astype(BF16), w_in[:, qk:].T.astype(BF16),
                                       tables, batch=batch, nq=D_MODEL, kv_block=KEY_CHUNK, q_scale=log2_scale,
                                       v_group=HEAD_DIM, q_dtype=F32, with_kmean=True)
            o = _moba_attention(q, k, vt, kmean.reshape(batch, seq // MOBA_BLOCK, D_MODEL), batch=batch)
            w_out = moba_w_out[slot]
        else:
            w_in, b_in = swa_w_in[slot], swa_b_in[slot]
            dup = lambda a: jnp.concatenate(
                [a[..., D_MODEL + kv * HEAD_DIM:D_MODEL + (kv + 1) * HEAD_DIM]
                 for kv in range(SWA_KV_HEADS) for _ in range(2)], axis=-1)
            v0 = D_MODEL + SWA_KV_HEADS * HEAD_DIM
            wqk = jnp.concatenate([w_in[:, :D_MODEL], dup(w_in)], axis=1).astype(BF16)
            bqk = jnp.concatenate([b_in[:D_MODEL], dup(b_in)])
            q, k, vt = _project(h, attn_norm_g[i], wqk, w_in[:, v0:].T.astype(BF16), tables, batch=batch,
                                nq=D_MODEL, kv_block=SWA_WINDOW, q_scale=log2_scale, v_group=HEAD_DIM,
                                bias=(bqk, b_in[v0:]))
            o = _swa_attention(q, k, vt, swa_sinks[slot], batch=batch)
            w_out = swa_w_out[slot]
        h = _post(h, o, w_out.astype(BF16), mlp_norm_g[i], w_up_all, w_down_all, i,
                  final_norm_g if i == depth - 1 else None)
    return h.reshape(batch, seq, D_MODEL)
```

```python
import functools
import math

import jax
import jax.numpy as jnp
from jax import lax
from jax.experimental import pallas as pl
from jax.experimental.pallas import tpu as pltpu

F32 = jnp.float32
BF16 = jnp.bfloat16

D_MODEL = 1024
HEAD_DIM = 64
ROT_DIM = HEAD_DIM // 4
ROT_HALF = ROT_DIM // 2
ROPE_THETA = 500000.0
NORM_EPS = 1e-6
D_FF = 4 * D_MODEL
Q_SCALE = HEAD_DIM ** -0.5
LOG2_E = math.log2(math.e)
N_MIXERS = 3

LANES = 128
SUBLANES = 8
BF16_ROWS = 16
MXU_COLS = 256
PAIRS = D_MODEL // LANES
MOBA_BLOCK = 256
MOBA_TOPK = 3
SWA_WINDOW = 128
SWA_KV_HEADS = 2

ATTN_BLOCK = 512
KEY_CHUNK = 256
KEY_CHUNKS = ATTN_BLOCK // KEY_CHUNK
Q_TILES_PER_STEP = 4
HEADS_PER_STEP = 4
SWA_Q_BLOCK = 1024
SWA_SUB_BLOCK = 256
PROJ_ROWS = 512
FF_CHUNK = 1024
VMEM_LIMIT = 56 * 1024 * 1024

NEG_INF = float("-inf")
M_INIT = -1e30
MASK_BIAS = -1e30


def _nt(a, b, precision=None):
    return lax.dot_general(a, b, (((1,), (1,)), ((), ())), preferred_element_type=F32, precision=precision)


def _nn(a, b):
    return jnp.dot(a, b, preferred_element_type=F32)


def _rms(x, g):
    return x * lax.rsqrt(jnp.mean(x * x, axis=-1, keepdims=True) + NORM_EPS) * g


def _lo_hi(x):
    lane = lax.broadcasted_iota(jnp.int32, (1, LANES), 1)
    zero = jnp.zeros_like(x)
    return jnp.where(lane < HEAD_DIM, x, zero), jnp.where(lane < HEAD_DIM, zero, x)


def _rope_table_kernel(pos_ref, inv_ref, c_ref, sa_ref, sb_ref):
    ang = pos_ref[...].astype(F32) * inv_ref[...]
    lane = lax.broadcasted_iota(jnp.int32, ang.shape, 1) % HEAD_DIM
    cos = jnp.cos(ang)
    sin = jnp.sin(ang)
    c_ref[...] = jnp.where(lane < ROT_DIM, cos, 1.0)
    sa_ref[...] = jnp.where(lane < ROT_HALF, -sin, 0.0)
    sb_ref[...] = jnp.where((lane >= ROT_HALF) & (lane < ROT_DIM), sin, 0.0)


def _rope_tables(positions):
    t = positions.size
    rows = 1024
    inv = ROPE_THETA ** (-jnp.arange(0, ROT_DIM, 2, dtype=F32) / ROT_DIM)
    lane = jnp.arange(LANES) % HEAD_DIM
    inv_lane = jnp.where(lane < ROT_DIM, inv[lane % ROT_HALF], 0.0).reshape(1, LANES)
    tab = jax.ShapeDtypeStruct((t, LANES), F32)
    spec = pl.BlockSpec((rows, LANES), lambda i: (i, 0))
    return pl.pallas_call(
        _rope_table_kernel,
        grid=(t // rows,),
        in_specs=[pl.BlockSpec((rows, 1), lambda i: (i, 0)), pl.BlockSpec((1, LANES), lambda i: (0, 0))],
        out_specs=[spec, spec, spec],
        out_shape=[tab, tab, tab],
    )(positions.reshape(t, 1), inv_lane)


def _proj_kernel(*refs, nq, nk, kv_block, q_scale, v_group, has_bias, with_kmean):
    it = iter(refs)
    h_ref, g_ref, wqk_ref, wvt_ref = next(it), next(it), next(it), next(it)
    bqk_ref = next(it) if has_bias else None
    bv_ref = next(it) if has_bias else None
    c_ref, sa_ref, sb_ref = next(it), next(it), next(it)
    q_ref, k_ref, vt_ref = next(it), next(it), next(it)
    km_ref = next(it) if with_kmean else None

    rows = h_ref.shape[0]
    xn = _rms(h_ref[...], g_ref[...]).astype(BF16)
    cos, sin_a, sin_b = c_ref[...], sa_ref[...], sb_ref[...]

    for c in range((nq + nk) // LANES):
        col = c * LANES
        if c % 2 == 0:
            wide = _nn(xn, wqk_ref[:, col:col + MXU_COLS])
            if has_bias:
                wide = wide + bqk_ref[:, col:col + MXU_COLS]
        y = wide[:, (c % 2) * LANES:(c % 2 + 1) * LANES]
        y = y * cos + pltpu.roll(y, LANES - ROT_HALF, 1) * sin_a + pltpu.roll(y, ROT_HALF, 1) * sin_b
        if col < nq:
            q_ref[:, col:col + LANES] = (y * q_scale).astype(q_ref.dtype)
        else:
            kc = col - nq
            k_ref[:, kc:kc + LANES] = y.astype(k_ref.dtype)
            if with_kmean:
                for r in range(rows // MOBA_BLOCK):
                    blk = y[r * MOBA_BLOCK:(r + 1) * MOBA_BLOCK]
                    km_ref[0, r:r + 1, kc:kc + LANES] = jnp.sum(blk, axis=0, keepdims=True) * (1.0 / MOBA_BLOCK)

    vt = _nt(wvt_ref[...], xn)
    if has_bias:
        vt = vt + bv_ref[...]
    nv = vt.shape[0]
    for r in range(rows // kv_block):
        blk = vt[:, r * kv_block:(r + 1) * kv_block].astype(vt_ref.dtype)
        if v_group is None:
            vt_ref[0, r] = blk
        else:
            stride = v_group + BF16_ROWS
            for g in range(nv // v_group):
                vt_ref[0, r, g * stride:g * stride + v_group] = blk[g * v_group:(g + 1) * v_group]
                vt_ref[0, r, g * stride + v_group:(g + 1) * stride] = jnp.ones((BF16_ROWS, kv_block), vt_ref.dtype)


def _project(h, g, wqk, wvt, tables, *, batch, nq, kv_block, q_scale, v_group=None, q_dtype=BF16, bias=None,
             with_kmean=False):
    t = h.shape[0]
    rows = PROJ_ROWS
    nk = wqk.shape[1] - nq
    nv = wvt.shape[0]
    nv_out = nv if v_group is None else nv // v_group * (v_group + BF16_ROWS)
    seq = t // batch
    tiles_per_seq = seq // rows
    chunks = rows // kv_block

    const = lambda i: (0, 0)
    row = lambda i: (i, 0)
    in_specs = [pl.BlockSpec((rows, D_MODEL), row), pl.BlockSpec((1, D_MODEL), const),
                pl.BlockSpec(wqk.shape, const), pl.BlockSpec(wvt.shape, const)]
    args = [h, g.reshape(1, D_MODEL), wqk, wvt]
    if bias is not None:
        in_specs += [pl.BlockSpec((1, nq + nk), const), pl.BlockSpec((nv, 1), const)]
        args += [bias[0].reshape(1, nq + nk), bias[1].reshape(nv, 1)]
    in_specs += [pl.BlockSpec((rows, LANES), row)] * 3
    args += list(tables)

    out_shape = [jax.ShapeDtypeStruct((t, nq), q_dtype), jax.ShapeDtypeStruct((t, nk), BF16),
                 jax.ShapeDtypeStruct((batch, seq // kv_block, nv_out, kv_block), BF16)]
    out_specs = [pl.BlockSpec((rows, nq), row), pl.BlockSpec((rows, nk), row),
                 pl.BlockSpec((1, chunks, nv_out, kv_block),
                              lambda i: (i // tiles_per_seq, i % tiles_per_seq, 0, 0))]
    if with_kmean:
        out_shape.append(jax.ShapeDtypeStruct((t // rows, rows // MOBA_BLOCK, nk), F32))
        out_specs.append(pl.BlockSpec((1, rows // MOBA_BLOCK, nk), lambda i: (i, 0, 0)))

    return pl.pallas_call(
        functools.partial(_proj_kernel, nq=nq, nk=nk, kv_block=kv_block, q_scale=q_scale, v_group=v_group,
                          has_bias=bias is not None, with_kmean=with_kmean),
        grid=(t // rows,),
        in_specs=in_specs, out_specs=out_specs, out_shape=out_shape,
        compiler_params=pltpu.CompilerParams(dimension_semantics=("arbitrary",), vmem_limit_bytes=VMEM_LIMIT),
    )(*args)


def _transposed_queries(q_ref):
    return [q_ref[:, hh * LANES:(hh + 1) * LANES].astype(F32).T for hh in range(HEADS_PER_STEP)]


def _prepare_queries(qts, qx_ref, spare_rows=None):
    row = lax.broadcasted_iota(jnp.int32, (LANES, 1), 0)
    for hh, qt in enumerate(qts):
        for c in range(2):
            qx = jnp.where(row < HEAD_DIM if c == 0 else row >= HEAD_DIM, qt, 0.0)
            if spare_rows is not None:
                qx = qx + spare_rows(2 * hh + c)
            qx_ref[2 * hh + c] = qx.astype(BF16)


def _scores_stage(qx_ref, k_ref, s_ref, tmax_ref, tile, keep_fn, k_spare, diagonal=False):
    row0 = pl.multiple_of(tile * ATTN_BLOCK, ATTN_BLOCK)
    spare = k_spare()
    for hh in range(HEADS_PER_STEP):
        k_head = k_ref[pl.ds(row0, ATTN_BLOCK), hh * LANES:(hh + 1) * LANES]
        for c, mi in enumerate((2 * hh, 2 * hh + 1)):
            k = k_head
            if spare is not None:
                lane = lax.broadcasted_iota(jnp.int32, (1, LANES), 1)
                k = jnp.where(lane < HEAD_DIM if c == 0 else lane >= HEAD_DIM, k_head, spare[c])
            if diagonal:
                s = jnp.concatenate([
                    jnp.concatenate(
                        [jnp.full((KEY_CHUNK, c * KEY_CHUNK), NEG_INF, F32)] * (c > 0)
                        + [_nn(k[c * KEY_CHUNK:(c + 1) * KEY_CHUNK], qx_ref[mi, :, c * KEY_CHUNK:])], axis=1)
                    for c in range(KEY_CHUNKS)], axis=0)
            else:
                s = _nn(k, qx_ref[mi])
            keep = keep_fn(mi)
            if keep is not None:
                s = jnp.where(keep, s, NEG_INF)
            s_ref[mi] = s
            tmax_ref[mi] = jnp.max(s, axis=0, keepdims=True)


def _softmax_stage(s_ref, tmax_ref, m_ref, alpha_ref, p_ref):
    for mi in range(2 * HEADS_PER_STEP):
        m_old = m_ref[mi]
        m_new = jnp.maximum(m_old, tmax_ref[mi])
        alpha_ref[mi] = jnp.exp2(m_old - m_new)
        m_ref[mi] = m_new
        p_ref[mi] = jnp.exp2(s_ref[mi] - m_new).astype(BF16)


def _values_stage(vt_of, tile, alpha_ref, p_ref, acc_ref):
    for mi in range(2 * HEADS_PER_STEP):
        vt = jnp.concatenate([vt_of(mi, tile * KEY_CHUNKS + ch) for ch in range(KEY_CHUNKS)], axis=1)
        acc_ref[mi] = alpha_ref[mi] * acc_ref[mi] + _nn(vt, p_ref[mi])


def _no_extra(*_):
    return None


def _query_tile(sub):
    return pl.program_id(2) * Q_TILES_PER_STEP + sub, slice(sub * ATTN_BLOCK, (sub + 1) * ATTN_BLOCK)


def _flash_pipeline(i, qx_ref, k_ref, vt_of, s_ref, tmax_ref, m_ref, alpha_ref, p_ref, acc_ref, diag_keep,
                    past_keep=_no_extra, diag_extra=_no_extra, past_extra=_no_extra, first_in_step=True):
    m_ref[...] = jnp.full(m_ref.shape, M_INIT, F32)

    if first_in_step:
        @pl.when((pl.program_id(0) == 0) & (pl.program_id(1) == 0) & (pl.program_id(2) == 0))
        def _init():
            acc_ref[...] = jnp.zeros(acc_ref.shape, F32)

    def tile_at(step):
        return jnp.where(step == 0, i, step - 1)

    def values(step):
        _values_stage(vt_of, tile_at(step), alpha_ref, p_ref, acc_ref)

    def softmax():
        _softmax_stage(s_ref, tmax_ref, m_ref, alpha_ref, p_ref)

    def scores(tile, keep_fn, k_extra, diagonal=False):
        _scores_stage(qx_ref, k_ref, s_ref, tmax_ref, tile, keep_fn, k_extra, diagonal)

    scores(i, diag_keep, diag_extra, diagonal=True)

    def body(t, carry):
        softmax()
        scores(t, functools.partial(past_keep, t), functools.partial(past_extra, t))
        values(t)
        return carry

    lax.fori_loop(0, i, body, 0)
    softmax()
    values(i)


def _tile_iotas():
    shape = (ATTN_BLOCK, ATTN_BLOCK)
    return lax.broadcasted_iota(jnp.int32, shape, 0), lax.broadcasted_iota(jnp.int32, shape, 1)


def _flash_scratch(acc_rows):
    maps = 2 * HEADS_PER_STEP
    blk = ATTN_BLOCK
    stat = pltpu.VMEM((maps, 1, blk), F32)
    return [pltpu.VMEM((maps, LANES, blk), BF16), pltpu.VMEM((maps, blk, blk), F32), stat, stat, stat,
            pltpu.VMEM((maps, blk, blk), BF16), pltpu.VMEM((maps, acc_rows, blk), F32)]


def _flash_specs(seq, nqb, vt_rows):
    width = LANES * HEADS_PER_STEP
    q_spec = pl.BlockSpec((Q_TILES_PER_STEP * ATTN_BLOCK, width), lambda b, h, i: (b * nqb + i, h))
    k_spec = pl.BlockSpec((seq, width), lambda b, h, i: (b, h))
    vt_spec = pl.BlockSpec((1, seq // KEY_CHUNK, vt_rows * HEADS_PER_STEP, KEY_CHUNK), lambda b, h, i: (b, 0, h, 0),
                           pipeline_mode=pl.Buffered(1))
    return q_spec, k_spec, vt_spec


def _diff_attn_kernel(lq1_ref, lk1_ref, lq2_ref, lk2_ref, subg_ref, q_ref, k_ref, vt_ref, o_ref,
                      qx_ref, s_ref, tmax_ref, m_ref, alpha_ref, p_ref, acc_ref, *, lambda_init):
    rows = LANES + BF16_ROWS

    def vt_of(mi, chunk):
        return vt_ref[0, chunk, (mi // 2) * rows:(mi // 2 + 1) * rows]

    def diag_keep(mi):
        key, qry = _tile_iotas()
        return key <= qry

    lam = (jnp.exp(jnp.sum(lq1_ref[...] * lk1_ref[...], axis=1, keepdims=True))
           - jnp.exp(jnp.sum(lq2_ref[...] * lk2_ref[...], axis=1, keepdims=True)) + lambda_init)
    for sub in range(Q_TILES_PER_STEP):
        i, q_rows = _query_tile(sub)
        _prepare_queries(_transposed_queries(q_ref.at[q_rows]), qx_ref)
        _flash_pipeline(i, qx_ref, k_ref, vt_of, s_ref, tmax_ref, m_ref, alpha_ref, p_ref, acc_ref, diag_keep,
                        first_in_step=sub == 0)
        for hh in range(HEADS_PER_STEP):
            a1, a2 = acc_ref[2 * hh], acc_ref[2 * hh + 1]
            o = a1[:LANES] / a1[LANES:LANES + 1] - lam * (a2[:LANES] / a2[LANES:LANES + 1])
            o = o * lax.rsqrt(jnp.mean(o * o, axis=0, keepdims=True) + NORM_EPS) * subg_ref[...]
            o_ref[q_rows, hh * LANES:(hh + 1) * LANES] = (o * (1.0 - lambda_init)).T.astype(o_ref.dtype)


def _diff_attention(q, k, vt, lam_params, sub_g, *, batch, lambda_init):
    t = q.shape[0]
    seq = t // batch
    nqb = seq // (Q_TILES_PER_STEP * ATTN_BLOCK)
    rows = LANES + BF16_ROWS
    small = pl.BlockSpec((1, HEAD_DIM), lambda b, h, i: (0, 0))
    q_spec, k_spec, vt_spec = _flash_specs(seq, nqb, rows)
    return pl.pallas_call(
        functools.partial(_diff_attn_kernel, lambda_init=lambda_init),
        grid=(batch, PAIRS // HEADS_PER_STEP, nqb),
        in_specs=[small, small, small, small, pl.BlockSpec((LANES, 1), lambda b, h, i: (0, 0)),
                  q_spec, k_spec, vt_spec],
        out_specs=q_spec,
        out_shape=jax.ShapeDtypeStruct((t, D_MODEL), BF16),
        scratch_shapes=_flash_scratch(rows),
        compiler_params=pltpu.CompilerParams(dimension_semantics=("arbitrary",) * 3, vmem_limit_bytes=VMEM_LIMIT),
    )(*[p.reshape(1, HEAD_DIM) for p in lam_params], sub_g.reshape(LANES, 1), q, k, vt)


def _moba_select(gate, own):
    nb, width = gate.shape
    groups = range(nb // SUBLANES)
    sub = lax.broadcasted_iota(jnp.int32, (SUBLANES, width), 0)
    ids = [sub + SUBLANES * r for r in groups]
    g = [jnp.where(ids[r] < own, gate[SUBLANES * r:SUBLANES * (r + 1)], NEG_INF) for r in groups]
    sel = [jnp.zeros((SUBLANES, width), F32) for _ in groups]

    def over_blocks(parts, op):
        x = functools.reduce(op, parts)
        for shift in (4, 2, 1):
            x = op(x, pltpu.roll(x, shift, 0))
        return x

    for t in range(MOBA_TOPK):
        best = over_blocks(g, jnp.maximum)
        first = over_blocks([jnp.where(g[r] == best, ids[r], nb) for r in groups], jnp.minimum)
        valid = jnp.where(t < own, 1.0, 0.0)
        for r in groups:
            hit = ids[r] == first
            sel[r] = jnp.maximum(sel[r], jnp.where(hit, valid, 0.0))
            g[r] = jnp.where(hit, NEG_INF, g[r])
    return jnp.concatenate(sel, axis=0)


def _moba_attn_kernel(q_ref, k_ref, vt_ref, km_ref, o_ref,
                      sel_ref, qx_ref, s_ref, tmax_ref, m_ref, alpha_ref, p_ref, acc_ref):
    for sub in range(Q_TILES_PER_STEP):
        _moba_query_tile(sub, q_ref, k_ref, vt_ref, km_ref, o_ref,
                         sel_ref, qx_ref, s_ref, tmax_ref, m_ref, alpha_ref, p_ref, acc_ref)


def _moba_query_tile(sub, q_ref, k_ref, vt_ref, km_ref, o_ref,
                     sel_ref, qx_ref, s_ref, tmax_ref, m_ref, alpha_ref, p_ref, acc_ref):
    i, q_rows = _query_tile(sub)
    half = HEAD_DIM + BF16_ROWS
    maps = 2 * HEADS_PER_STEP
    nb = sel_ref.shape[1]
    qts = _transposed_queries(q_ref.at[q_rows])
    gates = []
    for hh, qt in enumerate(qts):
        q_hi = qt.astype(BF16)
        q_lo = (qt - q_hi.astype(F32)).astype(BF16)
        for km in _lo_hi(km_ref[0, :, hh * LANES:(hh + 1) * LANES]):
            km_hi = km.astype(BF16)
            km_lo = (km - km_hi.astype(F32)).astype(BF16)
            gates.append(_nn(km_hi, q_hi) + _nn(km_hi, q_lo) + _nn(km_lo, q_hi))
    own = i * KEY_CHUNKS + (lax.broadcasted_iota(jnp.int32, (1, maps * ATTN_BLOCK), 1) % ATTN_BLOCK) // MOBA_BLOCK
    sel = _moba_select(jnp.concatenate(gates, axis=1), own)
    for mi in range(maps):
        sel_ref[mi] = sel[:, mi * ATTN_BLOCK:(mi + 1) * ATTN_BLOCK]

    def vt_of(mi, chunk):
        return vt_ref[0, chunk, mi * half:(mi + 1) * half]

    def selected(block, mi):
        return sel_ref[mi, pl.ds(block, 1), :] > 0.0

    def diag_keep(mi):
        key, qry = _tile_iotas()
        key_blk, qry_blk = key // MOBA_BLOCK, qry // MOBA_BLOCK
        keep = (key <= qry) & (key_blk == qry_blk)
        for ch in range(KEY_CHUNKS - 1):
            keep = keep | ((key_blk == ch) & (qry_blk > ch) & selected(i * KEY_CHUNKS + ch, mi))
        return keep

    def spare_base(c):
        return HEAD_DIM * (1 - c)

    def bias_rows(mi):
        rows = jnp.where(sel_ref[mi] > 0.0, 0.0, MASK_BIAS)
        above = spare_base(mi % 2)
        return jnp.concatenate([jnp.zeros((above, ATTN_BLOCK), F32)] * (above > 0) + [rows]
                               + [jnp.zeros((LANES - above - nb, ATTN_BLOCK), F32)], axis=0)

    def block_marker(t):
        lane = lax.broadcasted_iota(jnp.int32, (MOBA_BLOCK, LANES), 1)
        return [jnp.concatenate([jnp.where(lane == spare_base(c) + t * KEY_CHUNKS + ch, 1.0, 0.0).astype(BF16)
                                 for ch in range(KEY_CHUNKS)], axis=0) for c in range(2)]

    def no_marker():
        return [jnp.zeros((ATTN_BLOCK, LANES), BF16)] * 2

    _prepare_queries(qts, qx_ref, bias_rows)
    _flash_pipeline(i, qx_ref, k_ref, vt_of, s_ref, tmax_ref, m_ref, alpha_ref, p_ref, acc_ref, diag_keep,
                    diag_extra=no_marker, past_extra=block_marker, first_in_step=sub == 0)

    for hh in range(HEADS_PER_STEP):
        a1, a2 = acc_ref[2 * hh], acc_ref[2 * hh + 1]
        o = jnp.concatenate([a1[:HEAD_DIM] / a1[HEAD_DIM:HEAD_DIM + 1], a2[:HEAD_DIM] / a2[HEAD_DIM:HEAD_DIM + 1]],
                            axis=0)
        o_ref[q_rows, hh * LANES:(hh + 1) * LANES] = o.T.astype(o_ref.dtype)


def _moba_attention(q, k, vt, kmean, *, batch):
    t = q.shape[0]
    seq = t // batch
    nqb = seq // (Q_TILES_PER_STEP * ATTN_BLOCK)
    nb = seq // MOBA_BLOCK
    half = HEAD_DIM + BF16_ROWS
    q_spec, k_spec, vt_spec = _flash_specs(seq, nqb, 2 * half)
    return pl.pallas_call(
        _moba_attn_kernel,
        grid=(batch, PAIRS // HEADS_PER_STEP, nqb),
        in_specs=[q_spec, k_spec, vt_spec,
                  pl.BlockSpec((1, nb, LANES * HEADS_PER_STEP), lambda b, h, i: (b, 0, h))],
        out_specs=q_spec,
        out_shape=jax.ShapeDtypeStruct((t, D_MODEL), BF16),
        scratch_shapes=[pltpu.VMEM((2 * HEADS_PER_STEP, nb, ATTN_BLOCK), F32)] + _flash_scratch(half),
        compiler_params=pltpu.CompilerParams(dimension_semantics=("arbitrary",) * 3, vmem_limit_bytes=VMEM_LIMIT),
    )(q, k, vt, kmean)


def _swa_attn_kernel(sink_ref, q_ref, k_ref, vt_ref, o_ref):
    h = pl.program_id(1)
    i = pl.program_id(2)
    sub, w = SWA_SUB_BLOCK, SWA_WINDOW
    subs = SWA_Q_BLOCK // sub
    key = lax.broadcasted_iota(jnp.int32, (sub, sub), 0)
    qry = lax.broadcasted_iota(jnp.int32, (sub, sub), 1)
    own_keep = (key <= qry) & (qry - key < w)
    pkey = lax.broadcasted_iota(jnp.int32, (w, sub), 0)
    pqry = lax.broadcasted_iota(jnp.int32, (w, sub), 1)
    rows = HEAD_DIM + BF16_ROWS
    row0 = pl.multiple_of((h // (PAIRS // SWA_KV_HEADS)) * rows, BF16_ROWS)

    sinks = [sink_ref[2 * h + half] * LOG2_E for half in range(2)]
    chains = []
    for sb in range(subs):
        blk = i * subs + sb
        q = q_ref[sb * sub:(sb + 1) * sub, :]
        q0 = pl.multiple_of(blk * sub, sub)
        prev0 = pl.multiple_of(jnp.maximum(q0 - w, 0), w)
        k_own = _lo_hi(k_ref[pl.ds(q0, sub), :])
        k_prev = _lo_hi(k_ref[pl.ds(prev0, w), :])
        prev_keep = pkey > pqry
        if sb == 0:
            prev_keep = pkey > pqry + jnp.where(i > 0, 0, w)
        for half in range(2):
            s_own = jnp.where(own_keep, _nt(k_own[half], q), NEG_INF)
            s_prev = jnp.where(prev_keep, _nt(k_prev[half], q), NEG_INF)
            m = jnp.maximum(jnp.maximum(jnp.max(s_own, axis=0, keepdims=True),
                                        jnp.max(s_prev, axis=0, keepdims=True)), sinks[half])
            chains.append((s_own, s_prev, m))

    probs = [(jnp.exp2(s_own - m).astype(BF16), jnp.exp2(s_prev - m).astype(BF16)) for s_own, s_prev, m in chains]

    for sb in range(subs):
        chunk0 = (i * subs + sb) * (sub // w)
        vt_prev = vt_ref[0, jnp.maximum(chunk0 - 1, 0), pl.ds(row0, rows), :]
        vt_own = jnp.concatenate([vt_ref[0, chunk0 + c, pl.ds(row0, rows), :] for c in range(sub // w)], axis=1)
        outs = []
        for half in range(2):
            p_own, p_prev = probs[2 * sb + half]
            acc = _nn(vt_own, p_own) + _nn(vt_prev, p_prev)
            denom = acc[HEAD_DIM:HEAD_DIM + 1] + jnp.exp2(sinks[half] - chains[2 * sb + half][2])
            outs.append(acc[:HEAD_DIM] / denom)
        o_ref[sb * sub:(sb + 1) * sub, :] = jnp.concatenate(outs, axis=0).T.astype(o_ref.dtype)


def _swa_attention(q, kdup, vt, sinks, *, batch):
    t = q.shape[0]
    seq = t // batch
    qb = SWA_Q_BLOCK
    nqb = seq // qb
    per_kv = PAIRS // SWA_KV_HEADS
    vt_rows = SWA_KV_HEADS * (HEAD_DIM + BF16_ROWS)
    return pl.pallas_call(
        _swa_attn_kernel,
        grid=(batch, PAIRS, nqb),
        in_specs=[pl.BlockSpec(memory_space=pltpu.SMEM),
                  pl.BlockSpec((qb, LANES), lambda b, h, i: (b * nqb + i, h)),
                  pl.BlockSpec((seq, LANES), lambda b, h, i: (b, h // per_kv)),
                  pl.BlockSpec((1, seq // SWA_WINDOW, vt_rows, SWA_WINDOW), lambda b, h, i: (b, 0, 0, 0))],
        out_specs=pl.BlockSpec((qb, LANES), lambda b, h, i: (b * nqb + i, h)),
        out_shape=jax.ShapeDtypeStruct((t, D_MODEL), BF16),
        compiler_params=pltpu.CompilerParams(dimension_semantics=("arbitrary",) * 3, vmem_limit_bytes=VMEM_LIMIT),
    )(sinks, q, kdup, vt)


def _post_kernel(*refs, final):
    h_ref, o_ref, wout_ref, g_ref, wup_ref, wdn_ref = refs[:6]
    fg_ref = refs[6] if final else None
    out_ref = refs[-1]
    h1 = h_ref[...] + _nn(o_ref[...], wout_ref[...])
    xn = _rms(h1, g_ref[...]).astype(BF16)
    acc = h1
    for c in range(D_FF // FF_CHUNK):
        lo = c * FF_CHUNK
        a = jnp.maximum(_nn(xn, wup_ref[:, lo:lo + FF_CHUNK]), 0.0)
        acc = acc + _nn((a * a).astype(BF16), wdn_ref[lo:lo + FF_CHUNK, :])
    if final:
        acc = _rms(acc, fg_ref[...])
    out_ref[...] = acc


def _post(h, o, w_out, g, w_up, w_down, layer, final_g=None):
    t = h.shape[0]
    rows = PROJ_ROWS
    const = lambda i: (0, 0)
    row = lambda i: (i, 0)
    pick = lambda i: (layer, 0, 0)
    in_specs = [pl.BlockSpec((rows, D_MODEL), row), pl.BlockSpec((rows, D_MODEL), row),
                pl.BlockSpec(w_out.shape, const), pl.BlockSpec((1, D_MODEL), const),
                pl.BlockSpec((None,) + w_up.shape[1:], pick), pl.BlockSpec((None,) + w_down.shape[1:], pick)]
    args = [h, o, w_out, g.reshape(1, D_MODEL), w_up, w_down]
    if final_g is not None:
        in_specs.append(pl.BlockSpec((1, D_MODEL), const))
        args.append(final_g.reshape(1, D_MODEL))
    return pl.pallas_call(
        functools.partial(_post_kernel, final=final_g is not None),
        grid=(t // rows,),
        in_specs=in_specs,
        out_specs=pl.BlockSpec((rows, D_MODEL), row),
        out_shape=jax.ShapeDtypeStruct((t, D_MODEL), F32),
        compiler_params=pltpu.CompilerParams(dimension_semantics=("arbitrary",), vmem_limit_bytes=VMEM_LIMIT),
    )(*args)


def kernel(x, positions, attn_norm_g, mlp_norm_g, diff_w_in, diff_w_out, diff_lam_q1, diff_lam_k1, diff_lam_q2,
           diff_lam_k2, diff_subln_g, moba_w_in, moba_w_out, swa_w_in, swa_b_in, swa_sinks, swa_w_out, mlp_w_up,
           mlp_w_down, final_norm_g):
    batch, seq, _ = x.shape
    depth = attn_norm_g.shape[0]
    tables = _rope_tables(positions)
    h = x.reshape(batch * seq, D_MODEL)
    qk = 2 * D_MODEL
    log2_scale = Q_SCALE * LOG2_E
    w_up_all, w_down_all = mlp_w_up.astype(BF16), mlp_w_down.astype(BF16)

    for i in range(depth):
        mixer, slot = i % N_MIXERS, i // N_MIXERS
        if mixer == 0:
            w_in = diff_w_in[slot]
            q, k, vt = _project(h, attn_norm_g[i], w_in[:, :qk].astype(BF16), w_in[:, qk:].T.astype(BF16), tables,
                                batch=batch, nq=D_MODEL, kv_block=KEY_CHUNK, q_scale=log2_scale, v_group=LANES)
            o = _diff_attention(q, k, vt,
                                (diff_lam_q1[slot], diff_lam_k1[slot], diff_lam_q2[slot], diff_lam_k2[slot]),
                                diff_subln_g[slot], batch=batch, lambda_init=0.8 - 0.6 * math.exp(-0.3 * i))
            w_out = diff_w_out[slot]
        elif mixer == 1:
            w_in = moba_w_in[slot]
            q, k, vt, kmean = _project(h, attn_norm_g[i], w_in[:, :qk].astype(BF16), w_in[:, qk:].T.astype(BF16),
                                       tables, batch=batch, nq=D_MODEL, kv_block=KEY_CHUNK, q_scale=log2_scale,
                                       v_group=HEAD_DIM, q_dtype=F32, with_kmean=True)
            o = _moba_attention(q, k, vt, kmean.reshape(batch, seq // MOBA_BLOCK, D_MODEL), batch=batch)
            w_out = moba_w_out[slot]
        else:
            w_in, b_in = swa_w_in[slot], swa_b_in[slot]
            dup = lambda a: jnp.concatenate(
                [a[..., D_MODEL + kv * HEAD_DIM:D_MODEL + (kv + 1) * HEAD_DIM]
                 for kv in range(SWA_KV_HEADS) for _ in range(2)], axis=-1)
            v0 = D_MODEL + SWA_KV_HEADS * HEAD_DIM
            wqk = jnp.concatenate([w_in[:, :D_MODEL], dup(w_in)], axis=1).astype(BF16)
            bqk = jnp.concatenate([b_in[:D_MODEL], dup(b_in)])
            q, k, vt = _project(h, attn_norm_g[i], wqk, w_in[:, v0:].T.astype(BF16), tables, batch=batch,
                                nq=D_MODEL, kv_block=SWA_WINDOW, q_scale=log2_scale, v_group=HEAD_DIM,
                                bias=(bqk, b_in[v0:]))
            o = _swa_attention(q, k, vt, swa_sinks[slot], batch=batch)
            w_out = swa_w_out[slot]
        h = _post(h, o, w_out.astype(BF16), mlp_norm_g[i], w_up_all, w_down_all, i,
                  final_norm_g if i == depth - 1 else None)
    return h.reshape(batch, seq, D_MODEL)
```

```python
import functools
import math

import jax
import jax.numpy as jnp
from jax import lax
from jax.experimental import pallas as pl
from jax.experimental.pallas import tpu as pltpu

F32 = jnp.float32
BF16 = jnp.bfloat16

D_MODEL = 1024
HEAD_DIM = 64
ROT_DIM = HEAD_DIM // 4
ROT_HALF = ROT_DIM // 2
ROPE_THETA = 500000.0
NORM_EPS = 1e-6
D_FF = 4 * D_MODEL
Q_SCALE = HEAD_DIM ** -0.5
LOG2_E = math.log2(math.e)
N_MIXERS = 3

LANES = 128
SUBLANES = 8
BF16_ROWS = 16
MXU_COLS = 256
PAIRS = D_MODEL // LANES
MOBA_BLOCK = 256
MOBA_TOPK = 3
SWA_WINDOW = 128
SWA_KV_HEADS = 2

ATTN_BLOCK = 512
KEY_CHUNK = 256
KEY_CHUNKS = ATTN_BLOCK // KEY_CHUNK
Q_TILES_PER_STEP = 4
HEADS_PER_STEP = 4
SWA_Q_BLOCK = 1024
SWA_SUB_BLOCK = 256
PROJ_ROWS = 1024
POST_ROWS = 1024
FF_CHUNK = 1024
VMEM_LIMIT = 56 * 1024 * 1024

NEG_INF = float("-inf")
M_INIT = -1e30
MASK_BIAS = -1e30


def _nt(a, b, precision=None):
    return lax.dot_general(a, b, (((1,), (1,)), ((), ())), preferred_element_type=F32, precision=precision)


def _nn(a, b):
    return jnp.dot(a, b, preferred_element_type=F32)


def _rms(x, g):
    return x * lax.rsqrt(jnp.mean(x * x, axis=-1, keepdims=True) + NORM_EPS) * g


def _lo_hi(x):
    lane = lax.broadcasted_iota(jnp.int32, (1, LANES), 1)
    zero = jnp.zeros_like(x)
    return jnp.where(lane < HEAD_DIM, x, zero), jnp.where(lane < HEAD_DIM, zero, x)


def _rope_table_kernel(pos_ref, inv_ref, c_ref, sa_ref, sb_ref):
    ang = pos_ref[...].astype(F32) * inv_ref[...]
    lane = lax.broadcasted_iota(jnp.int32, ang.shape, 1) % HEAD_DIM
    cos = jnp.cos(ang)
    sin = jnp.sin(ang)
    c_ref[...] = jnp.where(lane < ROT_DIM, cos, 1.0)
    sa_ref[...] = jnp.where(lane < ROT_HALF, -sin, 0.0)
    sb_ref[...] = jnp.where((lane >= ROT_HALF) & (lane < ROT_DIM), sin, 0.0)


def _rope_tables(positions):
    t = positions.size
    rows = 1024
    inv = ROPE_THETA ** (-jnp.arange(0, ROT_DIM, 2, dtype=F32) / ROT_DIM)
    lane = jnp.arange(LANES) % HEAD_DIM
    inv_lane = jnp.where(lane < ROT_DIM, inv[lane % ROT_HALF], 0.0).reshape(1, LANES)
    tab = jax.ShapeDtypeStruct((t, LANES), F32)
    spec = pl.BlockSpec((rows, LANES), lambda i: (i, 0))
    return pl.pallas_call(
        _rope_table_kernel,
        grid=(t // rows,),
        in_specs=[pl.BlockSpec((rows, 1), lambda i: (i, 0)), pl.BlockSpec((1, LANES), lambda i: (0, 0))],
        out_specs=[spec, spec, spec],
        out_shape=[tab, tab, tab],
    )(positions.reshape(t, 1), inv_lane)


def _proj_kernel(*refs, nq, nk, kv_block, q_scale, v_group, has_bias, with_kmean):
    it = iter(refs)
    h_ref, g_ref, wqk_ref, wvt_ref = next(it), next(it), next(it), next(it)
    bqk_ref = next(it) if has_bias else None
    bv_ref = next(it) if has_bias else None
    c_ref, sa_ref, sb_ref = next(it), next(it), next(it)
    q_ref, k_ref, vt_ref = next(it), next(it), next(it)
    km_ref = next(it) if with_kmean else None

    rows = h_ref.shape[0]
    xn = _rms(h_ref[...], g_ref[...]).astype(BF16)
    cos, sin_a, sin_b = c_ref[...], sa_ref[...], sb_ref[...]

    for c in range((nq + nk) // LANES):
        col = c * LANES
        if c % 2 == 0:
            wide = _nn(xn, wqk_ref[:, col:col + MXU_COLS])
            if has_bias:
                wide = wide + bqk_ref[:, col:col + MXU_COLS]
        y = wide[:, (c % 2) * LANES:(c % 2 + 1) * LANES]
        y = y * cos + pltpu.roll(y, LANES - ROT_HALF, 1) * sin_a + pltpu.roll(y, ROT_HALF, 1) * sin_b
        if col < nq:
            q_ref[:, col:col + LANES] = (y * q_scale).astype(q_ref.dtype)
        else:
            kc = col - nq
            k_ref[:, kc:kc + LANES] = y.astype(k_ref.dtype)
            if with_kmean:
                for r in range(rows // MOBA_BLOCK):
                    blk = y[r * MOBA_BLOCK:(r + 1) * MOBA_BLOCK]
                    km_ref[0, r:r + 1, kc:kc + LANES] = jnp.sum(blk, axis=0, keepdims=True) * (1.0 / MOBA_BLOCK)

    vt = _nt(wvt_ref[...], xn)
    if has_bias:
        vt = vt + bv_ref[...]
    nv = vt.shape[0]
    for r in range(rows // kv_block):
        blk = vt[:, r * kv_block:(r + 1) * kv_block].astype(vt_ref.dtype)
        if v_group is None:
            vt_ref[0, r] = blk
        else:
            stride = v_group + BF16_ROWS
            for g in range(nv // v_group):
                vt_ref[0, r, g * stride:g * stride + v_group] = blk[g * v_group:(g + 1) * v_group]
                vt_ref[0, r, g * stride + v_group:(g + 1) * stride] = jnp.ones((BF16_ROWS, kv_block), vt_ref.dtype)


def _project(h, g, wqk, wvt, tables, *, batch, nq, kv_block, q_scale, v_group=None, q_dtype=BF16, bias=None,
             with_kmean=False):
    t = h.shape[0]
    rows = PROJ_ROWS
    nk = wqk.shape[1] - nq
    nv = wvt.shape[0]
    nv_out = nv if v_group is None else nv // v_group * (v_group + BF16_ROWS)
    seq = t // batch
    tiles_per_seq = seq // rows
    chunks = rows // kv_block

    const = lambda i: (0, 0)
    row = lambda i: (i, 0)
    in_specs = [pl.BlockSpec((rows, D_MODEL), row), pl.BlockSpec((1, D_MODEL), const),
                pl.BlockSpec(wqk.shape, const), pl.BlockSpec(wvt.shape, const)]
    args = [h, g.reshape(1, D_MODEL), wqk, wvt]
    if bias is not None:
        in_specs += [pl.BlockSpec((1, nq + nk), const), pl.BlockSpec((nv, 1), const)]
        args += [bias[0].reshape(1, nq + nk), bias[1].reshape(nv, 1)]
    in_specs += [pl.BlockSpec((rows, LANES), row)] * 3
    args += list(tables)

    out_shape = [jax.ShapeDtypeStruct((t, nq), q_dtype), jax.ShapeDtypeStruct((t, nk), BF16),
                 jax.ShapeDtypeStruct((batch, seq // kv_block, nv_out, kv_block), BF16)]
    out_specs = [pl.BlockSpec((rows, nq), row), pl.BlockSpec((rows, nk), row),
                 pl.BlockSpec((1, chunks, nv_out, kv_block),
                              lambda i: (i // tiles_per_seq, i % tiles_per_seq, 0, 0))]
    if with_kmean:
        out_shape.append(jax.ShapeDtypeStruct((t // rows, rows // MOBA_BLOCK, nk), F32))
        out_specs.append(pl.BlockSpec((1, rows // MOBA_BLOCK, nk), lambda i: (i, 0, 0)))

    return pl.pallas_call(
        functools.partial(_proj_kernel, nq=nq, nk=nk, kv_block=kv_block, q_scale=q_scale, v_group=v_group,
                          has_bias=bias is not None, with_kmean=with_kmean),
        grid=(t // rows,),
        in_specs=in_specs, out_specs=out_specs, out_shape=out_shape,
        compiler_params=pltpu.CompilerParams(dimension_semantics=("arbitrary",), vmem_limit_bytes=VMEM_LIMIT),
    )(*args)


def _transposed_queries(q_ref):
    return [q_ref[:, hh * LANES:(hh + 1) * LANES].astype(F32).T for hh in range(HEADS_PER_STEP)]


def _prepare_queries(qts, qx_ref, spare_rows=None):
    row = lax.broadcasted_iota(jnp.int32, (LANES, 1), 0)
    for hh, qt in enumerate(qts):
        for c in range(2):
            qx = jnp.where(row < HEAD_DIM if c == 0 else row >= HEAD_DIM, qt, 0.0)
            if spare_rows is not None:
                qx = qx + spare_rows(2 * hh + c)
            qx_ref[2 * hh + c] = qx.astype(BF16)


def _scores_stage(qx_ref, k_ref, s_ref, tmax_ref, tile, keep_fn, k_spare, diagonal=False):
    row0 = pl.multiple_of(tile * ATTN_BLOCK, ATTN_BLOCK)
    spare = k_spare()
    for hh in range(HEADS_PER_STEP):
        k_head = k_ref[pl.ds(row0, ATTN_BLOCK), hh * LANES:(hh + 1) * LANES]
        for c, mi in enumerate((2 * hh, 2 * hh + 1)):
            k = k_head
            if spare is not None:
                lane = lax.broadcasted_iota(jnp.int32, (1, LANES), 1)
                k = jnp.where(lane < HEAD_DIM if c == 0 else lane >= HEAD_DIM, k_head, spare[c])
            if diagonal:
                s = jnp.concatenate([
                    jnp.concatenate(
                        [jnp.full((KEY_CHUNK, c * KEY_CHUNK), NEG_INF, F32)] * (c > 0)
                        + [_nn(k[c * KEY_CHUNK:(c + 1) * KEY_CHUNK], qx_ref[mi, :, c * KEY_CHUNK:])], axis=1)
                    for c in range(KEY_CHUNKS)], axis=0)
            else:
                s = _nn(k, qx_ref[mi])
            keep = keep_fn(mi)
            if keep is not None:
                s = jnp.where(keep, s, NEG_INF)
            s_ref[mi] = s
            tmax_ref[mi] = jnp.max(s, axis=0, keepdims=True)


def _softmax_stage(s_ref, tmax_ref, m_ref, alpha_ref, p_ref):
    for mi in range(2 * HEADS_PER_STEP):
        m_old = m_ref[mi]
        m_new = jnp.maximum(m_old, tmax_ref[mi])
        alpha_ref[mi] = jnp.exp2(m_old - m_new)
        m_ref[mi] = m_new
        p_ref[mi] = jnp.exp2(s_ref[mi] - m_new).astype(BF16)


def _values_stage(vt_of, tile, alpha_ref, p_ref, acc_ref):
    for mi in range(2 * HEADS_PER_STEP):
        vt = jnp.concatenate([vt_of(mi, tile * KEY_CHUNKS + ch) for ch in range(KEY_CHUNKS)], axis=1)
        acc_ref[mi] = alpha_ref[mi] * acc_ref[mi] + _nn(vt, p_ref[mi])


def _no_extra(*_):
    return None


def _query_tile(sub):
    return pl.program_id(2) * Q_TILES_PER_STEP + sub, slice(sub * ATTN_BLOCK, (sub + 1) * ATTN_BLOCK)


def _flash_pipeline(i, qx_ref, k_ref, vt_of, s_ref, tmax_ref, m_ref, alpha_ref, p_ref, acc_ref, diag_keep,
                    past_keep=_no_extra, diag_extra=_no_extra, past_extra=_no_extra, first_in_step=True):
    m_ref[...] = jnp.full(m_ref.shape, M_INIT, F32)

    if first_in_step:
        @pl.when((pl.program_id(0) == 0) & (pl.program_id(1) == 0) & (pl.program_id(2) == 0))
        def _init():
            acc_ref[...] = jnp.zeros(acc_ref.shape, F32)

    def tile_at(step):
        return jnp.where(step == 0, i, step - 1)

    def values(step):
        _values_stage(vt_of, tile_at(step), alpha_ref, p_ref, acc_ref)

    def softmax():
        _softmax_stage(s_ref, tmax_ref, m_ref, alpha_ref, p_ref)

    def scores(tile, keep_fn, k_extra, diagonal=False):
        _scores_stage(qx_ref, k_ref, s_ref, tmax_ref, tile, keep_fn, k_extra, diagonal)

    scores(i, diag_keep, diag_extra, diagonal=True)

    def body(t, carry):
        softmax()
        scores(t, functools.partial(past_keep, t), functools.partial(past_extra, t))
        values(t)
        return carry

    lax.fori_loop(0, i, body, 0)
    softmax()
    values(i)


def _tile_iotas():
    shape = (ATTN_BLOCK, ATTN_BLOCK)
    return lax.broadcasted_iota(jnp.int32, shape, 0), lax.broadcasted_iota(jnp.int32, shape, 1)


def _flash_scratch(acc_rows):
    maps = 2 * HEADS_PER_STEP
    blk = ATTN_BLOCK
    stat = pltpu.VMEM((maps, 1, blk), F32)
    return [pltpu.VMEM((maps, LANES, blk), BF16), pltpu.VMEM((maps, blk, blk), F32), stat, stat, stat,
            pltpu.VMEM((maps, blk, blk), BF16), pltpu.VMEM((maps, acc_rows, blk), F32)]


def _flash_specs(seq, nqb, vt_rows):
    width = LANES * HEADS_PER_STEP
    q_spec = pl.BlockSpec((Q_TILES_PER_STEP * ATTN_BLOCK, width), lambda b, h, i: (b * nqb + i, h))
    k_spec = pl.BlockSpec((seq, width), lambda b, h, i: (b, h))
    vt_spec = pl.BlockSpec((1, seq // KEY_CHUNK, vt_rows * HEADS_PER_STEP, KEY_CHUNK), lambda b, h, i: (b, 0, h, 0),
                           pipeline_mode=pl.Buffered(1))
    return q_spec, k_spec, vt_spec


def _diff_attn_kernel(lq1_ref, lk1_ref, lq2_ref, lk2_ref, subg_ref, q_ref, k_ref, vt_ref, o_ref,
                      qx_ref, s_ref, tmax_ref, m_ref, alpha_ref, p_ref, acc_ref, *, lambda_init):
    rows = LANES + BF16_ROWS

    def vt_of(mi, chunk):
        return vt_ref[0, chunk, (mi // 2) * rows:(mi // 2 + 1) * rows]

    def diag_keep(mi):
        key, qry = _tile_iotas()
        return key <= qry

    lam = (jnp.exp(jnp.sum(lq1_ref[...] * lk1_ref[...], axis=1, keepdims=True))
           - jnp.exp(jnp.sum(lq2_ref[...] * lk2_ref[...], axis=1, keepdims=True)) + lambda_init)
    for sub in range(Q_TILES_PER_STEP):
        i, q_rows = _query_tile(sub)
        _prepare_queries(_transposed_queries(q_ref.at[q_rows]), qx_ref)
        _flash_pipeline(i, qx_ref, k_ref, vt_of, s_ref, tmax_ref, m_ref, alpha_ref, p_ref, acc_ref, diag_keep,
                        first_in_step=sub == 0)
        for hh in range(HEADS_PER_STEP):
            a1, a2 = acc_ref[2 * hh], acc_ref[2 * hh + 1]
            o = a1[:LANES] / a1[LANES:LANES + 1] - lam * (a2[:LANES] / a2[LANES:LANES + 1])
            o = o * lax.rsqrt(jnp.mean(o * o, axis=0, keepdims=True) + NORM_EPS) * subg_ref[...]
            o_ref[q_rows, hh * LANES:(hh + 1) * LANES] = (o * (1.0 - lambda_init)).T.astype(o_ref.dtype)


def _diff_attention(q, k, vt, lam_params, sub_g, *, batch, lambda_init):
    t = q.shape[0]
    seq = t // batch
    nqb = seq // (Q_TILES_PER_STEP * ATTN_BLOCK)
    rows = LANES + BF16_ROWS
    small = pl.BlockSpec((1, HEAD_DIM), lambda b, h, i: (0, 0))
    q_spec, k_spec, vt_spec = _flash_specs(seq, nqb, rows)
    return pl.pallas_call(
        functools.partial(_diff_attn_kernel, lambda_init=lambda_init),
        grid=(batch, PAIRS // HEADS_PER_STEP, nqb),
        in_specs=[small, small, small, small, pl.BlockSpec((LANES, 1), lambda b, h, i: (0, 0)),
                  q_spec, k_spec, vt_spec],
        out_specs=q_spec,
        out_shape=jax.ShapeDtypeStruct((t, D_MODEL), BF16),
        scratch_shapes=_flash_scratch(rows),
        compiler_params=pltpu.CompilerParams(dimension_semantics=("arbitrary",) * 3, vmem_limit_bytes=VMEM_LIMIT),
    )(*[p.reshape(1, HEAD_DIM) for p in lam_params], sub_g.reshape(LANES, 1), q, k, vt)


def _moba_select(gate, own):
    nb, width = gate.shape
    groups = range(nb // SUBLANES)
    sub = lax.broadcasted_iota(jnp.int32, (SUBLANES, width), 0)
    ids = [sub + SUBLANES * r for r in groups]
    g = [jnp.where(ids[r] < own, gate[SUBLANES * r:SUBLANES * (r + 1)], NEG_INF) for r in groups]
    sel = [jnp.zeros((SUBLANES, width), F32) for _ in groups]

    def over_blocks(parts, op):
        x = functools.reduce(op, parts)
        for shift in (4, 2, 1):
            x = op(x, pltpu.roll(x, shift, 0))
        return x

    for t in range(MOBA_TOPK):
        best = over_blocks(g, jnp.maximum)
        first = over_blocks([jnp.where(g[r] == best, ids[r], nb) for r in groups], jnp.minimum)
        valid = jnp.where(t < own, 1.0, 0.0)
        for r in groups:
            hit = ids[r] == first
            sel[r] = jnp.maximum(sel[r], jnp.where(hit, valid, 0.0))
            g[r] = jnp.where(hit, NEG_INF, g[r])
    return jnp.concatenate(sel, axis=0)


def _moba_attn_kernel(q_ref, k_ref, vt_ref, km_ref, o_ref,
                      sel_ref, qx_ref, s_ref, tmax_ref, m_ref, alpha_ref, p_ref, acc_ref):
    for sub in range(Q_TILES_PER_STEP):
        _moba_query_tile(sub, q_ref, k_ref, vt_ref, km_ref, o_ref,
                         sel_ref, qx_ref, s_ref, tmax_ref, m_ref, alpha_ref, p_ref, acc_ref)


def _moba_query_tile(sub, q_ref, k_ref, vt_ref, km_ref, o_ref,
                     sel_ref, qx_ref, s_ref, tmax_ref, m_ref, alpha_ref, p_ref, acc_ref):
    i, q_rows = _query_tile(sub)
    half = HEAD_DIM + BF16_ROWS
    maps = 2 * HEADS_PER_STEP
    nb = sel_ref.shape[1]
    qts = _transposed_queries(q_ref.at[q_rows])
    gates = []
    for hh, qt in enumerate(qts):
        q_hi = qt.astype(BF16)
        q_lo = (qt - q_hi.astype(F32)).astype(BF16)
        for km in _lo_hi(km_ref[0, :, hh * LANES:(hh + 1) * LANES]):
            km_hi = km.astype(BF16)
            km_lo = (km - km_hi.astype(F32)).astype(BF16)
            gates.append(_nn(km_hi, q_hi) + _nn(km_hi, q_lo) + _nn(km_lo, q_hi))
    own = i * KEY_CHUNKS + (lax.broadcasted_iota(jnp.int32, (1, maps * ATTN_BLOCK), 1) % ATTN_BLOCK) // MOBA_BLOCK
    sel = _moba_select(jnp.concatenate(gates, axis=1), own)
    for mi in range(maps):
        sel_ref[mi] = sel[:, mi * ATTN_BLOCK:(mi + 1) * ATTN_BLOCK]

    def vt_of(mi, chunk):
        return vt_ref[0, chunk, mi * half:(mi + 1) * half]

    def selected(block, mi):
        return sel_ref[mi, pl.ds(block, 1), :] > 0.0

    def diag_keep(mi):
        key, qry = _tile_iotas()
        key_blk, qry_blk = key // MOBA_BLOCK, qry // MOBA_BLOCK
        keep = (key <= qry) & (key_blk == qry_blk)
        for ch in range(KEY_CHUNKS - 1):
            keep = keep | ((key_blk == ch) & (qry_blk > ch) & selected(i * KEY_CHUNKS + ch, mi))
        return keep

    def spare_base(c):
        return HEAD_DIM * (1 - c)

    def bias_rows(mi):
        rows = jnp.where(sel_ref[mi] > 0.0, 0.0, MASK_BIAS)
        above = spare_base(mi % 2)
        return jnp.concatenate([jnp.zeros((above, ATTN_BLOCK), F32)] * (above > 0) + [rows]
                               + [jnp.zeros((LANES - above - nb, ATTN_BLOCK), F32)], axis=0)

    def block_marker(t):
        lane = lax.broadcasted_iota(jnp.int32, (MOBA_BLOCK, LANES), 1)
        return [jnp.concatenate([jnp.where(lane == spare_base(c) + t * KEY_CHUNKS + ch, 1.0, 0.0).astype(BF16)
                                 for ch in range(KEY_CHUNKS)], axis=0) for c in range(2)]

    def no_marker():
        return [jnp.zeros((ATTN_BLOCK, LANES), BF16)] * 2

    _prepare_queries(qts, qx_ref, bias_rows)
    _flash_pipeline(i, qx_ref, k_ref, vt_of, s_ref, tmax_ref, m_ref, alpha_ref, p_ref, acc_ref, diag_keep,
                    diag_extra=no_marker, past_extra=block_marker, first_in_step=sub == 0)

    for hh in range(HEADS_PER_STEP):
        a1, a2 = acc_ref[2 * hh], acc_ref[2 * hh + 1]
        o = jnp.concatenate([a1[:HEAD_DIM] / a1[HEAD_DIM:HEAD_DIM + 1], a2[:HEAD_DIM] / a2[HEAD_DIM:HEAD_DIM + 1]],
                            axis=0)
        o_ref[q_rows, hh * LANES:(hh + 1) * LANES] = o.T.astype(o_ref.dtype)


def _moba_attention(q, k, vt, kmean, *, batch):
    t = q.shape[0]
    seq = t // batch
    nqb = seq // (Q_TILES_PER_STEP * ATTN_BLOCK)
    nb = seq // MOBA_BLOCK
    half = HEAD_DIM + BF16_ROWS
    q_spec, k_spec, vt_spec = _flash_specs(seq, nqb, 2 * half)
    return pl.pallas_call(
        _moba_attn_kernel,
        grid=(batch, PAIRS // HEADS_PER_STEP, nqb),
        in_specs=[q_spec, k_spec, vt_spec,
                  pl.BlockSpec((1, nb, LANES * HEADS_PER_STEP), lambda b, h, i: (b, 0, h))],
        out_specs=q_spec,
        out_shape=jax.ShapeDtypeStruct((t, D_MODEL), BF16),
        scratch_shapes=[pltpu.VMEM((2 * HEADS_PER_STEP, nb, ATTN_BLOCK), F32)] + _flash_scratch(half),
        compiler_params=pltpu.CompilerParams(dimension_semantics=("arbitrary",) * 3, vmem_limit_bytes=VMEM_LIMIT),
    )(q, k, vt, kmean)


def _swa_attn_kernel(sink_ref, q_ref, k_ref, vt_ref, o_ref):
    h = pl.program_id(1)
    i = pl.program_id(2)
    sub, w = SWA_SUB_BLOCK, SWA_WINDOW
    subs = SWA_Q_BLOCK // sub
    key = lax.broadcasted_iota(jnp.int32, (sub, sub), 0)
    qry = lax.broadcasted_iota(jnp.int32, (sub, sub), 1)
    own_keep = (key <= qry) & (qry - key < w)
    pkey = lax.broadcasted_iota(jnp.int32, (w, sub), 0)
    pqry = lax.broadcasted_iota(jnp.int32, (w, sub), 1)
    rows = HEAD_DIM + BF16_ROWS
    row0 = pl.multiple_of((h // (PAIRS // SWA_KV_HEADS)) * rows, BF16_ROWS)

    sinks = [sink_ref[2 * h + half] * LOG2_E for half in range(2)]
    chains = []
    for sb in range(subs):
        blk = i * subs + sb
        q = q_ref[sb * sub:(sb + 1) * sub, :]
        q0 = pl.multiple_of(blk * sub, sub)
        prev0 = pl.multiple_of(jnp.maximum(q0 - w, 0), w)
        k_own = _lo_hi(k_ref[pl.ds(q0, sub), :])
        k_prev = _lo_hi(k_ref[pl.ds(prev0, w), :])
        prev_keep = pkey > pqry
        if sb == 0:
            prev_keep = pkey > pqry + jnp.where(i > 0, 0, w)
        for half in range(2):
            s_own = jnp.where(own_keep, _nt(k_own[half], q), NEG_INF)
            s_prev = jnp.where(prev_keep, _nt(k_prev[half], q), NEG_INF)
            m = jnp.maximum(jnp.maximum(jnp.max(s_own, axis=0, keepdims=True),
                                        jnp.max(s_prev, axis=0, keepdims=True)), sinks[half])
            chains.append((s_own, s_prev, m))

    probs = [(jnp.exp2(s_own - m).astype(BF16), jnp.exp2(s_prev - m).astype(BF16)) for s_own, s_prev, m in chains]

    for sb in range(subs):
        chunk0 = (i * subs + sb) * (sub // w)
        vt_prev = vt_ref[0, jnp.maximum(chunk0 - 1, 0), pl.ds(row0, rows), :]
        vt_own = jnp.concatenate([vt_ref[0, chunk0 + c, pl.ds(row0, rows), :] for c in range(sub // w)], axis=1)
        outs = []
        for half in range(2):
            p_own, p_prev = probs[2 * sb + half]
            acc = _nn(vt_own, p_own) + _nn(vt_prev, p_prev)
            denom = acc[HEAD_DIM:HEAD_DIM + 1] + jnp.exp2(sinks[half] - chains[2 * sb + half][2])
            outs.append(acc[:HEAD_DIM] / denom)
        o_ref[sb * sub:(sb + 1) * sub, :] = jnp.concatenate(outs, axis=0).T.astype(o_ref.dtype)


def _swa_attention(q, kdup, vt, sinks, *, batch):
    t = q.shape[0]
    seq = t // batch
    qb = SWA_Q_BLOCK
    nqb = seq // qb
    per_kv = PAIRS // SWA_KV_HEADS
    vt_rows = SWA_KV_HEADS * (HEAD_DIM + BF16_ROWS)
    return pl.pallas_call(
        _swa_attn_kernel,
        grid=(batch, PAIRS, nqb),
        in_specs=[pl.BlockSpec(memory_space=pltpu.SMEM),
                  pl.BlockSpec((qb, LANES), lambda b, h, i: (b * nqb + i, h)),
                  pl.BlockSpec((seq, LANES), lambda b, h, i: (b, h // per_kv)),
                  pl.BlockSpec((1, seq // SWA_WINDOW, vt_rows, SWA_WINDOW), lambda b, h, i: (b, 0, 0, 0))],
        out_specs=pl.BlockSpec((qb, LANES), lambda b, h, i: (b * nqb + i, h)),
        out_shape=jax.ShapeDtypeStruct((t, D_MODEL), BF16),
        compiler_params=pltpu.CompilerParams(dimension_semantics=("arbitrary",) * 3, vmem_limit_bytes=VMEM_LIMIT),
    )(sinks, q, kdup, vt)


def _post_kernel(*refs, final):
    h_ref, o_ref, wout_ref, g_ref, wup_ref, wdn_ref = refs[:6]
    fg_ref = refs[6] if final else None
    out_ref = refs[-1]
    h1 = h_ref[...] + _nn(o_ref[...], wout_ref[...])
    xn = _rms(h1, g_ref[...]).astype(BF16)
    acc = h1
    for c in range(D_FF // FF_CHUNK):
        lo = c * FF_CHUNK
        a = jnp.maximum(_nn(xn, wup_ref[:, lo:lo + FF_CHUNK]), 0.0)
        acc = acc + _nn((a * a).astype(BF16), wdn_ref[lo:lo + FF_CHUNK, :])
    if final:
        acc = _rms(acc, fg_ref[...])
    out_ref[...] = acc


def _post(h, o, w_out, g, w_up, w_down, layer, final_g=None):
    t = h.shape[0]
    rows = POST_ROWS
    const = lambda i: (0, 0)
    row = lambda i: (i, 0)
    pick = lambda i: (layer, 0, 0)
    once = pl.Buffered(1)
    in_specs = [pl.BlockSpec((rows, D_MODEL), row), pl.BlockSpec((rows, D_MODEL), row),
                pl.BlockSpec(w_out.shape, const, pipeline_mode=once), pl.BlockSpec((1, D_MODEL), const),
                pl.BlockSpec((None,) + w_up.shape[1:], pick, pipeline_mode=once),
                pl.BlockSpec((None,) + w_down.shape[1:], pick, pipeline_mode=once)]
    args = [h, o, w_out, g.reshape(1, D_MODEL), w_up, w_down]
    if final_g is not None:
        in_specs.append(pl.BlockSpec((1, D_MODEL), const))
        args.append(final_g.reshape(1, D_MODEL))
    return pl.pallas_call(
        functools.partial(_post_kernel, final=final_g is not None),
        grid=(t // rows,),
        in_specs=in_specs,
        out_specs=pl.BlockSpec((rows, D_MODEL), row),
        out_shape=jax.ShapeDtypeStruct((t, D_MODEL), F32),
        compiler_params=pltpu.CompilerParams(dimension_semantics=("arbitrary",), vmem_limit_bytes=VMEM_LIMIT),
    )(*args)


def kernel(x, positions, attn_norm_g, mlp_norm_g, diff_w_in, diff_w_out, diff_lam_q1, diff_lam_k1, diff_lam_q2,
           diff_lam_k2, diff_subln_g, moba_w_in, moba_w_out, swa_w_in, swa_b_in, swa_sinks, swa_w_out, mlp_w_up,
           mlp_w_down, final_norm_g):
    batch, seq, _ = x.shape
    depth = attn_norm_g.shape[0]
    tables = _rope_tables(positions)
    h = x.reshape(batch * seq, D_MODEL)
    qk = 2 * D_MODEL
    log2_scale = Q_SCALE * LOG2_E
    w_up_all, w_down_all = mlp_w_up.astype(BF16), mlp_w_down.astype(BF16)

    for i in range(depth):
        mixer, slot = i % N_MIXERS, i // N_MIXERS
        if mixer == 0:
            w_in = diff_w_in[slot]
            q, k, vt = _project(h, attn_norm_g[i], w_in[:, :qk].astype(BF16), w_in[:, qk:].T.astype(BF16), tables,
                                batch=batch, nq=D_MODEL, kv_block=KEY_CHUNK, q_scale=log2_scale, v_group=LANES)
            o = _diff_attention(q, k, vt,
                                (diff_lam_q1[slot], diff_lam_k1[slot], diff_lam_q2[slot], diff_lam_k2[slot]),
                                diff_subln_g[slot], batch=batch, lambda_init=0.8 - 0.6 * math.exp(-0.3 * i))
            w_out = diff_w_out[slot]
        elif mixer == 1:
            w_in = moba_w_in[slot]
            q, k, vt, kmean = _project(h, attn_norm_g[i], w_in[:, :qk].astype(BF16), w_in[:, qk:].T.astype(BF16),
                                       tables, batch=batch, nq=D_MODEL, kv_block=KEY_CHUNK, q_scale=log2_scale,
                                       v_group=HEAD_DIM, q_dtype=F32, with_kmean=True)
            o = _moba_attention(q, k, vt, kmean.reshape(batch, seq // MOBA_BLOCK, D_MODEL), batch=batch)
            w_out = moba_w_out[slot]
        else:
            w_in, b_in = swa_w_in[slot], swa_b_in[slot]
            dup = lambda a: jnp.concatenate(
                [a[..., D_MODEL + kv * HEAD_DIM:D_MODEL + (kv + 1) * HEAD_DIM]
                 for kv in range(SWA_KV_HEADS) for _ in range(2)], axis=-1)
            v0 = D_MODEL + SWA_KV_HEADS * HEAD_DIM
            wqk = jnp.concatenate([w_in[:, :D_MODEL], dup(w_in)], axis=1).astype(BF16)
            bqk = jnp.concatenate([b_in[:D_MODEL], dup(b_in)])
            q, k, vt = _project(h, attn_norm_g[i], wqk, w_in[:, v0:].T.astype(BF16), tables, batch=batch,
                                nq=D_MODEL, kv_block=SWA_WINDOW, q_scale=log2_scale, v_group=HEAD_DIM,
                                bias=(bqk, b_in[v0:]))
            o = _swa_attention(q, k, vt, swa_sinks[slot], batch=batch)
            w_out = swa_w_out[slot]
        h = _post(h, o, w_out.astype(BF16), mlp_norm_g[i], w_up_all, w_down_all, i,
                  final_norm_g if i == depth - 1 else None)
    return h.reshape(batch, seq, D_MODEL)
```

```python
import functools
import math

import jax
import jax.numpy as jnp
from jax import lax
from jax.experimental import pallas as pl
from jax.experimental.pallas import tpu as pltpu

F32 = jnp.float32
BF16 = jnp.bfloat16

D_MODEL = 1024
HEAD_DIM = 64
ROT_DIM = HEAD_DIM // 4
ROT_HALF = ROT_DIM // 2
ROPE_THETA = 500000.0
NORM_EPS = 1e-6
D_FF = 4 * D_MODEL
Q_SCALE = HEAD_DIM ** -0.5
LOG2_E = math.log2(math.e)
N_MIXERS = 3

LANES = 128
SUBLANES = 8
BF16_ROWS = 16
MXU_COLS = 256
PAIRS = D_MODEL // LANES
MOBA_BLOCK = 256
MOBA_TOPK = 3
SWA_WINDOW = 128
SWA_KV_HEADS = 2

ATTN_BLOCK = 512
KEY_CHUNK = 256
KEY_CHUNKS = ATTN_BLOCK // KEY_CHUNK
Q_TILES_PER_STEP = 4
HEADS_PER_STEP = 4
SWA_Q_BLOCK = 8192
SWA_SUB_BLOCK = 256
PROJ_ROWS = 1024
POST_ROWS = 1024
FF_CHUNK = 1024
VMEM_LIMIT = 56 * 1024 * 1024

NEG_INF = float("-inf")
M_INIT = -1e30
MASK_BIAS = -1e30


def _nt(a, b, precision=None):
    return lax.dot_general(a, b, (((1,), (1,)), ((), ())), preferred_element_type=F32, precision=precision)


def _nn(a, b):
    return jnp.dot(a, b, preferred_element_type=F32)


def _rms(x, g):
    return x * lax.rsqrt(jnp.mean(x * x, axis=-1, keepdims=True) + NORM_EPS) * g


def _lo_hi(x):
    lane = lax.broadcasted_iota(jnp.int32, (1, LANES), 1)
    zero = jnp.zeros_like(x)
    return jnp.where(lane < HEAD_DIM, x, zero), jnp.where(lane < HEAD_DIM, zero, x)


def _rope_table_kernel(pos_ref, inv_ref, c_ref, sa_ref, sb_ref):
    ang = pos_ref[...].astype(F32) * inv_ref[...]
    lane = lax.broadcasted_iota(jnp.int32, ang.shape, 1) % HEAD_DIM
    cos = jnp.cos(ang)
    sin = jnp.sin(ang)
    c_ref[...] = jnp.where(lane < ROT_DIM, cos, 1.0)
    sa_ref[...] = jnp.where(lane < ROT_HALF, -sin, 0.0)
    sb_ref[...] = jnp.where((lane >= ROT_HALF) & (lane < ROT_DIM), sin, 0.0)


def _rope_tables(positions):
    t = positions.size
    rows = 1024
    inv = ROPE_THETA ** (-jnp.arange(0, ROT_DIM, 2, dtype=F32) / ROT_DIM)
    lane = jnp.arange(LANES) % HEAD_DIM
    inv_lane = jnp.where(lane < ROT_DIM, inv[lane % ROT_HALF], 0.0).reshape(1, LANES)
    tab = jax.ShapeDtypeStruct((t, LANES), F32)
    spec = pl.BlockSpec((rows, LANES), lambda i: (i, 0))
    return pl.pallas_call(
        _rope_table_kernel,
        grid=(t // rows,),
        in_specs=[pl.BlockSpec((rows, 1), lambda i: (i, 0)), pl.BlockSpec((1, LANES), lambda i: (0, 0))],
        out_specs=[spec, spec, spec],
        out_shape=[tab, tab, tab],
    )(positions.reshape(t, 1), inv_lane)


def _proj_kernel(*refs, nq, nk, kv_block, q_scale, v_group, has_bias, with_kmean):
    it = iter(refs)
    h_ref, g_ref, wqk_ref, wvt_ref = next(it), next(it), next(it), next(it)
    bqk_ref = next(it) if has_bias else None
    bv_ref = next(it) if has_bias else None
    c_ref, sa_ref, sb_ref = next(it), next(it), next(it)
    q_ref, k_ref, vt_ref = next(it), next(it), next(it)
    km_ref = next(it) if with_kmean else None

    rows = h_ref.shape[0]
    xn = _rms(h_ref[...], g_ref[...]).astype(BF16)
    cos, sin_a, sin_b = c_ref[...], sa_ref[...], sb_ref[...]

    for c in range((nq + nk) // LANES):
        col = c * LANES
        if c % 2 == 0:
            wide = _nn(xn, wqk_ref[:, col:col + MXU_COLS])
            if has_bias:
                wide = wide + bqk_ref[:, col:col + MXU_COLS]
        y = wide[:, (c % 2) * LANES:(c % 2 + 1) * LANES]
        y = y * cos + pltpu.roll(y, LANES - ROT_HALF, 1) * sin_a + pltpu.roll(y, ROT_HALF, 1) * sin_b
        if col < nq:
            q_ref[:, col:col + LANES] = (y * q_scale).astype(q_ref.dtype)
        else:
            kc = col - nq
            k_ref[:, kc:kc + LANES] = y.astype(k_ref.dtype)
            if with_kmean:
                for r in range(rows // MOBA_BLOCK):
                    blk = y[r * MOBA_BLOCK:(r + 1) * MOBA_BLOCK]
                    km_ref[0, r:r + 1, kc:kc + LANES] = jnp.sum(blk, axis=0, keepdims=True) * (1.0 / MOBA_BLOCK)

    vt = _nt(wvt_ref[...], xn)
    if has_bias:
        vt = vt + bv_ref[...]
    nv = vt.shape[0]
    for r in range(rows // kv_block):
        blk = vt[:, r * kv_block:(r + 1) * kv_block].astype(vt_ref.dtype)
        if v_group is None:
            vt_ref[0, r] = blk
        else:
            stride = v_group + BF16_ROWS
            for g in range(nv // v_group):
                vt_ref[0, r, g * stride:g * stride + v_group] = blk[g * v_group:(g + 1) * v_group]
                vt_ref[0, r, g * stride + v_group:(g + 1) * stride] = jnp.ones((BF16_ROWS, kv_block), vt_ref.dtype)


def _project(h, g, wqk, wvt, tables, *, batch, nq, kv_block, q_scale, v_group=None, q_dtype=BF16, bias=None,
             with_kmean=False):
    t = h.shape[0]
    rows = PROJ_ROWS
    nk = wqk.shape[1] - nq
    nv = wvt.shape[0]
    nv_out = nv if v_group is None else nv // v_group * (v_group + BF16_ROWS)
    seq = t // batch
    tiles_per_seq = seq // rows
    chunks = rows // kv_block

    const = lambda i: (0, 0)
    row = lambda i: (i, 0)
    in_specs = [pl.BlockSpec((rows, D_MODEL), row), pl.BlockSpec((1, D_MODEL), const),
                pl.BlockSpec(wqk.shape, const), pl.BlockSpec(wvt.shape, const)]
    args = [h, g.reshape(1, D_MODEL), wqk, wvt]
    if bias is not None:
        in_specs += [pl.BlockSpec((1, nq + nk), const), pl.BlockSpec((nv, 1), const)]
        args += [bias[0].reshape(1, nq + nk), bias[1].reshape(nv, 1)]
    in_specs += [pl.BlockSpec((rows, LANES), row)] * 3
    args += list(tables)

    out_shape = [jax.ShapeDtypeStruct((t, nq), q_dtype), jax.ShapeDtypeStruct((t, nk), BF16),
                 jax.ShapeDtypeStruct((batch, seq // kv_block, nv_out, kv_block), BF16)]
    out_specs = [pl.BlockSpec((rows, nq), row), pl.BlockSpec((rows, nk), row),
                 pl.BlockSpec((1, chunks, nv_out, kv_block),
                              lambda i: (i // tiles_per_seq, i % tiles_per_seq, 0, 0))]
    if with_kmean:
        out_shape.append(jax.ShapeDtypeStruct((t // rows, rows // MOBA_BLOCK, nk), F32))
        out_specs.append(pl.BlockSpec((1, rows // MOBA_BLOCK, nk), lambda i: (i, 0, 0)))

    return pl.pallas_call(
        functools.partial(_proj_kernel, nq=nq, nk=nk, kv_block=kv_block, q_scale=q_scale, v_group=v_group,
                          has_bias=bias is not None, with_kmean=with_kmean),
        grid=(t // rows,),
        in_specs=in_specs, out_specs=out_specs, out_shape=out_shape,
        compiler_params=pltpu.CompilerParams(dimension_semantics=("arbitrary",), vmem_limit_bytes=VMEM_LIMIT),
    )(*args)


def _transposed_queries(q_ref):
    return [q_ref[:, hh * LANES:(hh + 1) * LANES].astype(F32).T for hh in range(HEADS_PER_STEP)]


def _prepare_queries(qts, qx_ref, spare_rows=None):
    row = lax.broadcasted_iota(jnp.int32, (LANES, 1), 0)
    for hh, qt in enumerate(qts):
        for c in range(2):
            qx = jnp.where(row < HEAD_DIM if c == 0 else row >= HEAD_DIM, qt, 0.0)
            if spare_rows is not None:
                qx = qx + spare_rows(2 * hh + c)
            qx_ref[2 * hh + c] = qx.astype(BF16)


def _scores_stage(qx_ref, k_ref, s_ref, tmax_ref, tile, keep_fn, k_spare, diagonal=False):
    row0 = pl.multiple_of(tile * ATTN_BLOCK, ATTN_BLOCK)
    spare = k_spare()
    for hh in range(HEADS_PER_STEP):
        k_head = k_ref[pl.ds(row0, ATTN_BLOCK), hh * LANES:(hh + 1) * LANES]
        for c, mi in enumerate((2 * hh, 2 * hh + 1)):
            k = k_head
            if spare is not None:
                lane = lax.broadcasted_iota(jnp.int32, (1, LANES), 1)
                k = jnp.where(lane < HEAD_DIM if c == 0 else lane >= HEAD_DIM, k_head, spare[c])
            if diagonal:
                s = jnp.concatenate([
                    jnp.concatenate(
                        [jnp.full((KEY_CHUNK, c * KEY_CHUNK), NEG_INF, F32)] * (c > 0)
                        + [_nn(k[c * KEY_CHUNK:(c + 1) * KEY_CHUNK], qx_ref[mi, :, c * KEY_CHUNK:])], axis=1)
                    for c in range(KEY_CHUNKS)], axis=0)
            else:
                s = _nn(k, qx_ref[mi])
            keep = keep_fn(mi)
            if keep is not None:
                s = jnp.where(keep, s, NEG_INF)
            s_ref[mi] = s
            tmax_ref[mi] = jnp.max(s, axis=0, keepdims=True)


def _softmax_stage(s_ref, tmax_ref, m_ref, alpha_ref, p_ref):
    for mi in range(2 * HEADS_PER_STEP):
        m_old = m_ref[mi]
        m_new = jnp.maximum(m_old, tmax_ref[mi])
        alpha_ref[mi] = jnp.exp2(m_old - m_new)
        m_ref[mi] = m_new
        p_ref[mi] = jnp.exp2(s_ref[mi] - m_new).astype(BF16)


def _values_stage(vt_of, tile, alpha_ref, p_ref, acc_ref):
    for mi in range(2 * HEADS_PER_STEP):
        vt = jnp.concatenate([vt_of(mi, tile * KEY_CHUNKS + ch) for ch in range(KEY_CHUNKS)], axis=1)
        acc_ref[mi] = alpha_ref[mi] * acc_ref[mi] + _nn(vt, p_ref[mi])


def _no_extra(*_):
    return None


def _query_tile(sub):
    return pl.program_id(2) * Q_TILES_PER_STEP + sub, slice(sub * ATTN_BLOCK, (sub + 1) * ATTN_BLOCK)


def _flash_pipeline(i, qx_ref, k_ref, vt_of, s_ref, tmax_ref, m_ref, alpha_ref, p_ref, acc_ref, diag_keep,
                    past_keep=_no_extra, diag_extra=_no_extra, past_extra=_no_extra, first_in_step=True):
    m_ref[...] = jnp.full(m_ref.shape, M_INIT, F32)

    if first_in_step:
        @pl.when((pl.program_id(0) == 0) & (pl.program_id(1) == 0) & (pl.program_id(2) == 0))
        def _init():
            acc_ref[...] = jnp.zeros(acc_ref.shape, F32)

    def tile_at(step):
        return jnp.where(step == 0, i, step - 1)

    def values(step):
        _values_stage(vt_of, tile_at(step), alpha_ref, p_ref, acc_ref)

    def softmax():
        _softmax_stage(s_ref, tmax_ref, m_ref, alpha_ref, p_ref)

    def scores(tile, keep_fn, k_extra, diagonal=False):
        _scores_stage(qx_ref, k_ref, s_ref, tmax_ref, tile, keep_fn, k_extra, diagonal)

    scores(i, diag_keep, diag_extra, diagonal=True)

    def body(t, carry):
        softmax()
        scores(t, functools.partial(past_keep, t), functools.partial(past_extra, t))
        values(t)
        return carry

    lax.fori_loop(0, i, body, 0)
    softmax()
    values(i)


def _tile_iotas():
    shape = (ATTN_BLOCK, ATTN_BLOCK)
    return lax.broadcasted_iota(jnp.int32, shape, 0), lax.broadcasted_iota(jnp.int32, shape, 1)


def _flash_scratch(acc_rows):
    maps = 2 * HEADS_PER_STEP
    blk = ATTN_BLOCK
    stat = pltpu.VMEM((maps, 1, blk), F32)
    return [pltpu.VMEM((maps, LANES, blk), BF16), pltpu.VMEM((maps, blk, blk), F32), stat, stat, stat,
            pltpu.VMEM((maps, blk, blk), BF16), pltpu.VMEM((maps, acc_rows, blk), F32)]


def _flash_specs(seq, nqb, vt_rows):
    width = LANES * HEADS_PER_STEP
    q_spec = pl.BlockSpec((Q_TILES_PER_STEP * ATTN_BLOCK, width), lambda b, h, i: (b * nqb + i, h))
    k_spec = pl.BlockSpec((seq, width), lambda b, h, i: (b, h))
    vt_spec = pl.BlockSpec((1, seq // KEY_CHUNK, vt_rows * HEADS_PER_STEP, KEY_CHUNK), lambda b, h, i: (b, 0, h, 0),
                           pipeline_mode=pl.Buffered(1))
    return q_spec, k_spec, vt_spec


def _diff_attn_kernel(lq1_ref, lk1_ref, lq2_ref, lk2_ref, subg_ref, q_ref, k_ref, vt_ref, o_ref,
                      qx_ref, s_ref, tmax_ref, m_ref, alpha_ref, p_ref, acc_ref, *, lambda_init):
    rows = LANES + BF16_ROWS

    def vt_of(mi, chunk):
        return vt_ref[0, chunk, (mi // 2) * rows:(mi // 2 + 1) * rows]

    def diag_keep(mi):
        key, qry = _tile_iotas()
        return key <= qry

    lam = (jnp.exp(jnp.sum(lq1_ref[...] * lk1_ref[...], axis=1, keepdims=True))
           - jnp.exp(jnp.sum(lq2_ref[...] * lk2_ref[...], axis=1, keepdims=True)) + lambda_init)
    for sub in range(Q_TILES_PER_STEP):
        i, q_rows = _query_tile(sub)
        _prepare_queries(_transposed_queries(q_ref.at[q_rows]), qx_ref)
        _flash_pipeline(i, qx_ref, k_ref, vt_of, s_ref, tmax_ref, m_ref, alpha_ref, p_ref, acc_ref, diag_keep,
                        first_in_step=sub == 0)
        for hh in range(HEADS_PER_STEP):
            a1, a2 = acc_ref[2 * hh], acc_ref[2 * hh + 1]
            o = a1[:LANES] / a1[LANES:LANES + 1] - lam * (a2[:LANES] / a2[LANES:LANES + 1])
            o = o * lax.rsqrt(jnp.mean(o * o, axis=0, keepdims=True) + NORM_EPS) * subg_ref[...]
            o_ref[q_rows, hh * LANES:(hh + 1) * LANES] = (o * (1.0 - lambda_init)).T.astype(o_ref.dtype)


def _diff_attention(q, k, vt, lam_params, sub_g, *, batch, lambda_init):
    t = q.shape[0]
    seq = t // batch
    nqb = seq // (Q_TILES_PER_STEP * ATTN_BLOCK)
    rows = LANES + BF16_ROWS
    small = pl.BlockSpec((1, HEAD_DIM), lambda b, h, i: (0, 0))
    q_spec, k_spec, vt_spec = _flash_specs(seq, nqb, rows)
    return pl.pallas_call(
        functools.partial(_diff_attn_kernel, lambda_init=lambda_init),
        grid=(batch, PAIRS // HEADS_PER_STEP, nqb),
        in_specs=[small, small, small, small, pl.BlockSpec((LANES, 1), lambda b, h, i: (0, 0)),
                  q_spec, k_spec, vt_spec],
        out_specs=q_spec,
        out_shape=jax.ShapeDtypeStruct((t, D_MODEL), BF16),
        scratch_shapes=_flash_scratch(rows),
        compiler_params=pltpu.CompilerParams(dimension_semantics=("arbitrary",) * 3, vmem_limit_bytes=VMEM_LIMIT),
    )(*[p.reshape(1, HEAD_DIM) for p in lam_params], sub_g.reshape(LANES, 1), q, k, vt)


def _moba_select(gate, own):
    nb, width = gate.shape
    groups = range(nb // SUBLANES)
    sub = lax.broadcasted_iota(jnp.int32, (SUBLANES, width), 0)
    ids = [sub + SUBLANES * r for r in groups]
    g = [jnp.where(ids[r] < own, gate[SUBLANES * r:SUBLANES * (r + 1)], NEG_INF) for r in groups]
    sel = [jnp.zeros((SUBLANES, width), F32) for _ in groups]

    def over_blocks(parts, op):
        x = functools.reduce(op, parts)
        for shift in (4, 2, 1):
            x = op(x, pltpu.roll(x, shift, 0))
        return x

    for t in range(MOBA_TOPK):
        best = over_blocks(g, jnp.maximum)
        first = over_blocks([jnp.where(g[r] == best, ids[r], nb) for r in groups], jnp.minimum)
        valid = jnp.where(t < own, 1.0, 0.0)
        for r in groups:
            hit = ids[r] == first
            sel[r] = jnp.maximum(sel[r], jnp.where(hit, valid, 0.0))
            g[r] = jnp.where(hit, NEG_INF, g[r])
    return jnp.concatenate(sel, axis=0)


def _moba_attn_kernel(q_ref, k_ref, vt_ref, km_ref, o_ref,
                      sel_ref, qx_ref, s_ref, tmax_ref, m_ref, alpha_ref, p_ref, acc_ref):
    for sub in range(Q_TILES_PER_STEP):
        _moba_query_tile(sub, q_ref, k_ref, vt_ref, km_ref, o_ref,
                         sel_ref, qx_ref, s_ref, tmax_ref, m_ref, alpha_ref, p_ref, acc_ref)


def _moba_query_tile(sub, q_ref, k_ref, vt_ref, km_ref, o_ref,
                     sel_ref, qx_ref, s_ref, tmax_ref, m_ref, alpha_ref, p_ref, acc_ref):
    i, q_rows = _query_tile(sub)
    half = HEAD_DIM + BF16_ROWS
    maps = 2 * HEADS_PER_STEP
    nb = sel_ref.shape[1]
    qts = _transposed_queries(q_ref.at[q_rows])
    gates = []
    for hh, qt in enumerate(qts):
        q_hi = qt.astype(BF16)
        q_lo = (qt - q_hi.astype(F32)).astype(BF16)
        for km in _lo_hi(km_ref[0, :, hh * LANES:(hh + 1) * LANES]):
            km_hi = km.astype(BF16)
            km_lo = (km - km_hi.astype(F32)).astype(BF16)
            gates.append(_nn(km_hi, q_hi) + _nn(km_hi, q_lo) + _nn(km_lo, q_hi))
    own = i * KEY_CHUNKS + (lax.broadcasted_iota(jnp.int32, (1, maps * ATTN_BLOCK), 1) % ATTN_BLOCK) // MOBA_BLOCK
    sel = _moba_select(jnp.concatenate(gates, axis=1), own)
    for mi in range(maps):
        sel_ref[mi] = sel[:, mi * ATTN_BLOCK:(mi + 1) * ATTN_BLOCK]

    def vt_of(mi, chunk):
        return vt_ref[0, chunk, mi * half:(mi + 1) * half]

    def selected(block, mi):
        return sel_ref[mi, pl.ds(block, 1), :] > 0.0

    def diag_keep(mi):
        key, qry = _tile_iotas()
        key_blk, qry_blk = key // MOBA_BLOCK, qry // MOBA_BLOCK
        keep = (key <= qry) & (key_blk == qry_blk)
        for ch in range(KEY_CHUNKS - 1):
            keep = keep | ((key_blk == ch) & (qry_blk > ch) & selected(i * KEY_CHUNKS + ch, mi))
        return keep

    def spare_base(c):
        return HEAD_DIM * (1 - c)

    def bias_rows(mi):
        rows = jnp.where(sel_ref[mi] > 0.0, 0.0, MASK_BIAS)
        above = spare_base(mi % 2)
        return jnp.concatenate([jnp.zeros((above, ATTN_BLOCK), F32)] * (above > 0) + [rows]
                               + [jnp.zeros((LANES - above - nb, ATTN_BLOCK), F32)], axis=0)

    def block_marker(t):
        lane = lax.broadcasted_iota(jnp.int32, (MOBA_BLOCK, LANES), 1)
        return [jnp.concatenate([jnp.where(lane == spare_base(c) + t * KEY_CHUNKS + ch, 1.0, 0.0).astype(BF16)
                                 for ch in range(KEY_CHUNKS)], axis=0) for c in range(2)]

    def no_marker():
        return [jnp.zeros((ATTN_BLOCK, LANES), BF16)] * 2

    _prepare_queries(qts, qx_ref, bias_rows)
    _flash_pipeline(i, qx_ref, k_ref, vt_of, s_ref, tmax_ref, m_ref, alpha_ref, p_ref, acc_ref, diag_keep,
                    diag_extra=no_marker, past_extra=block_marker, first_in_step=sub == 0)

    for hh in range(HEADS_PER_STEP):
        a1, a2 = acc_ref[2 * hh], acc_ref[2 * hh + 1]
        o = jnp.concatenate([a1[:HEAD_DIM] / a1[HEAD_DIM:HEAD_DIM + 1], a2[:HEAD_DIM] / a2[HEAD_DIM:HEAD_DIM + 1]],
                            axis=0)
        o_ref[q_rows, hh * LANES:(hh + 1) * LANES] = o.T.astype(o_ref.dtype)


def _moba_attention(q, k, vt, kmean, *, batch):
    t = q.shape[0]
    seq = t // batch
    nqb = seq // (Q_TILES_PER_STEP * ATTN_BLOCK)
    nb = seq // MOBA_BLOCK
    half = HEAD_DIM + BF16_ROWS
    q_spec, k_spec, vt_spec = _flash_specs(seq, nqb, 2 * half)
    return pl.pallas_call(
        _moba_attn_kernel,
        grid=(batch, PAIRS // HEADS_PER_STEP, nqb),
        in_specs=[q_spec, k_spec, vt_spec,
                  pl.BlockSpec((1, nb, LANES * HEADS_PER_STEP), lambda b, h, i: (b, 0, h))],
        out_specs=q_spec,
        out_shape=jax.ShapeDtypeStruct((t, D_MODEL), BF16),
        scratch_shapes=[pltpu.VMEM((2 * HEADS_PER_STEP, nb, ATTN_BLOCK), F32)] + _flash_scratch(half),
        compiler_params=pltpu.CompilerParams(dimension_semantics=("arbitrary",) * 3, vmem_limit_bytes=VMEM_LIMIT),
    )(q, k, vt, kmean)


def _swa_attn_kernel(sink_ref, q_ref, k_ref, vt_ref, o_ref):
    h = pl.program_id(1)
    i = pl.program_id(2)
    sub, w = SWA_SUB_BLOCK, SWA_WINDOW
    subs = q_ref.shape[0] // sub
    key = lax.broadcasted_iota(jnp.int32, (sub, sub), 0)
    qry = lax.broadcasted_iota(jnp.int32, (sub, sub), 1)
    own_keep = (key <= qry) & (qry - key < w)
    pkey = lax.broadcasted_iota(jnp.int32, (w, sub), 0)
    pqry = lax.broadcasted_iota(jnp.int32, (w, sub), 1)
    rows = HEAD_DIM + BF16_ROWS
    row0 = pl.multiple_of((h // (PAIRS // SWA_KV_HEADS)) * rows, BF16_ROWS)

    sinks = [sink_ref[2 * h + half] * LOG2_E for half in range(2)]
    chains = []
    for sb in range(subs):
        blk = i * subs + sb
        q = q_ref[sb * sub:(sb + 1) * sub, :]
        q0 = pl.multiple_of(blk * sub, sub)
        prev0 = pl.multiple_of(jnp.maximum(q0 - w, 0), w)
        k_own = _lo_hi(k_ref[pl.ds(q0, sub), :])
        k_prev = _lo_hi(k_ref[pl.ds(prev0, w), :])
        prev_keep = pkey > pqry
        if sb == 0:
            prev_keep = pkey > pqry + jnp.where(i > 0, 0, w)
        for half in range(2):
            s_own = jnp.where(own_keep, _nt(k_own[half], q), NEG_INF)
            s_prev = jnp.where(prev_keep, _nt(k_prev[half], q), NEG_INF)
            m = jnp.maximum(jnp.maximum(jnp.max(s_own, axis=0, keepdims=True),
                                        jnp.max(s_prev, axis=0, keepdims=True)), sinks[half])
            chains.append((s_own, s_prev, m))

    probs = [(jnp.exp2(s_own - m).astype(BF16), jnp.exp2(s_prev - m).astype(BF16)) for s_own, s_prev, m in chains]

    for sb in range(subs):
        chunk0 = (i * subs + sb) * (sub // w)
        vt_prev = vt_ref[0, jnp.maximum(chunk0 - 1, 0), pl.ds(row0, rows), :]
        vt_own = jnp.concatenate([vt_ref[0, chunk0 + c, pl.ds(row0, rows), :] for c in range(sub // w)], axis=1)
        outs = []
        for half in range(2):
            p_own, p_prev = probs[2 * sb + half]
            acc = _nn(vt_own, p_own) + _nn(vt_prev, p_prev)
            denom = acc[HEAD_DIM:HEAD_DIM + 1] + jnp.exp2(sinks[half] - chains[2 * sb + half][2])
            outs.append(acc[:HEAD_DIM] / denom)
        o_ref[sb * sub:(sb + 1) * sub, :] = jnp.concatenate(outs, axis=0).T.astype(o_ref.dtype)


def _swa_attention(q, kdup, vt, sinks, *, batch):
    t = q.shape[0]
    seq = t // batch
    qb = min(SWA_Q_BLOCK, seq)
    nqb = seq // qb
    per_kv = PAIRS // SWA_KV_HEADS
    vt_rows = SWA_KV_HEADS * (HEAD_DIM + BF16_ROWS)
    return pl.pallas_call(
        _swa_attn_kernel,
        grid=(batch, PAIRS, nqb),
        in_specs=[pl.BlockSpec(memory_space=pltpu.SMEM),
                  pl.BlockSpec((qb, LANES), lambda b, h, i: (b * nqb + i, h)),
                  pl.BlockSpec((seq, LANES), lambda b, h, i: (b, h // per_kv)),
                  pl.BlockSpec((1, seq // SWA_WINDOW, vt_rows, SWA_WINDOW), lambda b, h, i: (b, 0, 0, 0))],
        out_specs=pl.BlockSpec((qb, LANES), lambda b, h, i: (b * nqb + i, h)),
        out_shape=jax.ShapeDtypeStruct((t, D_MODEL), BF16),
        compiler_params=pltpu.CompilerParams(dimension_semantics=("arbitrary",) * 3, vmem_limit_bytes=VMEM_LIMIT),
    )(sinks, q, kdup, vt)


def _post_kernel(*refs, final):
    h_ref, o_ref, wout_ref, g_ref, wup_ref, wdn_ref = refs[:6]
    fg_ref = refs[6] if final else None
    out_ref = refs[-1]
    h1 = h_ref[...] + _nn(o_ref[...], wout_ref[...])
    xn = _rms(h1, g_ref[...]).astype(BF16)
    acc = h1
    for c in range(D_FF // FF_CHUNK):
        lo = c * FF_CHUNK
        a = jnp.maximum(_nn(xn, wup_ref[:, lo:lo + FF_CHUNK]), 0.0)
        acc = acc + _nn((a * a).astype(BF16), wdn_ref[lo:lo + FF_CHUNK, :])
    if final:
        acc = _rms(acc, fg_ref[...])
    out_ref[...] = acc


def _post(h, o, w_out, g, w_up, w_down, layer, final_g=None):
    t = h.shape[0]
    rows = POST_ROWS
    const = lambda i: (0, 0)
    row = lambda i: (i, 0)
    pick = lambda i: (layer, 0, 0)
    once = pl.Buffered(1)
    in_specs = [pl.BlockSpec((rows, D_MODEL), row), pl.BlockSpec((rows, D_MODEL), row),
                pl.BlockSpec(w_out.shape, const, pipeline_mode=once), pl.BlockSpec((1, D_MODEL), const),
                pl.BlockSpec((None,) + w_up.shape[1:], pick, pipeline_mode=once),
                pl.BlockSpec((None,) + w_down.shape[1:], pick, pipeline_mode=once)]
    args = [h, o, w_out, g.reshape(1, D_MODEL), w_up, w_down]
    if final_g is not None:
        in_specs.append(pl.BlockSpec((1, D_MODEL), const))
        args.append(final_g.reshape(1, D_MODEL))
    return pl.pallas_call(
        functools.partial(_post_kernel, final=final_g is not None),
        grid=(t // rows,),
        in_specs=in_specs,
        out_specs=pl.BlockSpec((rows, D_MODEL), row),
        out_shape=jax.ShapeDtypeStruct((t, D_MODEL), F32),
        compiler_params=pltpu.CompilerParams(dimension_semantics=("arbitrary",), vmem_limit_bytes=VMEM_LIMIT),
    )(*args)


def kernel(x, positions, attn_norm_g, mlp_norm_g, diff_w_in, diff_w_out, diff_lam_q1, diff_lam_k1, diff_lam_q2,
           diff_lam_k2, diff_subln_g, moba_w_in, moba_w_out, swa_w_in, swa_b_in, swa_sinks, swa_w_out, mlp_w_up,
           mlp_w_down, final_norm_g):
    batch, seq, _ = x.shape
    depth = attn_norm_g.shape[0]
    tables = _rope_tables(positions)
    h = x.reshape(batch * seq, D_MODEL)
    qk = 2 * D_MODEL
    log2_scale = Q_SCALE * LOG2_E
    w_up_all, w_down_all = mlp_w_up.astype(BF16), mlp_w_down.astype(BF16)

    for i in range(depth):
        mixer, slot = i % N_MIXERS, i // N_MIXERS
        if mixer == 0:
            w_in = diff_w_in[slot]
            q, k, vt = _project(h, attn_norm_g[i], w_in[:, :qk].astype(BF16), w_in[:, qk:].T.astype(BF16), tables,
                                batch=batch, nq=D_MODEL, kv_block=KEY_CHUNK, q_scale=log2_scale, v_group=LANES)
            o = _diff_attention(q, k, vt,
                                (diff_lam_q1[slot], diff_lam_k1[slot], diff_lam_q2[slot], diff_lam_k2[slot]),
                                diff_subln_g[slot], batch=batch, lambda_init=0.8 - 0.6 * math.exp(-0.3 * i))
            w_out = diff_w_out[slot]
        elif mixer == 1:
            w_in = moba_w_in[slot]
            q, k, vt, kmean = _project(h, attn_norm_g[i], w_in[:, :qk].astype(BF16), w_in[:, qk:].T.astype(BF16),
                                       tables, batch=batch, nq=D_MODEL, kv_block=KEY_CHUNK, q_scale=log2_scale,
                                       v_group=HEAD_DIM, q_dtype=F32, with_kmean=True)
            o = _moba_attention(q, k, vt, kmean.reshape(batch, seq // MOBA_BLOCK, D_MODEL), batch=batch)
            w_out = moba_w_out[slot]
        else:
            w_in, b_in = swa_w_in[slot], swa_b_in[slot]
            dup = lambda a: jnp.concatenate(
                [a[..., D_MODEL + kv * HEAD_DIM:D_MODEL + (kv + 1) * HEAD_DIM]
                 for kv in range(SWA_KV_HEADS) for _ in range(2)], axis=-1)
            v0 = D_MODEL + SWA_KV_HEADS * HEAD_DIM
            wqk = jnp.concatenate([w_in[:, :D_MODEL], dup(w_in)], axis=1).astype(BF16)
            bqk = jnp.concatenate([b_in[:D_MODEL], dup(b_in)])
            q, k, vt = _project(h, attn_norm_g[i], wqk, w_in[:, v0:].T.astype(BF16), tables, batch=batch,
                                nq=D_MODEL, kv_block=SWA_WINDOW, q_scale=log2_scale, v_group=HEAD_DIM,
                                bias=(bqk, b_in[v0:]))
            o = _swa_attention(q, k, vt, swa_sinks[slot], batch=batch)
            w_out = swa_w_out[slot]
        h = _post(h, o, w_out.astype(BF16), mlp_norm_g[i], w_up_all, w_down_all, i,
                  final_norm_g if i == depth - 1 else None)
    return h.reshape(batch, seq, D_MODEL)
```

```python
import functools
import math

import jax
import jax.numpy as jnp
from jax import lax
from jax.experimental import pallas as pl
from jax.experimental.pallas import tpu as pltpu

F32 = jnp.float32
BF16 = jnp.bfloat16

D_MODEL = 1024
HEAD_DIM = 64
ROT_DIM = HEAD_DIM // 4
ROT_HALF = ROT_DIM // 2
ROPE_THETA = 500000.0
NORM_EPS = 1e-6
D_FF = 4 * D_MODEL
Q_SCALE = HEAD_DIM ** -0.5
LOG2_E = math.log2(math.e)
N_MIXERS = 3

LANES = 128
SUBLANES = 8
BF16_ROWS = 16
MXU_COLS = 256
PAIRS = D_MODEL // LANES
MOBA_BLOCK = 256
MOBA_TOPK = 3
SWA_WINDOW = 128
SWA_KV_HEADS = 2

ATTN_BLOCK = 512
KEY_CHUNK = 256
KEY_CHUNKS = ATTN_BLOCK // KEY_CHUNK
DIFF_Q_TILES = 2
MOBA_Q_TILES = 4
HEADS_PER_STEP = 4
SWA_Q_BLOCK = 8192
SWA_SUB_BLOCK = 256
ROPE_ROWS = 1024
PROJ_ROWS = 1024
POST_ROWS = 1024
FF_CHUNK = 1024
VMEM_LIMIT = 56 * 1024 * 1024

NEG_INF = float("-inf")
M_INIT = -1e30
MASK_BIAS = -1e30


def _nt(a, b, precision=None):
    return lax.dot_general(a, b, (((1,), (1,)), ((), ())), preferred_element_type=F32, precision=precision)


def _nn(a, b):
    return jnp.dot(a, b, preferred_element_type=F32)


def _rms(x, g):
    return x * lax.rsqrt(jnp.mean(x * x, axis=-1, keepdims=True) + NORM_EPS) * g


def _lo_hi(x):
    lane = lax.broadcasted_iota(jnp.int32, (1, LANES), 1)
    zero = jnp.zeros_like(x)
    return jnp.where(lane < HEAD_DIM, x, zero), jnp.where(lane < HEAD_DIM, zero, x)


def _rope_table_kernel(pos_ref, inv_ref, c_ref, sa_ref, sb_ref):
    ang = pos_ref[...].astype(F32) * inv_ref[...]
    lane = lax.broadcasted_iota(jnp.int32, ang.shape, 1) % HEAD_DIM
    cos = jnp.cos(ang)
    sin = jnp.sin(ang)
    c_ref[...] = jnp.where(lane < ROT_DIM, cos, 1.0)
    sa_ref[...] = jnp.where(lane < ROT_HALF, -sin, 0.0)
    sb_ref[...] = jnp.where((lane >= ROT_HALF) & (lane < ROT_DIM), sin, 0.0)


def _rope_tables(positions):
    t = positions.size
    rows = ROPE_ROWS
    inv = ROPE_THETA ** (-jnp.arange(0, ROT_DIM, 2, dtype=F32) / ROT_DIM)
    lane = jnp.arange(LANES) % HEAD_DIM
    inv_lane = jnp.where(lane < ROT_DIM, inv[lane % ROT_HALF], 0.0).reshape(1, LANES)
    tab = jax.ShapeDtypeStruct((t, LANES), F32)
    spec = pl.BlockSpec((rows, LANES), lambda i: (i, 0))
    return pl.pallas_call(
        _rope_table_kernel,
        grid=(t // rows,),
        in_specs=[pl.BlockSpec((rows, 1), lambda i: (i, 0)), pl.BlockSpec((1, LANES), lambda i: (0, 0))],
        out_specs=[spec, spec, spec],
        out_shape=[tab, tab, tab],
    )(positions.reshape(t, 1), inv_lane)


def _proj_kernel(*refs, nq, nk, kv_block, q_scale, v_group, has_bias, with_kmean):
    it = iter(refs)
    h_ref, g_ref, wqk_ref, wvt_ref = next(it), next(it), next(it), next(it)
    bqk_ref = next(it) if has_bias else None
    bv_ref = next(it) if has_bias else None
    c_ref, sa_ref, sb_ref = next(it), next(it), next(it)
    q_ref, k_ref, vt_ref = next(it), next(it), next(it)
    km_ref = next(it) if with_kmean else None

    rows = h_ref.shape[0]
    xn = _rms(h_ref[...], g_ref[...]).astype(BF16)
    cos, sin_a, sin_b = c_ref[...], sa_ref[...], sb_ref[...]

    for c in range((nq + nk) // LANES):
        col = c * LANES
        if c % 2 == 0:
            wide = _nn(xn, wqk_ref[:, col:col + MXU_COLS])
            if has_bias:
                wide = wide + bqk_ref[:, col:col + MXU_COLS]
        y = wide[:, (c % 2) * LANES:(c % 2 + 1) * LANES]
        y = y * cos + pltpu.roll(y, LANES - ROT_HALF, 1) * sin_a + pltpu.roll(y, ROT_HALF, 1) * sin_b
        if col < nq:
            q_ref[:, col:col + LANES] = (y * q_scale).astype(q_ref.dtype)
        else:
            kc = col - nq
            k_ref[:, kc:kc + LANES] = y.astype(k_ref.dtype)
            if with_kmean:
                for r in range(rows // MOBA_BLOCK):
                    blk = y[r * MOBA_BLOCK:(r + 1) * MOBA_BLOCK]
                    km_ref[0, r:r + 1, kc:kc + LANES] = jnp.sum(blk, axis=0, keepdims=True) * (1.0 / MOBA_BLOCK)

    vt = _nt(wvt_ref[...], xn)
    if has_bias:
        vt = vt + bv_ref[...]
    nv = vt.shape[0]
    for r in range(rows // kv_block):
        blk = vt[:, r * kv_block:(r + 1) * kv_block].astype(vt_ref.dtype)
        if v_group is None:
            vt_ref[0, r] = blk
        else:
            stride = v_group + BF16_ROWS
            for g in range(nv // v_group):
                vt_ref[0, r, g * stride:g * stride + v_group] = blk[g * v_group:(g + 1) * v_group]
                vt_ref[0, r, g * stride + v_group:(g + 1) * stride] = jnp.ones((BF16_ROWS, kv_block), vt_ref.dtype)


def _project(h, g, wqk, wvt, tables, *, batch, nq, kv_block, q_scale, v_group=None, q_dtype=BF16, bias=None,
             with_kmean=False):
    t = h.shape[0]
    rows = PROJ_ROWS
    nk = wqk.shape[1] - nq
    nv = wvt.shape[0]
    nv_out = nv if v_group is None else nv // v_group * (v_group + BF16_ROWS)
    seq = t // batch
    tiles_per_seq = seq // rows
    chunks = rows // kv_block

    const = lambda i: (0, 0)
    row = lambda i: (i, 0)
    in_specs = [pl.BlockSpec((rows, D_MODEL), row), pl.BlockSpec((1, D_MODEL), const),
                pl.BlockSpec(wqk.shape, const), pl.BlockSpec(wvt.shape, const)]
    args = [h, g.reshape(1, D_MODEL), wqk, wvt]
    if bias is not None:
        in_specs += [pl.BlockSpec((1, nq + nk), const), pl.BlockSpec((nv, 1), const)]
        args += [bias[0].reshape(1, nq + nk), bias[1].reshape(nv, 1)]
    in_specs += [pl.BlockSpec((rows, LANES), row)] * 3
    args += list(tables)

    out_shape = [jax.ShapeDtypeStruct((t, nq), q_dtype), jax.ShapeDtypeStruct((t, nk), BF16),
                 jax.ShapeDtypeStruct((batch, seq // kv_block, nv_out, kv_block), BF16)]
    out_specs = [pl.BlockSpec((rows, nq), row), pl.BlockSpec((rows, nk), row),
                 pl.BlockSpec((1, chunks, nv_out, kv_block),
                              lambda i: (i // tiles_per_seq, i % tiles_per_seq, 0, 0))]
    if with_kmean:
        out_shape.append(jax.ShapeDtypeStruct((t // rows, rows // MOBA_BLOCK, nk), F32))
        out_specs.append(pl.BlockSpec((1, rows // MOBA_BLOCK, nk), lambda i: (i, 0, 0)))

    return pl.pallas_call(
        functools.partial(_proj_kernel, nq=nq, nk=nk, kv_block=kv_block, q_scale=q_scale, v_group=v_group,
                          has_bias=bias is not None, with_kmean=with_kmean),
        grid=(t // rows,),
        in_specs=in_specs, out_specs=out_specs, out_shape=out_shape,
        compiler_params=pltpu.CompilerParams(dimension_semantics=("arbitrary",), vmem_limit_bytes=VMEM_LIMIT),
    )(*args)


def _transposed_queries(q_ref):
    return [q_ref[:, hh * LANES:(hh + 1) * LANES].astype(F32).T for hh in range(HEADS_PER_STEP)]


def _prepare_queries(qts, qx_ref, spare_rows=None):
    row = lax.broadcasted_iota(jnp.int32, (LANES, 1), 0)
    for hh, qt in enumerate(qts):
        for c in range(2):
            qx = jnp.where(row < HEAD_DIM if c == 0 else row >= HEAD_DIM, qt, 0.0)
            if spare_rows is not None:
                qx = qx + spare_rows(2 * hh + c)
            qx_ref[2 * hh + c] = qx.astype(BF16)


def _scores_stage(qx_ref, k_ref, s_ref, tmax_ref, tile, keep_fn, k_spare, diagonal=False):
    row0 = pl.multiple_of(tile * ATTN_BLOCK, ATTN_BLOCK)
    spare = k_spare()
    for hh in range(HEADS_PER_STEP):
        k_head = k_ref[pl.ds(row0, ATTN_BLOCK), hh * LANES:(hh + 1) * LANES]
        for c, mi in enumerate((2 * hh, 2 * hh + 1)):
            k = k_head
            if spare is not None:
                lane = lax.broadcasted_iota(jnp.int32, (1, LANES), 1)
                k = jnp.where(lane < HEAD_DIM if c == 0 else lane >= HEAD_DIM, k_head, spare[c])
            if diagonal:
                s = jnp.concatenate([
                    jnp.concatenate(
                        [jnp.full((KEY_CHUNK, c * KEY_CHUNK), NEG_INF, F32)] * (c > 0)
                        + [_nn(k[c * KEY_CHUNK:(c + 1) * KEY_CHUNK], qx_ref[mi, :, c * KEY_CHUNK:])], axis=1)
                    for c in range(KEY_CHUNKS)], axis=0)
            else:
                s = _nn(k, qx_ref[mi])
            keep = keep_fn(mi)
            if keep is not None:
                s = jnp.where(keep, s, NEG_INF)
            s_ref[mi] = s
            tmax_ref[mi] = jnp.max(s, axis=0, keepdims=True)


def _softmax_stage(s_ref, tmax_ref, m_ref, alpha_ref, p_ref):
    for mi in range(2 * HEADS_PER_STEP):
        m_old = m_ref[mi]
        m_new = jnp.maximum(m_old, tmax_ref[mi])
        alpha_ref[mi] = jnp.exp2(m_old - m_new)
        m_ref[mi] = m_new
        p_ref[mi] = jnp.exp2(s_ref[mi] - m_new).astype(BF16)


def _values_stage(vt_of, tile, alpha_ref, p_ref, acc_ref):
    for mi in range(2 * HEADS_PER_STEP):
        vt = jnp.concatenate([vt_of(mi, tile * KEY_CHUNKS + ch) for ch in range(KEY_CHUNKS)], axis=1)
        acc_ref[mi] = alpha_ref[mi] * acc_ref[mi] + _nn(vt, p_ref[mi])


def _no_extra(*_):
    return None


def _query_tile(sub, tiles_per_step):
    return pl.program_id(2) * tiles_per_step + sub, slice(sub * ATTN_BLOCK, (sub + 1) * ATTN_BLOCK)


def _flash_pipeline(i, qx_ref, k_ref, vt_of, s_ref, tmax_ref, m_ref, alpha_ref, p_ref, acc_ref, diag_keep,
                    past_keep=_no_extra, diag_extra=_no_extra, past_extra=_no_extra, first_in_step=True):
    m_ref[...] = jnp.full(m_ref.shape, M_INIT, F32)

    if first_in_step:
        @pl.when((pl.program_id(0) == 0) & (pl.program_id(1) == 0) & (pl.program_id(2) == 0))
        def _init():
            acc_ref[...] = jnp.zeros(acc_ref.shape, F32)

    def tile_at(step):
        return jnp.where(step == 0, i, step - 1)

    def values(step):
        _values_stage(vt_of, tile_at(step), alpha_ref, p_ref, acc_ref)

    def softmax():
        _softmax_stage(s_ref, tmax_ref, m_ref, alpha_ref, p_ref)

    def scores(tile, keep_fn, k_extra, diagonal=False):
        _scores_stage(qx_ref, k_ref, s_ref, tmax_ref, tile, keep_fn, k_extra, diagonal)

    scores(i, diag_keep, diag_extra, diagonal=True)

    def body(t, carry):
        softmax()
        scores(t, functools.partial(past_keep, t), functools.partial(past_extra, t))
        values(t)
        return carry

    lax.fori_loop(0, i, body, 0)
    softmax()
    values(i)


def _flash_pipeline_pair(i, streams, k_ref, vt_of, diag_keep, first_in_step):
    for qx_ref, s_ref, tmax_ref, m_ref, alpha_ref, p_ref, acc_ref in streams:
        m_ref[...] = jnp.full(m_ref.shape, M_INIT, F32)
        if first_in_step:
            @pl.when((pl.program_id(0) == 0) & (pl.program_id(1) == 0) & (pl.program_id(2) == 0))
            def _init():
                acc_ref[...] = jnp.zeros(acc_ref.shape, F32)

    def softmax(st):
        qx_ref, s_ref, tmax_ref, m_ref, alpha_ref, p_ref, acc_ref = streams[st]
        _softmax_stage(s_ref, tmax_ref, m_ref, alpha_ref, p_ref)

    def scores(st, tile, keep_fn=_no_extra, diagonal=False):
        qx_ref, s_ref, tmax_ref, m_ref, alpha_ref, p_ref, acc_ref = streams[st]
        _scores_stage(qx_ref, k_ref, s_ref, tmax_ref, tile, keep_fn, _no_extra, diagonal)

    def values(st, tile):
        qx_ref, s_ref, tmax_ref, m_ref, alpha_ref, p_ref, acc_ref = streams[st]
        _values_stage(vt_of, tile, alpha_ref, p_ref, acc_ref)

    def tile_at(step):
        return jnp.where(step == 0, i, step - 1)

    scores(0, i, diag_keep, diagonal=True)
    scores(1, i + 1, diag_keep, diagonal=True)
    softmax(1)
    scores(1, i)
    values(1, i + 1)

    def body(t, carry):
        softmax(0)
        softmax(1)
        scores(0, t)
        scores(1, t)
        values(0, tile_at(t))
        values(1, tile_at(t))
        return carry

    lax.fori_loop(0, i, body, 0)
    softmax(0)
    softmax(1)
    values(0, tile_at(i))
    values(1, tile_at(i))


def _tile_iotas():
    shape = (ATTN_BLOCK, ATTN_BLOCK)
    return lax.broadcasted_iota(jnp.int32, shape, 0), lax.broadcasted_iota(jnp.int32, shape, 1)


def _flash_scratch(acc_rows):
    maps = 2 * HEADS_PER_STEP
    blk = ATTN_BLOCK
    stat = pltpu.VMEM((maps, 1, blk), F32)
    return [pltpu.VMEM((maps, LANES, blk), BF16), pltpu.VMEM((maps, blk, blk), F32), stat, stat, stat,
            pltpu.VMEM((maps, blk, blk), BF16), pltpu.VMEM((maps, acc_rows, blk), F32)]


def _flash_specs(seq, q_tiles, vt_rows, k_buffers=2):
    width = LANES * HEADS_PER_STEP
    nqb = seq // (q_tiles * ATTN_BLOCK)
    q_spec = pl.BlockSpec((q_tiles * ATTN_BLOCK, width), lambda b, h, i: (b * nqb + i, h))
    k_spec = pl.BlockSpec((seq, width), lambda b, h, i: (b, h), pipeline_mode=pl.Buffered(k_buffers))
    vt_spec = pl.BlockSpec((1, seq // KEY_CHUNK, vt_rows * HEADS_PER_STEP, KEY_CHUNK), lambda b, h, i: (b, 0, h, 0),
                           pipeline_mode=pl.Buffered(1))
    return q_spec, k_spec, vt_spec


def _key_tile_copies(k_hbm, vt_hbm, k_ref, vt_ref, k_sem, v_sem, tile):
    b, hg = pl.program_id(0), pl.program_id(1)
    seq, width = k_ref.shape
    rows = vt_ref.shape[1]
    k_copy = pltpu.make_async_copy(
        k_hbm.at[pl.ds(pl.multiple_of(b * seq + tile * ATTN_BLOCK, ATTN_BLOCK), ATTN_BLOCK),
                 pl.ds(pl.multiple_of(hg * width, LANES), width)],
        k_ref.at[pl.ds(pl.multiple_of(tile * ATTN_BLOCK, ATTN_BLOCK), ATTN_BLOCK)], k_sem.at[tile])
    v_copy = pltpu.make_async_copy(
        vt_hbm.at[b, pl.ds(tile * KEY_CHUNKS, KEY_CHUNKS), pl.ds(pl.multiple_of(hg * rows, BF16_ROWS), rows)],
        vt_ref.at[pl.ds(tile * KEY_CHUNKS, KEY_CHUNKS)], v_sem.at[tile])
    return k_copy, v_copy


def _diff_attn_kernel(lq1_ref, lk1_ref, lq2_ref, lk2_ref, subg_ref, q_ref, k_hbm, vt_hbm, o_ref,
                      k_ref, vt_ref, k_sem, v_sem, *scratch, lambda_init):
    rows = LANES + BF16_ROWS
    streams = (scratch[:len(scratch) // 2], scratch[len(scratch) // 2:])
    copies = functools.partial(_key_tile_copies, k_hbm, vt_hbm, k_ref, vt_ref, k_sem, v_sem)

    @pl.when(pl.program_id(2) == 0)
    def _start_fetch():
        for tile in range(k_ref.shape[0] // ATTN_BLOCK):
            for copy in copies(tile):
                copy.start()

    for st in range(DIFF_Q_TILES):
        for copy in copies(pl.program_id(2) * DIFF_Q_TILES + st):
            copy.wait()

    def vt_of(mi, chunk):
        return vt_ref[chunk, (mi // 2) * rows:(mi // 2 + 1) * rows]

    def diag_keep(mi):
        key, qry = _tile_iotas()
        return key <= qry

    lam = (jnp.exp(jnp.sum(lq1_ref[...] * lk1_ref[...], axis=1, keepdims=True))
           - jnp.exp(jnp.sum(lq2_ref[...] * lk2_ref[...], axis=1, keepdims=True)) + lambda_init)
    for pair in range(DIFF_Q_TILES // 2):
        tiles = [_query_tile(2 * pair + st, DIFF_Q_TILES) for st in range(2)]
        for st, (_, q_rows) in enumerate(tiles):
            _prepare_queries(_transposed_queries(q_ref.at[q_rows]), streams[st][0])
        _flash_pipeline_pair(tiles[0][0], streams, k_ref, vt_of, diag_keep, first_in_step=pair == 0)
        for st, (_, q_rows) in enumerate(tiles):
            acc_ref = streams[st][-1]
            for hh in range(HEADS_PER_STEP):
                a1, a2 = acc_ref[2 * hh], acc_ref[2 * hh + 1]
                o = a1[:LANES] / a1[LANES:LANES + 1] - lam * (a2[:LANES] / a2[LANES:LANES + 1])
                o = o * lax.rsqrt(jnp.mean(o * o, axis=0, keepdims=True) + NORM_EPS) * subg_ref[...]
                o_ref[q_rows, hh * LANES:(hh + 1) * LANES] = (o * (1.0 - lambda_init)).T.astype(o_ref.dtype)


def _diff_attention(q, k, vt, lam_params, sub_g, *, batch, lambda_init):
    t = q.shape[0]
    seq = t // batch
    nqb = seq // (DIFF_Q_TILES * ATTN_BLOCK)
    rows = LANES + BF16_ROWS
    small = pl.BlockSpec((1, HEAD_DIM), lambda b, h, i: (0, 0))
    q_spec, _, _ = _flash_specs(seq, DIFF_Q_TILES, rows)
    in_hbm = pl.BlockSpec(memory_space=pl.ANY)
    key_scratch = [pltpu.VMEM((seq, LANES * HEADS_PER_STEP), BF16),
                   pltpu.VMEM((seq // KEY_CHUNK, rows * HEADS_PER_STEP, KEY_CHUNK), BF16),
                   pltpu.SemaphoreType.DMA((seq // ATTN_BLOCK,)), pltpu.SemaphoreType.DMA((seq // ATTN_BLOCK,))]
    return pl.pallas_call(
        functools.partial(_diff_attn_kernel, lambda_init=lambda_init),
        grid=(batch, PAIRS // HEADS_PER_STEP, nqb),
        in_specs=[small, small, small, small, pl.BlockSpec((LANES, 1), lambda b, h, i: (0, 0)),
                  q_spec, in_hbm, in_hbm],
        out_specs=q_spec,
        out_shape=jax.ShapeDtypeStruct((t, D_MODEL), BF16),
        scratch_shapes=key_scratch + _flash_scratch(rows) * 2,
        compiler_params=pltpu.CompilerParams(dimension_semantics=("arbitrary",) * 3, vmem_limit_bytes=VMEM_LIMIT),
    )(*[p.reshape(1, HEAD_DIM) for p in lam_params], sub_g.reshape(LANES, 1), q, k, vt)


def _moba_select(gate, own):
    nb, width = gate.shape
    groups = range(nb // SUBLANES)
    sub = lax.broadcasted_iota(jnp.int32, (SUBLANES, width), 0)
    ids = [sub + SUBLANES * r for r in groups]
    g = [jnp.where(ids[r] < own, gate[SUBLANES * r:SUBLANES * (r + 1)], NEG_INF) for r in groups]
    sel = [jnp.zeros((SUBLANES, width), F32) for _ in groups]

    def over_blocks(parts, op):
        x = functools.reduce(op, parts)
        for shift in (4, 2, 1):
            x = op(x, pltpu.roll(x, shift, 0))
        return x

    for t in range(MOBA_TOPK):
        best = over_blocks(g, jnp.maximum)
        first = over_blocks([jnp.where(g[r] == best, ids[r], nb) for r in groups], jnp.minimum)
        valid = jnp.where(t < own, 1.0, 0.0)
        for r in groups:
            hit = ids[r] == first
            sel[r] = jnp.maximum(sel[r], jnp.where(hit, valid, 0.0))
            g[r] = jnp.where(hit, NEG_INF, g[r])
    return jnp.concatenate(sel, axis=0)


def _moba_attn_kernel(q_ref, k_ref, vt_ref, km_ref, o_ref,
                      sel_ref, qx_ref, s_ref, tmax_ref, m_ref, alpha_ref, p_ref, acc_ref):
    for sub in range(MOBA_Q_TILES):
        _moba_query_tile(sub, q_ref, k_ref, vt_ref, km_ref, o_ref,
                         sel_ref, qx_ref, s_ref, tmax_ref, m_ref, alpha_ref, p_ref, acc_ref)


def _moba_query_tile(sub, q_ref, k_ref, vt_ref, km_ref, o_ref,
                     sel_ref, qx_ref, s_ref, tmax_ref, m_ref, alpha_ref, p_ref, acc_ref):
    i, q_rows = _query_tile(sub, MOBA_Q_TILES)
    half = HEAD_DIM + BF16_ROWS
    maps = 2 * HEADS_PER_STEP
    nb = sel_ref.shape[1]
    qts = _transposed_queries(q_ref.at[q_rows])
    gates = []
    for hh, qt in enumerate(qts):
        q_hi = qt.astype(BF16)
        q_lo = (qt - q_hi.astype(F32)).astype(BF16)
        for km in _lo_hi(km_ref[0, :, hh * LANES:(hh + 1) * LANES]):
            km_hi = km.astype(BF16)
            km_lo = (km - km_hi.astype(F32)).astype(BF16)
            gates.append(_nn(km_hi, q_hi) + _nn(km_hi, q_lo) + _nn(km_lo, q_hi))
    own = i * KEY_CHUNKS + (lax.broadcasted_iota(jnp.int32, (1, maps * ATTN_BLOCK), 1) % ATTN_BLOCK) // MOBA_BLOCK
    sel = _moba_select(jnp.concatenate(gates, axis=1), own)
    for mi in range(maps):
        sel_ref[mi] = sel[:, mi * ATTN_BLOCK:(mi + 1) * ATTN_BLOCK]

    def vt_of(mi, chunk):
        return vt_ref[0, chunk, mi * half:(mi + 1) * half]

    def selected(block, mi):
        return sel_ref[mi, pl.ds(block, 1), :] > 0.0

    def diag_keep(mi):
        key, qry = _tile_iotas()
        key_blk, qry_blk = key // MOBA_BLOCK, qry // MOBA_BLOCK
        keep = (key <= qry) & (key_blk == qry_blk)
        for ch in range(KEY_CHUNKS - 1):
            keep = keep | ((key_blk == ch) & (qry_blk > ch) & selected(i * KEY_CHUNKS + ch, mi))
        return keep

    def spare_base(c):
        return HEAD_DIM * (1 - c)

    def bias_rows(mi):
        rows = jnp.where(sel_ref[mi] > 0.0, 0.0, MASK_BIAS)
        above = spare_base(mi % 2)
        return jnp.concatenate([jnp.zeros((above, ATTN_BLOCK), F32)] * (above > 0) + [rows]
                               + [jnp.zeros((LANES - above - nb, ATTN_BLOCK), F32)], axis=0)

    def block_marker(t):
        lane = lax.broadcasted_iota(jnp.int32, (MOBA_BLOCK, LANES), 1)
        return [jnp.concatenate([jnp.where(lane == spare_base(c) + t * KEY_CHUNKS + ch, 1.0, 0.0).astype(BF16)
                                 for ch in range(KEY_CHUNKS)], axis=0) for c in range(2)]

    def no_marker():
        return [jnp.zeros((ATTN_BLOCK, LANES), BF16)] * 2

    _prepare_queries(qts, qx_ref, bias_rows)
    _flash_pipeline(i, qx_ref, k_ref, vt_of, s_ref, tmax_ref, m_ref, alpha_ref, p_ref, acc_ref, diag_keep,
                    diag_extra=no_marker, past_extra=block_marker, first_in_step=sub == 0)

    for hh in range(HEADS_PER_STEP):
        a1, a2 = acc_ref[2 * hh], acc_ref[2 * hh + 1]
        o = jnp.concatenate([a1[:HEAD_DIM] / a1[HEAD_DIM:HEAD_DIM + 1], a2[:HEAD_DIM] / a2[HEAD_DIM:HEAD_DIM + 1]],
                            axis=0)
        o_ref[q_rows, hh * LANES:(hh + 1) * LANES] = o.T.astype(o_ref.dtype)


def _moba_attention(q, k, vt, kmean, *, batch):
    t = q.shape[0]
    seq = t // batch
    nqb = seq // (MOBA_Q_TILES * ATTN_BLOCK)
    nb = seq // MOBA_BLOCK
    half = HEAD_DIM + BF16_ROWS
    q_spec, k_spec, vt_spec = _flash_specs(seq, MOBA_Q_TILES, 2 * half)
    return pl.pallas_call(
        _moba_attn_kernel,
        grid=(batch, PAIRS // HEADS_PER_STEP, nqb),
        in_specs=[q_spec, k_spec, vt_spec,
                  pl.BlockSpec((1, nb, LANES * HEADS_PER_STEP), lambda b, h, i: (b, 0, h))],
        out_specs=q_spec,
        out_shape=jax.ShapeDtypeStruct((t, D_MODEL), BF16),
        scratch_shapes=[pltpu.VMEM((2 * HEADS_PER_STEP, nb, ATTN_BLOCK), F32)] + _flash_scratch(half),
        compiler_params=pltpu.CompilerParams(dimension_semantics=("arbitrary",) * 3, vmem_limit_bytes=VMEM_LIMIT),
    )(q, k, vt, kmean)


def _swa_attn_kernel(sink_ref, q_ref, k_ref, vt_ref, o_ref):
    h = pl.program_id(1)
    i = pl.program_id(2)
    sub, w = SWA_SUB_BLOCK, SWA_WINDOW
    subs = q_ref.shape[0] // sub
    key = lax.broadcasted_iota(jnp.int32, (sub, sub), 0)
    qry = lax.broadcasted_iota(jnp.int32, (sub, sub), 1)
    own_keep = (key <= qry) & (qry - key < w)
    pkey = lax.broadcasted_iota(jnp.int32, (w, sub), 0)
    pqry = lax.broadcasted_iota(jnp.int32, (w, sub), 1)
    rows = HEAD_DIM + BF16_ROWS
    row0 = pl.multiple_of((h // (PAIRS // SWA_KV_HEADS)) * rows, BF16_ROWS)

    sinks = [sink_ref[2 * h + half] * LOG2_E for half in range(2)]
    chains = []
    for sb in range(subs):
        blk = i * subs + sb
        q = q_ref[sb * sub:(sb + 1) * sub, :]
        q0 = pl.multiple_of(blk * sub, sub)
        prev0 = pl.multiple_of(jnp.maximum(q0 - w, 0), w)
        k_own = _lo_hi(k_ref[pl.ds(q0, sub), :])
        k_prev = _lo_hi(k_ref[pl.ds(prev0, w), :])
        prev_keep = pkey > pqry
        if sb == 0:
            prev_keep = pkey > pqry + jnp.where(i > 0, 0, w)
        for half in range(2):
            s_own = jnp.where(own_keep, _nt(k_own[half], q), NEG_INF)
            s_prev = jnp.where(prev_keep, _nt(k_prev[half], q), NEG_INF)
            m = jnp.maximum(jnp.maximum(jnp.max(s_own, axis=0, keepdims=True),
                                        jnp.max(s_prev, axis=0, keepdims=True)), sinks[half])
            chains.append((s_own, s_prev, m))

    probs = [(jnp.exp2(s_own - m).astype(BF16), jnp.exp2(s_prev - m).astype(BF16)) for s_own, s_prev, m in chains]

    for sb in range(subs):
        chunk0 = (i * subs + sb) * (sub // w)
        vt_prev = vt_ref[0, jnp.maximum(chunk0 - 1, 0), pl.ds(row0, rows), :]
        vt_own = jnp.concatenate([vt_ref[0, chunk0 + c, pl.ds(row0, rows), :] for c in range(sub // w)], axis=1)
        outs = []
        for half in range(2):
            p_own, p_prev = probs[2 * sb + half]
            acc = _nn(vt_own, p_own) + _nn(vt_prev, p_prev)
            denom = acc[HEAD_DIM:HEAD_DIM + 1] + jnp.exp2(sinks[half] - chains[2 * sb + half][2])
            outs.append(acc[:HEAD_DIM] / denom)
        o_ref[sb * sub:(sb + 1) * sub, :] = jnp.concatenate(outs, axis=0).T.astype(o_ref.dtype)


def _swa_attention(q, kdup, vt, sinks, *, batch):
    t = q.shape[0]
    seq = t // batch
    qb = min(SWA_Q_BLOCK, seq)
    nqb = seq // qb
    per_kv = PAIRS // SWA_KV_HEADS
    vt_rows = SWA_KV_HEADS * (HEAD_DIM + BF16_ROWS)
    return pl.pallas_call(
        _swa_attn_kernel,
        grid=(batch, PAIRS, nqb),
        in_specs=[pl.BlockSpec(memory_space=pltpu.SMEM),
                  pl.BlockSpec((qb, LANES), lambda b, h, i: (b * nqb + i, h)),
                  pl.BlockSpec((seq, LANES), lambda b, h, i: (b, h // per_kv)),
                  pl.BlockSpec((1, seq // SWA_WINDOW, vt_rows, SWA_WINDOW), lambda b, h, i: (b, 0, 0, 0))],
        out_specs=pl.BlockSpec((qb, LANES), lambda b, h, i: (b * nqb + i, h)),
        out_shape=jax.ShapeDtypeStruct((t, D_MODEL), BF16),
        compiler_params=pltpu.CompilerParams(dimension_semantics=("arbitrary",) * 3, vmem_limit_bytes=VMEM_LIMIT),
    )(sinks, q, kdup, vt)


def _post_kernel(*refs, final):
    h_ref, o_ref, wout_ref, g_ref, wup_ref, wdn_ref = refs[:6]
    fg_ref = refs[6] if final else None
    out_ref = refs[-1]
    h1 = h_ref[...] + _nn(o_ref[...], wout_ref[...])
    xn = _rms(h1, g_ref[...]).astype(BF16)
    acc = h1
    for c in range(D_FF // FF_CHUNK):
        lo = c * FF_CHUNK
        a = jnp.maximum(_nn(xn, wup_ref[:, lo:lo + FF_CHUNK]), 0.0)
        acc = acc + _nn((a * a).astype(BF16), wdn_ref[lo:lo + FF_CHUNK, :])
    if final:
        acc = _rms(acc, fg_ref[...])
    out_ref[...] = acc


def _post(h, o, w_out, g, w_up, w_down, layer, final_g=None):
    t = h.shape[0]
    rows = POST_ROWS
    const = lambda i: (0, 0)
    row = lambda i: (i, 0)
    pick = lambda i: (layer, 0, 0)
    once = pl.Buffered(1)
    in_specs = [pl.BlockSpec((rows, D_MODEL), row), pl.BlockSpec((rows, D_MODEL), row),
                pl.BlockSpec(w_out.shape, const, pipeline_mode=once), pl.BlockSpec((1, D_MODEL), const),
                pl.BlockSpec((None,) + w_up.shape[1:], pick, pipeline_mode=once),
                pl.BlockSpec((None,) + w_down.shape[1:], pick, pipeline_mode=once)]
    args = [h, o, w_out, g.reshape(1, D_MODEL), w_up, w_down]
    if final_g is not None:
        in_specs.append(pl.BlockSpec((1, D_MODEL), const))
        args.append(final_g.reshape(1, D_MODEL))
    return pl.pallas_call(
        functools.partial(_post_kernel, final=final_g is not None),
        grid=(t // rows,),
        in_specs=in_specs,
        out_specs=pl.BlockSpec((rows, D_MODEL), row),
        out_shape=jax.ShapeDtypeStruct((t, D_MODEL), F32),
        compiler_params=pltpu.CompilerParams(dimension_semantics=("arbitrary",), vmem_limit_bytes=VMEM_LIMIT),
    )(*args)


def kernel(x, positions, attn_norm_g, mlp_norm_g, diff_w_in, diff_w_out, diff_lam_q1, diff_lam_k1, diff_lam_q2,
           diff_lam_k2, diff_subln_g, moba_w_in, moba_w_out, swa_w_in, swa_b_in, swa_sinks, swa_w_out, mlp_w_up,
           mlp_w_down, final_norm_g):
    batch, seq, _ = x.shape
    depth = attn_norm_g.shape[0]
    tables = _rope_tables(positions)
    h = x.reshape(batch * seq, D_MODEL)
    qk = 2 * D_MODEL
    log2_scale = Q_SCALE * LOG2_E
    w_up_all, w_down_all = mlp_w_up.astype(BF16), mlp_w_down.astype(BF16)

    for i in range(depth):
        mixer, slot = i % N_MIXERS, i // N_MIXERS
        if mixer == 0:
            w_in = diff_w_in[slot]
            q, k, vt = _project(h, attn_norm_g[i], w_in[:, :qk].astype(BF16), w_in[:, qk:].T.astype(BF16), tables,
                                batch=batch, nq=D_MODEL, kv_block=KEY_CHUNK, q_scale=log2_scale, v_group=LANES)
            o = _diff_attention(q, k, vt,
                                (diff_lam_q1[slot], diff_lam_k1[slot], diff_lam_q2[slot], diff_lam_k2[slot]),
                                diff_subln_g[slot], batch=batch, lambda_init=0.8 - 0.6 * math.exp(-0.3 * i))
            w_out = diff_w_out[slot]
        elif mixer == 1:
            w_in = moba_w_in[slot]
            q, k, vt, kmean = _project(h, attn_norm_g[i], w_in[:, :qk].astype(BF16), w_in[:, qk:].T.astype(BF16),
                                       tables, batch=batch, nq=D_MODEL, kv_block=KEY_CHUNK, q_scale=log2_scale,
                                       v_group=HEAD_DIM, q_dtype=F32, with_kmean=True)
            o = _moba_attention(q, k, vt, kmean.reshape(batch, seq // MOBA_BLOCK, D_MODEL), batch=batch)
            w_out = moba_w_out[slot]
        else:
            w_in, b_in = swa_w_in[slot], swa_b_in[slot]
            dup = lambda a: jnp.concatenate(
                [a[..., D_MODEL + kv * HEAD_DIM:D_MODEL + (kv + 1) * HEAD_DIM]
                 for kv in range(SWA_KV_HEADS) for _ in range(2)], axis=-1)
            v0 = D_MODEL + SWA_KV_HEADS * HEAD_DIM
            wqk = jnp.concatenate([w_in[:, :D_MODEL], dup(w_in)], axis=1).astype(BF16)
            bqk = jnp.concatenate([b_in[:D_MODEL], dup(b_in)])
            q, k, vt = _project(h, attn_norm_g[i], wqk, w_in[:, v0:].T.astype(BF16), tables, batch=batch,
                                nq=D_MODEL, kv_block=SWA_WINDOW, q_scale=log2_scale, v_group=HEAD_DIM,
                                bias=(bqk, b_in[v0:]))
            o = _swa_attention(q, k, vt, swa_sinks[slot], batch=batch)
            w_out = swa_w_out[slot]
        h = _post(h, o, w_out.astype(BF16), mlp_norm_g[i], w_up_all, w_down_all, i,
                  final_norm_g if i == depth - 1 else None)
    return h.reshape(batch, seq, D_MODEL)
```

```python
import functools
import math

import jax
import jax.numpy as jnp
from jax import lax
from jax.experimental import pallas as pl
from jax.experimental.pallas import tpu as pltpu

F32 = jnp.float32
BF16 = jnp.bfloat16

D_MODEL = 1024
HEAD_DIM = 64
ROT_DIM = HEAD_DIM // 4
ROT_HALF = ROT_DIM // 2
ROPE_THETA = 500000.0
NORM_EPS = 1e-6
D_FF = 4 * D_MODEL
Q_SCALE = HEAD_DIM ** -0.5
LOG2_E = math.log2(math.e)
N_MIXERS = 3

LANES = 128
SUBLANES = 8
BF16_ROWS = 16
MXU_COLS = 256
PAIRS = D_MODEL // LANES
MOBA_BLOCK = 256
MOBA_TOPK = 3
SWA_WINDOW = 128
SWA_KV_HEADS = 2

ATTN_BLOCK = 512
KEY_CHUNK = 256
KEY_CHUNKS = ATTN_BLOCK // KEY_CHUNK
DIFF_Q_TILES = 2
MOBA_Q_TILES = 4
HEADS_PER_STEP = 4
SWA_Q_BLOCK = 8192
SWA_SUB_BLOCK = 256
ROPE_ROWS = 1024
PROJ_ROWS = 1024
POST_ROWS = 1024
FF_CHUNK = 1024
VMEM_LIMIT = 56 * 1024 * 1024

NEG_INF = float("-inf")
M_INIT = -1e30
MASK_BIAS = -1e30


def _nt(a, b, precision=None):
    return lax.dot_general(a, b, (((1,), (1,)), ((), ())), preferred_element_type=F32, precision=precision)


def _nn(a, b):
    return jnp.dot(a, b, preferred_element_type=F32)


def _rms(x, g):
    return x * lax.rsqrt(jnp.mean(x * x, axis=-1, keepdims=True) + NORM_EPS) * g


def _lo_hi(x):
    lane = lax.broadcasted_iota(jnp.int32, (1, LANES), 1)
    zero = jnp.zeros_like(x)
    return jnp.where(lane < HEAD_DIM, x, zero), jnp.where(lane < HEAD_DIM, zero, x)


def _rope_table_kernel(pos_ref, inv_ref, c_ref, sa_ref, sb_ref):
    ang = pos_ref[...].astype(F32) * inv_ref[...]
    lane = lax.broadcasted_iota(jnp.int32, ang.shape, 1) % HEAD_DIM
    cos = jnp.cos(ang)
    sin = jnp.sin(ang)
    c_ref[...] = jnp.where(lane < ROT_DIM, cos, 1.0)
    sa_ref[...] = jnp.where(lane < ROT_HALF, -sin, 0.0)
    sb_ref[...] = jnp.where((lane >= ROT_HALF) & (lane < ROT_DIM), sin, 0.0)


def _rope_tables(positions):
    t = positions.size
    rows = ROPE_ROWS
    inv = ROPE_THETA ** (-jnp.arange(0, ROT_DIM, 2, dtype=F32) / ROT_DIM)
    lane = jnp.arange(LANES) % HEAD_DIM
    inv_lane = jnp.where(lane < ROT_DIM, inv[lane % ROT_HALF], 0.0).reshape(1, LANES)
    tab = jax.ShapeDtypeStruct((t, LANES), F32)
    spec = pl.BlockSpec((rows, LANES), lambda i: (i, 0))
    return pl.pallas_call(
        _rope_table_kernel,
        grid=(t // rows,),
        in_specs=[pl.BlockSpec((rows, 1), lambda i: (i, 0)), pl.BlockSpec((1, LANES), lambda i: (0, 0))],
        out_specs=[spec, spec, spec],
        out_shape=[tab, tab, tab],
    )(positions.reshape(t, 1), inv_lane)


def _proj_kernel(*refs, nq, nk, kv_block, q_scale, v_group, has_bias, with_kmean):
    it = iter(refs)
    h_ref, g_ref, wqk_ref, wvt_ref = next(it), next(it), next(it), next(it)
    bqk_ref = next(it) if has_bias else None
    bv_ref = next(it) if has_bias else None
    c_ref, sa_ref, sb_ref = next(it), next(it), next(it)
    q_ref, k_ref, vt_ref = next(it), next(it), next(it)
    km_ref = next(it) if with_kmean else None

    rows = h_ref.shape[0]
    xn = _rms(h_ref[...], g_ref[...]).astype(BF16)
    cos, sin_a, sin_b = c_ref[...], sa_ref[...], sb_ref[...]

    for c in range((nq + nk) // LANES):
        col = c * LANES
        if c % 2 == 0:
            wide = _nn(xn, wqk_ref[:, col:col + MXU_COLS])
            if has_bias:
                wide = wide + bqk_ref[:, col:col + MXU_COLS]
        y = wide[:, (c % 2) * LANES:(c % 2 + 1) * LANES]
        y = y * cos + pltpu.roll(y, LANES - ROT_HALF, 1) * sin_a + pltpu.roll(y, ROT_HALF, 1) * sin_b
        if col < nq:
            q_ref[:, col:col + LANES] = (y * q_scale).astype(q_ref.dtype)
        else:
            kc = col - nq
            k_ref[:, kc:kc + LANES] = y.astype(k_ref.dtype)
            if with_kmean:
                for r in range(rows // MOBA_BLOCK):
                    blk = y[r * MOBA_BLOCK:(r + 1) * MOBA_BLOCK]
                    km_ref[0, r:r + 1, kc:kc + LANES] = jnp.sum(blk, axis=0, keepdims=True) * (1.0 / MOBA_BLOCK)

    vt = _nt(wvt_ref[...], xn)
    if has_bias:
        vt = vt + bv_ref[...]
    nv = vt.shape[0]
    for r in range(rows // kv_block):
        blk = vt[:, r * kv_block:(r + 1) * kv_block].astype(vt_ref.dtype)
        if v_group is None:
            vt_ref[0, r] = blk
        else:
            stride = v_group + BF16_ROWS
            for g in range(nv // v_group):
                vt_ref[0, r, g * stride:g * stride + v_group] = blk[g * v_group:(g + 1) * v_group]
                vt_ref[0, r, g * stride + v_group:(g + 1) * stride] = jnp.ones((BF16_ROWS, kv_block), vt_ref.dtype)


def _project(h, g, wqk, wvt, tables, *, batch, nq, kv_block, q_scale, v_group=None, q_dtype=BF16, bias=None,
             with_kmean=False):
    t = h.shape[0]
    rows = PROJ_ROWS
    nk = wqk.shape[1] - nq
    nv = wvt.shape[0]
    nv_out = nv if v_group is None else nv // v_group * (v_group + BF16_ROWS)
    seq = t // batch
    tiles_per_seq = seq // rows
    chunks = rows // kv_block

    const = lambda i: (0, 0)
    row = lambda i: (i, 0)
    in_specs = [pl.BlockSpec((rows, D_MODEL), row), pl.BlockSpec((1, D_MODEL), const),
                pl.BlockSpec(wqk.shape, const), pl.BlockSpec(wvt.shape, const)]
    args = [h, g.reshape(1, D_MODEL), wqk, wvt]
    if bias is not None:
        in_specs += [pl.BlockSpec((1, nq + nk), const), pl.BlockSpec((nv, 1), const)]
        args += [bias[0].reshape(1, nq + nk), bias[1].reshape(nv, 1)]
    in_specs += [pl.BlockSpec((rows, LANES), row)] * 3
    args += list(tables)

    out_shape = [jax.ShapeDtypeStruct((t, nq), q_dtype), jax.ShapeDtypeStruct((t, nk), BF16),
                 jax.ShapeDtypeStruct((batch, seq // kv_block, nv_out, kv_block), BF16)]
    out_specs = [pl.BlockSpec((rows, nq), row), pl.BlockSpec((rows, nk), row),
                 pl.BlockSpec((1, chunks, nv_out, kv_block),
                              lambda i: (i // tiles_per_seq, i % tiles_per_seq, 0, 0))]
    if with_kmean:
        out_shape.append(jax.ShapeDtypeStruct((t // rows, rows // MOBA_BLOCK, nk), F32))
        out_specs.append(pl.BlockSpec((1, rows // MOBA_BLOCK, nk), lambda i: (i, 0, 0)))

    return pl.pallas_call(
        functools.partial(_proj_kernel, nq=nq, nk=nk, kv_block=kv_block, q_scale=q_scale, v_group=v_group,
                          has_bias=bias is not None, with_kmean=with_kmean),
        grid=(t // rows,),
        in_specs=in_specs, out_specs=out_specs, out_shape=out_shape,
        compiler_params=pltpu.CompilerParams(dimension_semantics=("arbitrary",), vmem_limit_bytes=VMEM_LIMIT),
    )(*args)


def _transposed_queries(q_ref):
    return [q_ref[:, hh * LANES:(hh + 1) * LANES].astype(F32).T for hh in range(HEADS_PER_STEP)]


def _prepare_queries(qts, qx_ref, spare_rows=None):
    row = lax.broadcasted_iota(jnp.int32, (LANES, 1), 0)
    for hh, qt in enumerate(qts):
        for c in range(2):
            qx = jnp.where(row < HEAD_DIM if c == 0 else row >= HEAD_DIM, qt, 0.0)
            if spare_rows is not None:
                qx = qx + spare_rows(2 * hh + c)
            qx_ref[2 * hh + c] = qx.astype(BF16)


def _scores_stage(qx_ref, k_ref, s_ref, tmax_ref, tile, keep_fn, k_spare, diagonal=False):
    row0 = pl.multiple_of(tile * ATTN_BLOCK, ATTN_BLOCK)
    spare = k_spare()
    for hh in range(HEADS_PER_STEP):
        k_head = k_ref[pl.ds(row0, ATTN_BLOCK), hh * LANES:(hh + 1) * LANES]
        for c, mi in enumerate((2 * hh, 2 * hh + 1)):
            k = k_head
            if spare is not None:
                lane = lax.broadcasted_iota(jnp.int32, (1, LANES), 1)
                k = jnp.where(lane < HEAD_DIM if c == 0 else lane >= HEAD_DIM, k_head, spare[c])
            if diagonal:
                s = jnp.concatenate([
                    jnp.concatenate(
                        [jnp.full((KEY_CHUNK, c * KEY_CHUNK), NEG_INF, F32)] * (c > 0)
                        + [_nn(k[c * KEY_CHUNK:(c + 1) * KEY_CHUNK], qx_ref[mi, :, c * KEY_CHUNK:])], axis=1)
                    for c in range(KEY_CHUNKS)], axis=0)
            else:
                s = _nn(k, qx_ref[mi])
            keep = keep_fn(mi)
            if keep is not None:
                s = jnp.where(keep, s, NEG_INF)
            s_ref[mi] = s
            tmax_ref[mi] = jnp.max(s, axis=0, keepdims=True)


def _softmax_stage(s_ref, tmax_ref, m_ref, alpha_ref, p_ref):
    for mi in range(2 * HEADS_PER_STEP):
        m_old = m_ref[mi]
        m_new = jnp.maximum(m_old, tmax_ref[mi])
        alpha_ref[mi] = jnp.exp2(m_old - m_new)
        m_ref[mi] = m_new
        p_ref[mi] = jnp.exp2(s_ref[mi] - m_new).astype(BF16)


def _values_stage(vt_of, tile, alpha_ref, p_ref, acc_ref):
    for mi in range(2 * HEADS_PER_STEP):
        vt = jnp.concatenate([vt_of(mi, tile * KEY_CHUNKS + ch) for ch in range(KEY_CHUNKS)], axis=1)
        acc_ref[mi] = alpha_ref[mi] * acc_ref[mi] + _nn(vt, p_ref[mi])


def _no_extra(*_):
    return None


def _query_tile(sub, tiles_per_step):
    return pl.program_id(2) * tiles_per_step + sub, slice(sub * ATTN_BLOCK, (sub + 1) * ATTN_BLOCK)


def _flash_pipeline(i, qx_ref, k_ref, vt_of, s_ref, tmax_ref, m_ref, alpha_ref, p_ref, acc_ref, diag_keep,
                    past_keep=_no_extra, diag_extra=_no_extra, past_extra=_no_extra, first_in_step=True):
    m_ref[...] = jnp.full(m_ref.shape, M_INIT, F32)

    if first_in_step:
        @pl.when((pl.program_id(0) == 0) & (pl.program_id(1) == 0) & (pl.program_id(2) == 0))
        def _init():
            acc_ref[...] = jnp.zeros(acc_ref.shape, F32)

    def tile_at(step):
        return jnp.where(step == 0, i, step - 1)

    def values(step):
        _values_stage(vt_of, tile_at(step), alpha_ref, p_ref, acc_ref)

    def softmax():
        _softmax_stage(s_ref, tmax_ref, m_ref, alpha_ref, p_ref)

    def scores(tile, keep_fn, k_extra, diagonal=False):
        _scores_stage(qx_ref, k_ref, s_ref, tmax_ref, tile, keep_fn, k_extra, diagonal)

    scores(i, diag_keep, diag_extra, diagonal=True)

    def body(t, carry):
        softmax()
        scores(t, functools.partial(past_keep, t), functools.partial(past_extra, t))
        values(t)
        return carry

    lax.fori_loop(0, i, body, 0)
    softmax()
    values(i)


def _flash_pipeline_pair(i, streams, k_ref, vt_of, diag_keep, first_in_step):
    for qx_ref, s_ref, tmax_ref, m_ref, alpha_ref, p_ref, acc_ref in streams:
        m_ref[...] = jnp.full(m_ref.shape, M_INIT, F32)
        if first_in_step:
            @pl.when((pl.program_id(0) == 0) & (pl.program_id(1) == 0) & (pl.program_id(2) == 0))
            def _init():
                acc_ref[...] = jnp.zeros(acc_ref.shape, F32)

    def softmax(st):
        qx_ref, s_ref, tmax_ref, m_ref, alpha_ref, p_ref, acc_ref = streams[st]
        _softmax_stage(s_ref, tmax_ref, m_ref, alpha_ref, p_ref)

    def scores(st, tile, keep_fn=_no_extra, diagonal=False):
        qx_ref, s_ref, tmax_ref, m_ref, alpha_ref, p_ref, acc_ref = streams[st]
        _scores_stage(qx_ref, k_ref, s_ref, tmax_ref, tile, keep_fn, _no_extra, diagonal)

    def values(st, tile):
        qx_ref, s_ref, tmax_ref, m_ref, alpha_ref, p_ref, acc_ref = streams[st]
        _values_stage(vt_of, tile, alpha_ref, p_ref, acc_ref)

    def tile_at(step):
        return jnp.where(step == 0, i, step - 1)

    scores(0, i, diag_keep, diagonal=True)
    scores(1, i + 1, diag_keep, diagonal=True)
    softmax(1)
    scores(1, i)
    values(1, i + 1)

    def body(t, carry):
        softmax(0)
        softmax(1)
        scores(0, t)
        scores(1, t)
        values(0, tile_at(t))
        values(1, tile_at(t))
        return carry

    lax.fori_loop(0, i, body, 0)
    softmax(0)
    softmax(1)
    values(0, tile_at(i))
    values(1, tile_at(i))


def _tile_iotas():
    shape = (ATTN_BLOCK, ATTN_BLOCK)
    return lax.broadcasted_iota(jnp.int32, shape, 0), lax.broadcasted_iota(jnp.int32, shape, 1)


def _flash_scratch(acc_rows):
    maps = 2 * HEADS_PER_STEP
    blk = ATTN_BLOCK
    stat = pltpu.VMEM((maps, 1, blk), F32)
    return [pltpu.VMEM((maps, LANES, blk), BF16), pltpu.VMEM((maps, blk, blk), F32), stat, stat, stat,
            pltpu.VMEM((maps, blk, blk), BF16), pltpu.VMEM((maps, acc_rows, blk), F32)]


def _flash_specs(seq, q_tiles, vt_rows, k_buffers=2):
    width = LANES * HEADS_PER_STEP
    nqb = seq // (q_tiles * ATTN_BLOCK)
    q_spec = pl.BlockSpec((q_tiles * ATTN_BLOCK, width), lambda b, h, i: (b * nqb + i, h))
    k_spec = pl.BlockSpec((seq, width), lambda b, h, i: (b, h), pipeline_mode=pl.Buffered(k_buffers))
    vt_spec = pl.BlockSpec((1, seq // KEY_CHUNK, vt_rows * HEADS_PER_STEP, KEY_CHUNK), lambda b, h, i: (b, 0, h, 0),
                           pipeline_mode=pl.Buffered(1))
    return q_spec, k_spec, vt_spec


def _key_tile_copies(k_hbm, vt_hbm, k_ref, vt_ref, k_sem, v_sem, tile):
    b, hg = pl.program_id(0), pl.program_id(1)
    seq, width = k_ref.shape
    rows = vt_ref.shape[1]
    k_copy = pltpu.make_async_copy(
        k_hbm.at[pl.ds(pl.multiple_of(b * seq + tile * ATTN_BLOCK, ATTN_BLOCK), ATTN_BLOCK),
                 pl.ds(pl.multiple_of(hg * width, LANES), width)],
        k_ref.at[pl.ds(pl.multiple_of(tile * ATTN_BLOCK, ATTN_BLOCK), ATTN_BLOCK)], k_sem.at[tile])
    v_copy = pltpu.make_async_copy(
        vt_hbm.at[b, pl.ds(tile * KEY_CHUNKS, KEY_CHUNKS), pl.ds(pl.multiple_of(hg * rows, BF16_ROWS), rows)],
        vt_ref.at[pl.ds(tile * KEY_CHUNKS, KEY_CHUNKS)], v_sem.at[tile])
    return k_copy, v_copy


def _fetch_key_tiles(k_hbm, vt_hbm, k_ref, vt_ref, k_sem, v_sem, q_tiles):
    copies = functools.partial(_key_tile_copies, k_hbm, vt_hbm, k_ref, vt_ref, k_sem, v_sem)

    @pl.when(pl.program_id(2) == 0)
    def _start_fetch():
        for tile in range(k_ref.shape[0] // ATTN_BLOCK):
            for copy in copies(tile):
                copy.start()

    for sub in range(q_tiles):
        for copy in copies(pl.program_id(2) * q_tiles + sub):
            copy.wait()


def _key_fetch_scratch(seq, vt_rows):
    tiles = seq // ATTN_BLOCK
    return [pltpu.VMEM((seq, LANES * HEADS_PER_STEP), BF16),
            pltpu.VMEM((seq // KEY_CHUNK, vt_rows * HEADS_PER_STEP, KEY_CHUNK), BF16),
            pltpu.SemaphoreType.DMA((tiles,)), pltpu.SemaphoreType.DMA((tiles,))]


def _diff_attn_kernel(lq1_ref, lk1_ref, lq2_ref, lk2_ref, subg_ref, q_ref, k_hbm, vt_hbm, o_ref,
                      k_ref, vt_ref, k_sem, v_sem, *scratch, lambda_init):
    rows = LANES + BF16_ROWS
    streams = (scratch[:len(scratch) // 2], scratch[len(scratch) // 2:])
    _fetch_key_tiles(k_hbm, vt_hbm, k_ref, vt_ref, k_sem, v_sem, DIFF_Q_TILES)

    def vt_of(mi, chunk):
        return vt_ref[chunk, (mi // 2) * rows:(mi // 2 + 1) * rows]

    def diag_keep(mi):
        key, qry = _tile_iotas()
        return key <= qry

    lam = (jnp.exp(jnp.sum(lq1_ref[...] * lk1_ref[...], axis=1, keepdims=True))
           - jnp.exp(jnp.sum(lq2_ref[...] * lk2_ref[...], axis=1, keepdims=True)) + lambda_init)
    for pair in range(DIFF_Q_TILES // 2):
        tiles = [_query_tile(2 * pair + st, DIFF_Q_TILES) for st in range(2)]
        for st, (_, q_rows) in enumerate(tiles):
            _prepare_queries(_transposed_queries(q_ref.at[q_rows]), streams[st][0])
        _flash_pipeline_pair(tiles[0][0], streams, k_ref, vt_of, diag_keep, first_in_step=pair == 0)
        for st, (_, q_rows) in enumerate(tiles):
            acc_ref = streams[st][-1]
            for hh in range(HEADS_PER_STEP):
                a1, a2 = acc_ref[2 * hh], acc_ref[2 * hh + 1]
                o = a1[:LANES] / a1[LANES:LANES + 1] - lam * (a2[:LANES] / a2[LANES:LANES + 1])
                o = o * lax.rsqrt(jnp.mean(o * o, axis=0, keepdims=True) + NORM_EPS) * subg_ref[...]
                o_ref[q_rows, hh * LANES:(hh + 1) * LANES] = (o * (1.0 - lambda_init)).T.astype(o_ref.dtype)


def _diff_attention(q, k, vt, lam_params, sub_g, *, batch, lambda_init):
    t = q.shape[0]
    seq = t // batch
    nqb = seq // (DIFF_Q_TILES * ATTN_BLOCK)
    rows = LANES + BF16_ROWS
    small = pl.BlockSpec((1, HEAD_DIM), lambda b, h, i: (0, 0))
    q_spec, _, _ = _flash_specs(seq, DIFF_Q_TILES, rows)
    in_hbm = pl.BlockSpec(memory_space=pl.ANY)
    key_scratch = _key_fetch_scratch(seq, rows)
    return pl.pallas_call(
        functools.partial(_diff_attn_kernel, lambda_init=lambda_init),
        grid=(batch, PAIRS // HEADS_PER_STEP, nqb),
        in_specs=[small, small, small, small, pl.BlockSpec((LANES, 1), lambda b, h, i: (0, 0)),
                  q_spec, in_hbm, in_hbm],
        out_specs=q_spec,
        out_shape=jax.ShapeDtypeStruct((t, D_MODEL), BF16),
        scratch_shapes=key_scratch + _flash_scratch(rows) * 2,
        compiler_params=pltpu.CompilerParams(dimension_semantics=("arbitrary",) * 3, vmem_limit_bytes=VMEM_LIMIT),
    )(*[p.reshape(1, HEAD_DIM) for p in lam_params], sub_g.reshape(LANES, 1), q, k, vt)


def _moba_select(gate, own):
    nb, width = gate.shape
    groups = range(nb // SUBLANES)
    sub = lax.broadcasted_iota(jnp.int32, (SUBLANES, width), 0)
    ids = [sub + SUBLANES * r for r in groups]
    g = [jnp.where(ids[r] < own, gate[SUBLANES * r:SUBLANES * (r + 1)], NEG_INF) for r in groups]
    sel = [jnp.zeros((SUBLANES, width), F32) for _ in groups]

    def over_blocks(parts, op):
        x = functools.reduce(op, parts)
        for shift in (4, 2, 1):
            x = op(x, pltpu.roll(x, shift, 0))
        return x

    for t in range(MOBA_TOPK):
        best = over_blocks(g, jnp.maximum)
        first = over_blocks([jnp.where(g[r] == best, ids[r], nb) for r in groups], jnp.minimum)
        valid = jnp.where(t < own, 1.0, 0.0)
        for r in groups:
            hit = ids[r] == first
            sel[r] = jnp.maximum(sel[r], jnp.where(hit, valid, 0.0))
            g[r] = jnp.where(hit, NEG_INF, g[r])
    return jnp.concatenate(sel, axis=0)


def _moba_attn_kernel(q_ref, k_hbm, vt_hbm, km_ref, o_ref, k_ref, vt_ref, k_sem, v_sem,
                      sel_ref, qx_ref, s_ref, tmax_ref, m_ref, alpha_ref, p_ref, acc_ref):
    _fetch_key_tiles(k_hbm, vt_hbm, k_ref, vt_ref, k_sem, v_sem, MOBA_Q_TILES)
    for sub in range(MOBA_Q_TILES):
        _moba_query_tile(sub, q_ref, k_ref, vt_ref, km_ref, o_ref,
                         sel_ref, qx_ref, s_ref, tmax_ref, m_ref, alpha_ref, p_ref, acc_ref)


def _moba_query_tile(sub, q_ref, k_ref, vt_ref, km_ref, o_ref,
                     sel_ref, qx_ref, s_ref, tmax_ref, m_ref, alpha_ref, p_ref, acc_ref):
    i, q_rows = _query_tile(sub, MOBA_Q_TILES)
    half = HEAD_DIM + BF16_ROWS
    maps = 2 * HEADS_PER_STEP
    nb = sel_ref.shape[1]
    qts = _transposed_queries(q_ref.at[q_rows])
    gates = []
    for hh, qt in enumerate(qts):
        q_hi = qt.astype(BF16)
        q_lo = (qt - q_hi.astype(F32)).astype(BF16)
        for km in _lo_hi(km_ref[0, :, hh * LANES:(hh + 1) * LANES]):
            km_hi = km.astype(BF16)
            km_lo = (km - km_hi.astype(F32)).astype(BF16)
            gates.append(_nn(km_hi, q_hi) + _nn(km_hi, q_lo) + _nn(km_lo, q_hi))
    own = i * KEY_CHUNKS + (lax.broadcasted_iota(jnp.int32, (1, maps * ATTN_BLOCK), 1) % ATTN_BLOCK) // MOBA_BLOCK
    sel = _moba_select(jnp.concatenate(gates, axis=1), own)
    for mi in range(maps):
        sel_ref[mi] = sel[:, mi * ATTN_BLOCK:(mi + 1) * ATTN_BLOCK]

    def vt_of(mi, chunk):
        return vt_ref[chunk, mi * half:(mi + 1) * half]

    def selected(block, mi):
        return sel_ref[mi, pl.ds(block, 1), :] > 0.0

    def diag_keep(mi):
        key, qry = _tile_iotas()
        key_blk, qry_blk = key // MOBA_BLOCK, qry // MOBA_BLOCK
        keep = (key <= qry) & (key_blk == qry_blk)
        for ch in range(KEY_CHUNKS - 1):
            keep = keep | ((key_blk == ch) & (qry_blk > ch) & selected(i * KEY_CHUNKS + ch, mi))
        return keep

    def spare_base(c):
        return HEAD_DIM * (1 - c)

    def bias_rows(mi):
        rows = jnp.where(sel_ref[mi] > 0.0, 0.0, MASK_BIAS)
        above = spare_base(mi % 2)
        return jnp.concatenate([jnp.zeros((above, ATTN_BLOCK), F32)] * (above > 0) + [rows]
                               + [jnp.zeros((LANES - above - nb, ATTN_BLOCK), F32)], axis=0)

    def block_marker(t):
        lane = lax.broadcasted_iota(jnp.int32, (MOBA_BLOCK, LANES), 1)
        return [jnp.concatenate([jnp.where(lane == spare_base(c) + t * KEY_CHUNKS + ch, 1.0, 0.0).astype(BF16)
                                 for ch in range(KEY_CHUNKS)], axis=0) for c in range(2)]

    def no_marker():
        return [jnp.zeros((ATTN_BLOCK, LANES), BF16)] * 2

    _prepare_queries(qts, qx_ref, bias_rows)
    _flash_pipeline(i, qx_ref, k_ref, vt_of, s_ref, tmax_ref, m_ref, alpha_ref, p_ref, acc_ref, diag_keep,
                    diag_extra=no_marker, past_extra=block_marker, first_in_step=sub == 0)

    for hh in range(HEADS_PER_STEP):
        a1, a2 = acc_ref[2 * hh], acc_ref[2 * hh + 1]
        o = jnp.concatenate([a1[:HEAD_DIM] / a1[HEAD_DIM:HEAD_DIM + 1], a2[:HEAD_DIM] / a2[HEAD_DIM:HEAD_DIM + 1]],
                            axis=0)
        o_ref[q_rows, hh * LANES:(hh + 1) * LANES] = o.T.astype(o_ref.dtype)


def _moba_attention(q, k, vt, kmean, *, batch):
    t = q.shape[0]
    seq = t // batch
    nqb = seq // (MOBA_Q_TILES * ATTN_BLOCK)
    nb = seq // MOBA_BLOCK
    half = HEAD_DIM + BF16_ROWS
    q_spec, _, _ = _flash_specs(seq, MOBA_Q_TILES, 2 * half)
    in_hbm = pl.BlockSpec(memory_space=pl.ANY)
    return pl.pallas_call(
        _moba_attn_kernel,
        grid=(batch, PAIRS // HEADS_PER_STEP, nqb),
        in_specs=[q_spec, in_hbm, in_hbm,
                  pl.BlockSpec((1, nb, LANES * HEADS_PER_STEP), lambda b, h, i: (b, 0, h))],
        out_specs=q_spec,
        out_shape=jax.ShapeDtypeStruct((t, D_MODEL), BF16),
        scratch_shapes=(_key_fetch_scratch(seq, 2 * half)
                        + [pltpu.VMEM((2 * HEADS_PER_STEP, nb, ATTN_BLOCK), F32)] + _flash_scratch(half)),
        compiler_params=pltpu.CompilerParams(dimension_semantics=("arbitrary",) * 3, vmem_limit_bytes=VMEM_LIMIT),
    )(q, k, vt, kmean)


def _swa_attn_kernel(sink_ref, q_ref, k_ref, vt_ref, o_ref):
    h = pl.program_id(1)
    i = pl.program_id(2)
    sub, w = SWA_SUB_BLOCK, SWA_WINDOW
    subs = q_ref.shape[0] // sub
    key = lax.broadcasted_iota(jnp.int32, (sub, sub), 0)
    qry = lax.broadcasted_iota(jnp.int32, (sub, sub), 1)
    own_keep = (key <= qry) & (qry - key < w)
    pkey = lax.broadcasted_iota(jnp.int32, (w, sub), 0)
    pqry = lax.broadcasted_iota(jnp.int32, (w, sub), 1)
    rows = HEAD_DIM + BF16_ROWS
    row0 = pl.multiple_of((h // (PAIRS // SWA_KV_HEADS)) * rows, BF16_ROWS)

    sinks = [sink_ref[2 * h + half] * LOG2_E for half in range(2)]
    chains = []
    for sb in range(subs):
        blk = i * subs + sb
        q = q_ref[sb * sub:(sb + 1) * sub, :]
        q0 = pl.multiple_of(blk * sub, sub)
        prev0 = pl.multiple_of(jnp.maximum(q0 - w, 0), w)
        k_own = _lo_hi(k_ref[pl.ds(q0, sub), :])
        k_prev = _lo_hi(k_ref[pl.ds(prev0, w), :])
        prev_keep = pkey > pqry
        if sb == 0:
            prev_keep = pkey > pqry + jnp.where(i > 0, 0, w)
        for half in range(2):
            s_own = jnp.where(own_keep, _nt(k_own[half], q), NEG_INF)
            s_prev = jnp.where(prev_keep, _nt(k_prev[half], q), NEG_INF)
            m = jnp.maximum(jnp.maximum(jnp.max(s_own, axis=0, keepdims=True),
                                        jnp.max(s_prev, axis=0, keepdims=True)), sinks[half])
            chains.append((s_own, s_prev, m))

    probs = [(jnp.exp2(s_own - m).astype(BF16), jnp.exp2(s_prev - m).astype(BF16)) for s_own, s_prev, m in chains]

    for sb in range(subs):
        chunk0 = (i * subs + sb) * (sub // w)
        vt_prev = vt_ref[0, jnp.maximum(chunk0 - 1, 0), pl.ds(row0, rows), :]
        vt_own = jnp.concatenate([vt_ref[0, chunk0 + c, pl.ds(row0, rows), :] for c in range(sub // w)], axis=1)
        outs = []
        for half in range(2):
            p_own, p_prev = probs[2 * sb + half]
            acc = _nn(vt_own, p_own) + _nn(vt_prev, p_prev)
            denom = acc[HEAD_DIM:HEAD_DIM + 1] + jnp.exp2(sinks[half] - chains[2 * sb + half][2])
            outs.append(acc[:HEAD_DIM] / denom)
        o_ref[sb * sub:(sb + 1) * sub, :] = jnp.concatenate(outs, axis=0).T.astype(o_ref.dtype)


def _swa_attention(q, kdup, vt, sinks, *, batch):
    t = q.shape[0]
    seq = t // batch
    qb = min(SWA_Q_BLOCK, seq)
    nqb = seq // qb
    per_kv = PAIRS // SWA_KV_HEADS
    vt_rows = SWA_KV_HEADS * (HEAD_DIM + BF16_ROWS)
    return pl.pallas_call(
        _swa_attn_kernel,
        grid=(batch, PAIRS, nqb),
        in_specs=[pl.BlockSpec(memory_space=pltpu.SMEM),
                  pl.BlockSpec((qb, LANES), lambda b, h, i: (b * nqb + i, h)),
                  pl.BlockSpec((seq, LANES), lambda b, h, i: (b, h // per_kv)),
                  pl.BlockSpec((1, seq // SWA_WINDOW, vt_rows, SWA_WINDOW), lambda b, h, i: (b, 0, 0, 0))],
        out_specs=pl.BlockSpec((qb, LANES), lambda b, h, i: (b * nqb + i, h)),
        out_shape=jax.ShapeDtypeStruct((t, D_MODEL), BF16),
        compiler_params=pltpu.CompilerParams(dimension_semantics=("arbitrary",) * 3, vmem_limit_bytes=VMEM_LIMIT),
    )(sinks, q, kdup, vt)


def _post_kernel(*refs, final):
    h_ref, o_ref, wout_ref, g_ref, wup_ref, wdn_ref = refs[:6]
    fg_ref = refs[6] if final else None
    out_ref = refs[-1]
    h1 = h_ref[...] + _nn(o_ref[...], wout_ref[...])
    xn = _rms(h1, g_ref[...]).astype(BF16)
    acc = h1
    for c in range(D_FF // FF_CHUNK):
        lo = c * FF_CHUNK
        a = jnp.maximum(_nn(xn, wup_ref[:, lo:lo + FF_CHUNK]), 0.0)
        acc = acc + _nn((a * a).astype(BF16), wdn_ref[lo:lo + FF_CHUNK, :])
    if final:
        acc = _rms(acc, fg_ref[...])
    out_ref[...] = acc


def _post(h, o, w_out, g, w_up, w_down, layer, final_g=None):
    t = h.shape[0]
    rows = POST_ROWS
    const = lambda i: (0, 0)
    row = lambda i: (i, 0)
    pick = lambda i: (layer, 0, 0)
    once = pl.Buffered(1)
    in_specs = [pl.BlockSpec((rows, D_MODEL), row), pl.BlockSpec((rows, D_MODEL), row),
                pl.BlockSpec(w_out.shape, const, pipeline_mode=once), pl.BlockSpec((1, D_MODEL), const),
                pl.BlockSpec((None,) + w_up.shape[1:], pick, pipeline_mode=once),
                pl.BlockSpec((None,) + w_down.shape[1:], pick, pipeline_mode=once)]
    args = [h, o, w_out, g.reshape(1, D_MODEL), w_up, w_down]
    if final_g is not None:
        in_specs.append(pl.BlockSpec((1, D_MODEL), const))
        args.append(final_g.reshape(1, D_MODEL))
    return pl.pallas_call(
        functools.partial(_post_kernel, final=final_g is not None),
        grid=(t // rows,),
        in_specs=in_specs,
        out_specs=pl.BlockSpec((rows, D_MODEL), row),
        out_shape=jax.ShapeDtypeStruct((t, D_MODEL), F32),
        compiler_params=pltpu.CompilerParams(dimension_semantics=("arbitrary",), vmem_limit_bytes=VMEM_LIMIT),
    )(*args)


def kernel(x, positions, attn_norm_g, mlp_norm_g, diff_w_in, diff_w_out, diff_lam_q1, diff_lam_k1, diff_lam_q2,
           diff_lam_k2, diff_subln_g, moba_w_in, moba_w_out, swa_w_in, swa_b_in, swa_sinks, swa_w_out, mlp_w_up,
           mlp_w_down, final_norm_g):
    batch, seq, _ = x.shape
    depth = attn_norm_g.shape[0]
    tables = _rope_tables(positions)
    h = x.reshape(batch * seq, D_MODEL)
    qk = 2 * D_MODEL
    log2_scale = Q_SCALE * LOG2_E
    w_up_all, w_down_all = mlp_w_up.astype(BF16), mlp_w_down.astype(BF16)

    for i in range(depth):
        mixer, slot = i % N_MIXERS, i // N_MIXERS
        if mixer == 0:
            w_in = diff_w_in[slot]
            q, k, vt = _project(h, attn_norm_g[i], w_in[:, :qk].astype(BF16), w_in[:, qk:].T.astype(BF16), tables,
                                batch=batch, nq=D_MODEL, kv_block=KEY_CHUNK, q_scale=log2_scale, v_group=LANES)
            o = _diff_attention(q, k, vt,
                                (diff_lam_q1[slot], diff_lam_k1[slot], diff_lam_q2[slot], diff_lam_k2[slot]),
                                diff_subln_g[slot], batch=batch, lambda_init=0.8 - 0.6 * math.exp(-0.3 * i))
            w_out = diff_w_out[slot]
        elif mixer == 1:
            w_in = moba_w_in[slot]
            q, k, vt, kmean = _project(h, attn_norm_g[i], w_in[:, :qk].astype(BF16), w_in[:, qk:].T.astype(BF16),
                                       tables, batch=batch, nq=D_MODEL, kv_block=KEY_CHUNK, q_scale=log2_scale,
                                       v_group=HEAD_DIM, q_dtype=F32, with_kmean=True)
            o = _moba_attention(q, k, vt, kmean.reshape(batch, seq // MOBA_BLOCK, D_MODEL), batch=batch)
            w_out = moba_w_out[slot]
        else:
            w_in, b_in = swa_w_in[slot], swa_b_in[slot]
            dup = lambda a: jnp.concatenate(
                [a[..., D_MODEL + kv * HEAD_DIM:D_MODEL + (kv + 1) * HEAD_DIM]
                 for kv in range(SWA_KV_HEADS) for _ in range(2)], axis=-1)
            v0 = D_MODEL + SWA_KV_HEADS * HEAD_DIM
            wqk = jnp.concatenate([w_in[:, :D_MODEL], dup(w_in)], axis=1).astype(BF16)
            bqk = jnp.concatenate([b_in[:D_MODEL], dup(b_in)])
            q, k, vt = _project(h, attn_norm_g[i], wqk, w_in[:, v0:].T.astype(BF16), tables, batch=batch,
                                nq=D_MODEL, kv_block=SWA_WINDOW, q_scale=log2_scale, v_group=HEAD_DIM,
                                bias=(bqk, b_in[v0:]))
            o = _swa_attention(q, k, vt, swa_sinks[slot], batch=batch)
            w_out = swa_w_out[slot]
        h = _post(h, o, w_out.astype(BF16), mlp_norm_g[i], w_up_all, w_down_all, i,
                  final_norm_g if i == depth - 1 else None)
    return h.reshape(batch, seq, D_MODEL)
```
